```python
import math
import jax, jax.numpy as jnp
from jax import lax
import numpy as np

D_MODEL = 1024
BATCH = 8
SEQ = 2048
DEPTH = 2
DEC_BATCH = 128
DEC_SEQ = 8
PAST_LEN = 16384
PAGE_SIZE = 128

SSM_GROUP_CH = 16
SSM_WIDTH = D_MODEL // 2
SSM_GROUPS = SSM_WIDTH // SSM_GROUP_CH
SSM_STATE = 64
DT_MIN = 1e-3
DT_MAX = 1e-1
GMLP_HEADS = 4
GMLP_CHUNK = 128
GMLP_WIDTH = D_MODEL // 4
GMLP_HEAD_DIM = GMLP_WIDTH // GMLP_HEADS
POOL_WINDOWS = (2, 4, 8, 16)
POOL_GROUPS = len(POOL_WINDOWS)
POOL_WIDTH = D_MODEL // 2
POOL_GROUP_CH = POOL_WIDTH // POOL_GROUPS
POOL_PAST = max(POOL_WINDOWS) - 1
N_BRANCH = 3
IN_COLS = SSM_WIDTH + 2 * GMLP_WIDTH + POOL_WIDTH + N_BRANCH * D_MODEL
D_FF = ((8 * D_MODEL) // 3 + 127) // 128 * 128
FFN_CONV = 3
LN_EPS = 1e-5
ALPHA = (2 * DEPTH) ** 0.25
BETA = (8 * DEPTH) ** -0.25

kernel_name = 'hybrid_s5_gmlp_pool_convffn_step'


def layer_norm(x, g, b):
    xf = x.astype(jnp.float32)
    mu = jnp.mean(xf, axis=-1, keepdims=True)
    var = jnp.mean(jnp.square(xf - mu), axis=-1, keepdims=True)
    y = (xf - mu) * lax.rsqrt(var + LN_EPS) * g.astype(jnp.float32) + b.astype(jnp.float32)
    return y.astype(x.dtype)


def ssm_discretise(a_re, a_im, log_step, b_re, b_im):
    f32 = jnp.float32
    a_re = a_re.astype(f32)
    a_im = a_im.astype(f32)
    dt = jnp.exp(log_step.astype(f32))[:, None]
    mag = jnp.exp(a_re * dt)
    ab_re = mag * jnp.cos(a_im * dt)
    ab_im = mag * jnp.sin(a_im * dt)
    den = jnp.square(a_re) + jnp.square(a_im)
    nr = ab_re - 1.0
    k_re = (nr * a_re + ab_im * a_im) / den
    k_im = (ab_im * a_re - nr * a_im) / den
    b_re = b_re.astype(f32)
    b_im = b_im.astype(f32)
    bb_re = k_re[..., None] * b_re - k_im[..., None] * b_im
    bb_im = k_re[..., None] * b_im + k_im[..., None] * b_re
    return ab_re, ab_im, bb_re, bb_im


def complex_affine_combine(e1, e2):
    a1r, a1i, b1r, b1i = e1
    a2r, a2i, b2r, b2i = e2
    return (a2r * a1r - a2i * a1i,
            a2r * a1i + a2i * a1r,
            a2r * b1r - a2i * b1i + b2r,
            a2r * b1i + a2i * b1r + b2i)


def ssm_branch(u, h0_re, h0_im, a_re, a_im, log_step, b_re, b_im, c_re, c_im, d_skip, w_glu, b_glu):
    f32 = jnp.float32
    nb, t, _ = u.shape
    uf = u.astype(f32).reshape(nb, t, SSM_GROUPS, SSM_GROUP_CH)
    ab_re, ab_im, bb_re, bb_im = ssm_discretise(a_re, a_im, log_step, b_re, b_im)
    bu_re = jnp.einsum('gnc,btgc->btgn', bb_re, uf)
    bu_im = jnp.einsum('gnc,btgc->btgn', bb_im, uf)
    a_r = jnp.broadcast_to(ab_re, bu_re.shape)
    a_i = jnp.broadcast_to(ab_im, bu_im.shape)
    p_re, p_im, h_re, h_im = lax.associative_scan(
        complex_affine_combine, (a_r, a_i, bu_re, bu_im), axis=1)
    if h0_re is not None:
        s_re = h0_re.astype(f32)[:, None]
        s_im = h0_im.astype(f32)[:, None]
        h_re, h_im = (h_re + p_re * s_re - p_im * s_im,
                      h_im + p_re * s_im + p_im * s_re)
    y = (jnp.einsum('gcn,btgn->btgc', c_re.astype(f32), h_re)
         - jnp.einsum('gcn,btgn->btgc', c_im.astype(f32), h_im)
         + d_skip.astype(f32) * uf).reshape(nb, t, SSM_WIDTH)
    z = jax.nn.gelu(y)
    out = z * jax.nn.sigmoid(z @ w_glu.astype(f32) + b_glu.astype(f32))
    return out.astype(u.dtype), h_re[:, -1], h_im[:, -1]


def gmlp_branch(u, v, ln_g, ln_b, w_s, b_s):
    nb, t, _ = v.shape
    vn = layer_norm(v, ln_g, ln_b)
    cl = min(t, GMLP_CHUNK)
    nc = t // cl
    mask = jnp.tril(jnp.ones((cl, cl), dtype=bool))
    w = jnp.where(mask, w_s[:, :cl, :cl], 0)
    vh = vn.reshape(nb, nc, cl, GMLP_HEADS, GMLP_HEAD_DIM)
    s = (jnp.einsum('hts,bcshd->bcthd', w, vh)
         + jnp.transpose(b_s[:, :cl])[None, None, :, :, None])
    return u * s.reshape(nb, t, GMLP_WIDTH), vn


def pool_branch(xp, past, start_pos, w_pool, scale):
    f32 = jnp.float32
    nb, t, _ = xp.shape
    if past is None:
        past = jnp.zeros((nb, POOL_PAST, POOL_WIDTH), xp.dtype)
    xf = jnp.concatenate([past.astype(f32), xp.astype(f32)], axis=1)
    cs = jnp.concatenate([jnp.zeros((nb, 1, POOL_WIDTH), f32),
                          jnp.cumsum(xf, axis=1)], axis=1)
    pos = start_pos + jnp.arange(t, dtype=jnp.int32)
    means = []
    for g, win in enumerate(POOL_WINDOWS):
        ch = slice(g * POOL_GROUP_CH, (g + 1) * POOL_GROUP_CH)
        lo = POOL_PAST + 1 - win
        wsum = cs[:, POOL_PAST + 1:, ch] - cs[:, lo:lo + t, ch]
        cnt = jnp.minimum(pos + 1, win).astype(f32)[None, :, None]
        means.append(wsum / cnt)
    d = (jnp.concatenate(means, axis=-1) - xf[:, POOL_PAST:]).reshape(
        nb, t, POOL_GROUPS, POOL_GROUP_CH)
    y = jnp.einsum('btgc,gcd->btgd', d, w_pool.astype(f32)).reshape(nb, t, POOL_WIDTH)
    y = y * scale.astype(f32)
    return y.astype(xp.dtype), xf[:, -POOL_PAST:].astype(xp.dtype)


def conv_ffn(x, past, w_up, conv_w, conv_b, w_down):
    nb, t, _ = x.shape
    h = x @ w_up
    if past is None:
        past = jnp.zeros((nb, FFN_CONV - 1, 2 * D_FF), h.dtype)
    hc = jnp.concatenate([past.astype(h.dtype), h], axis=1)
    y = conv_b + hc[:, 0:t] * conv_w[0]
    for k in range(1, FFN_CONV):
        y = y + hc[:, k:k + t] * conv_w[k]
    val, gate = y[..., :D_FF], y[..., D_FF:]
    out = (jax.nn.gelu(gate) * val) @ w_down
    return out, hc[:, -(FFN_CONV - 1):]


def token_mix(x, st_re, st_im, st_pool, start_pos, p):
    nb, t, _ = x.shape
    proj = x @ p['w_in']
    o1 = SSM_WIDTH
    o2 = o1 + GMLP_WIDTH
    o3 = o2 + GMLP_WIDTH
    o4 = o3 + POOL_WIDTH
    y_a, h_re, h_im = ssm_branch(proj[..., :o1], st_re, st_im, p['ssm_a_re'], p['ssm_a_im'],
                                 p['ssm_log_step'], p['ssm_b_re'], p['ssm_b_im'],
                                 p['ssm_c_re'], p['ssm_c_im'], p['ssm_d'],
                                 p['ssm_w_glu'], p['ssm_b_glu'])
    y_b, v_rows = gmlp_branch(jax.nn.gelu(proj[..., o1:o2]), jax.nn.gelu(proj[..., o2:o3]),
                              p['gmlp_ln_g'], p['gmlp_ln_b'], p['gmlp_w_s'], p['gmlp_b_s'])
    y_c, pool_buf = pool_branch(proj[..., o3:o4], st_pool, start_pos, p['pool_w'], p['pool_scale'])
    gates = jax.nn.sigmoid(proj[..., o4:].reshape(nb, t, N_BRANCH, D_MODEL) + p['b_gate'])
    merged = (gates[:, :, 0] * (y_a @ p['w_br_ssm'])
              + gates[:, :, 1] * (y_b @ p['w_br_gmlp'])
              + gates[:, :, 2] * (y_c @ p['w_br_pool']))
    return merged @ p['w_o'], h_re, h_im, v_rows, pool_buf


def trunk_layer(x, st_re, st_im, st_pool, st_conv, start_pos, p):
    mix, h_re, h_im, v_rows, pool_buf = token_mix(x, st_re, st_im, st_pool, start_pos, p)
    x1 = layer_norm(ALPHA * x + mix, p['ln1_g'], p['ln1_b'])
    ff, conv_buf = conv_ffn(x1, st_conv, p['ffn_w_up'], p['ffn_conv_w'], p['ffn_conv_b'], p['ffn_w_down'])
    x2 = layer_norm(ALPHA * x1 + ff, p['ln2_g'], p['ln2_b'])
    return x2, h_re, h_im, v_rows, pool_buf, conv_buf


def setup_inputs(seed: int = 0) -> dict:
    key = jax.random.key(seed)
    ks = iter(jax.random.split(key, 40))

    def nrm(shape, scale=1.0):
        return jax.random.normal(next(ks), shape, jnp.float32) * scale

    L = DEPTH
    G, N, C = SSM_GROUPS, SSM_STATE, SSM_GROUP_CH
    F2 = 2 * D_FF
    return {
        'x_prompt': nrm((BATCH, SEQ, D_MODEL)),
        'x_sample': nrm((DEC_BATCH, DEC_SEQ, D_MODEL)),
        'state_ssm_re': nrm((L, DEC_BATCH, G, N), 0.1),
        'state_ssm_im': nrm((L, DEC_BATCH, G, N), 0.1),
        'state_pool': nrm((L, DEC_BATCH, POOL_PAST, POOL_WIDTH)),
        'state_ffn_conv': nrm((L, DEC_BATCH, FFN_CONV - 1, F2)),
        'w_in': nrm((L, D_MODEL, IN_COLS), D_MODEL ** -0.5),
        'b_gate': nrm((L, N_BRANCH, D_MODEL), 0.01),
        'ssm_a_re': -0.5 + nrm((L, G, N), 0.01),
        'ssm_a_im': math.pi * jnp.arange(N, dtype=jnp.float32) + nrm((L, G, N), 0.01),
        'ssm_log_step': jax.random.uniform(next(ks), (L, G), jnp.float32,
                                           math.log(DT_MIN), math.log(DT_MAX)),
        'ssm_b_re': nrm((L, G, N, C), (2 * C) ** -0.5),
        'ssm_b_im': nrm((L, G, N, C), (2 * C) ** -0.5),
        'ssm_c_re': nrm((L, G, C, N), (2 * N) ** -0.5),
        'ssm_c_im': nrm((L, G, C, N), (2 * N) ** -0.5),
        'ssm_d': nrm((L, G, C)),
        'ssm_w_glu': nrm((L, SSM_WIDTH, SSM_WIDTH), SSM_WIDTH ** -0.5),
        'ssm_b_glu': nrm((L, SSM_WIDTH), 0.01),
        'gmlp_ln_g': 1.0 + nrm((L, GMLP_WIDTH), 0.05),
        'gmlp_ln_b': nrm((L, GMLP_WIDTH), 0.01),
        'gmlp_w_s': nrm((L, GMLP_HEADS, GMLP_CHUNK, GMLP_CHUNK), GMLP_CHUNK ** -0.5),
        'gmlp_b_s': 1.0 + nrm((L, GMLP_HEADS, GMLP_CHUNK), 0.1),
        'pool_w': nrm((L, POOL_GROUPS, POOL_GROUP_CH, POOL_GROUP_CH), POOL_GROUP_CH ** -0.5),
        'pool_scale': 1.0 + nrm((L, POOL_WIDTH), 0.1),
        'w_br_ssm': nrm((L, SSM_WIDTH, D_MODEL), SSM_WIDTH ** -0.5),
        'w_br_gmlp': nrm((L, GMLP_WIDTH, D_MODEL), GMLP_WIDTH ** -0.5),
        'w_br_pool': nrm((L, POOL_WIDTH, D_MODEL), POOL_WIDTH ** -0.5),
        'w_o': nrm((L, D_MODEL, D_MODEL), BETA * D_MODEL ** -0.5),
        'ln1_g': 1.0 + nrm((L, D_MODEL), 0.05),
        'ln1_b': nrm((L, D_MODEL), 0.01),
        'ffn_w_up': nrm((L, D_MODEL, F2), D_MODEL ** -0.5),
        'ffn_conv_w': nrm((L, FFN_CONV, F2), FFN_CONV ** -0.5),
        'ffn_conv_b': nrm((L, F2), 0.01),
        'ffn_w_down': nrm((L, D_FF, D_MODEL), BETA * D_FF ** -0.5),
        'ln2_g': 1.0 + nrm((L, D_MODEL), 0.05),
        'ln2_b': nrm((L, D_MODEL), 0.01),
    }


def reference(x_prompt, x_sample, state_ssm_re, state_ssm_im, state_pool, state_ffn_conv,
              w_in, b_gate, ssm_a_re, ssm_a_im, ssm_log_step, ssm_b_re, ssm_b_im,
              ssm_c_re, ssm_c_im, ssm_d, ssm_w_glu, ssm_b_glu, gmlp_ln_g, gmlp_ln_b,
              gmlp_w_s, gmlp_b_s, pool_w, pool_scale, w_br_ssm, w_br_gmlp, w_br_pool, w_o,
              ln1_g, ln1_b, ffn_w_up, ffn_conv_w, ffn_conv_b, ffn_w_down, ln2_g, ln2_b):
    hp = x_prompt
    hs = x_sample
    p_re, p_im, p_pool, p_conv = [], [], [], []
    s_re, s_im, s_v, s_pool, s_conv = [], [], [], [], []
    for l in range(DEPTH):
        p = dict(w_in=w_in[l], b_gate=b_gate[l], ssm_a_re=ssm_a_re[l], ssm_a_im=ssm_a_im[l],
                 ssm_log_step=ssm_log_step[l], ssm_b_re=ssm_b_re[l], ssm_b_im=ssm_b_im[l],
                 ssm_c_re=ssm_c_re[l], ssm_c_im=ssm_c_im[l], ssm_d=ssm_d[l],
                 ssm_w_glu=ssm_w_glu[l], ssm_b_glu=ssm_b_glu[l],
                 gmlp_ln_g=gmlp_ln_g[l], gmlp_ln_b=gmlp_ln_b[l], gmlp_w_s=gmlp_w_s[l],
                 gmlp_b_s=gmlp_b_s[l], pool_w=pool_w[l], pool_scale=pool_scale[l],
                 w_br_ssm=w_br_ssm[l], w_br_gmlp=w_br_gmlp[l], w_br_pool=w_br_pool[l], w_o=w_o[l],
                 ln1_g=ln1_g[l], ln1_b=ln1_b[l], ffn_w_up=ffn_w_up[l], ffn_conv_w=ffn_conv_w[l],
                 ffn_conv_b=ffn_conv_b[l], ffn_w_down=ffn_w_down[l], ln2_g=ln2_g[l], ln2_b=ln2_b[l])
        hp, a_re_n, a_im_n, _, pool_n, conv_n = trunk_layer(hp, None, None, None, None, 0, p)
        p_re.append(a_re_n)
        p_im.append(a_im_n)
        p_pool.append(pool_n)
        p_conv.append(conv_n)
        hs, b_re_n, b_im_n, v_n, pool_m, conv_m = trunk_layer(
            hs, state_ssm_re[l], state_ssm_im[l], state_pool[l], state_ffn_conv[l], PAST_LEN, p)
        s_re.append(b_re_n)
        s_im.append(b_im_n)
        s_v.append(v_n)
        s_pool.append(pool_m)
        s_conv.append(conv_m)
    return (hp, hs,
            jnp.stack(p_re), jnp.stack(p_im), jnp.stack(p_pool), jnp.stack(p_conv),
            jnp.stack(s_re), jnp.stack(s_im), jnp.stack(s_v), jnp.stack(s_pool), jnp.stack(s_conv))
```

```python
import functools
import math

import jax
import jax.numpy as jnp
from jax import lax
from jax.experimental import pallas as pl
from jax.experimental.pallas import tpu as pltpu

F32 = jnp.float32
BF16 = jnp.bfloat16

D_MODEL = 1024
DEPTH = 2
PAST_LEN = 16384
SSM_GROUP_CH = 16
SSM_WIDTH = 512
SSM_GROUPS = 32
SSM_STATE = 64
SSM_COLS = SSM_GROUPS * SSM_STATE
GMLP_HEADS = 4
GMLP_CHUNK = 128
GMLP_WIDTH = 256
GMLP_HEAD_DIM = 64
POOL_WINDOWS = (2, 4, 8, 16)
POOL_WIDTH = 512
POOL_GROUP_CH = 128
POOL_PAST = 15
N_BRANCH = 3
BRANCH_COLS = SSM_WIDTH + 2 * GMLP_WIDTH + POOL_WIDTH
D_FF = 2816
F2 = 2 * D_FF
FFN_CONV = 3
LN_EPS = 1e-5
ALPHA = (2 * DEPTH) ** 0.25

SUBLANES = 8
MXU_DIM = 256
VMEM_LIMIT_BYTES = 60 * 1024 * 1024

TILE_ROWS = 1024
ROW_CHUNK = 256
SSM_KT = SSM_WIDTH // MXU_DIM
KT_COLS = SSM_COLS // SSM_KT
FF_COL_CHUNK = D_FF // 2


def _gelu(x):
    c = math.sqrt(2.0 / math.pi)
    return 0.5 * x * (1.0 + jnp.tanh(c * (x + 0.044715 * (x * x * x))))


def _sigmoid(x):
    return 1.0 / (1.0 + jnp.exp(-x))


def _layer_norm(x, g, b):
    mu = jnp.mean(x, axis=-1, keepdims=True)
    xc = x - mu
    var = jnp.mean(xc * xc, axis=-1, keepdims=True)
    return xc * lax.rsqrt(var + LN_EPS) * g + b


def _dot(a, b):
    return jnp.dot(a, b, preferred_element_type=F32)


def _whole(shape=None):
    return pl.BlockSpec(memory_space=pltpu.VMEM)


def _branches_kernel(x_ref, h0re_ref, h0im_ref, ppast_ref,
                     w_in_ref, bb_ref, are_ref, aim_ref, cc_ref, dsk_ref,
                     wglu_ref, bglu_ref, lng_ref, lnb_ref, wk_ref, bs_ref,
                     poolw_ref, pscale_ref,
                     ycat_ref, hfre_ref, hfim_ref, vn_out_ref, pool_out_ref,
                     hbuf, us, ya, gu, vn, xp, hcre, hcim,
                     *, nb, npos, n_tiles, start_pos):
    rows = nb * npos
    n_chunks = rows // ROW_CHUNK
    past_rows = POOL_PAST * nb
    tile = pl.program_id(0)

    @pl.when(tile == 0)
    def _init():
        hcre[...] = h0re_ref[...]
        hcim[...] = h0im_ref[...]
        xp[0:past_rows, :] = ppast_ref[...]

    def proj_chunk(c, carry):
        r0 = pl.multiple_of(c * ROW_CHUNK, ROW_CHUNK)
        xb = x_ref[pl.ds(r0, ROW_CHUNK), :].astype(BF16)
        pa = _dot(xb, w_in_ref[...])
        us[pl.ds(r0, ROW_CHUNK), :] = pa[:, 0:SSM_WIDTH]
        gu[pl.ds(r0, ROW_CHUNK), :] = _gelu(pa[:, SSM_WIDTH:SSM_WIDTH + GMLP_WIDTH])
        v = _gelu(pa[:, SSM_WIDTH + GMLP_WIDTH:SSM_WIDTH + 2 * GMLP_WIDTH])
        vn[pl.ds(r0, ROW_CHUNK), :] = _layer_norm(v, lng_ref[...], lnb_ref[...])
        xp[pl.ds(past_rows + r0, ROW_CHUNK), :] = pa[:, SSM_WIDTH + 2 * GMLP_WIDTH:BRANCH_COLS]
        return carry

    lax.fori_loop(0, n_chunks, proj_chunk, 0)

    for kt in range(SSM_KT):
        c_lo = kt * KT_COLS

        def bu_chunk(c, carry, kt=kt):
            r0 = pl.multiple_of(c * ROW_CHUNK, ROW_CHUNK)
            ub = us[pl.ds(r0, ROW_CHUNK), kt * MXU_DIM:(kt + 1) * MXU_DIM].astype(BF16)
            hbuf[pl.ds(r0, ROW_CHUNK), :] = _dot(ub, bb_ref[kt])
            return carry

        lax.fori_loop(0, n_chunks, bu_chunk, 0)

        half = KT_COLS // 2
        for ch in range(2):
            lo = ch * half
            a_r = jnp.broadcast_to(are_ref[:, c_lo + lo:c_lo + lo + half], (SUBLANES, half))
            a_i = jnp.broadcast_to(aim_ref[:, c_lo + lo:c_lo + lo + half], (SUBLANES, half))

            def seq_block(rb, carry, lo=lo, a_r=a_r, a_i=a_i, c_lo=c_lo):
                b0 = pl.multiple_of(rb * SUBLANES, SUBLANES)
                h_r = hcre[pl.ds(b0, SUBLANES), c_lo + lo:c_lo + lo + half]
                h_i = hcim[pl.ds(b0, SUBLANES), c_lo + lo:c_lo + lo + half]

                def step(t, h):
                    h_r, h_i = h
                    row = pl.multiple_of(t * nb + b0, SUBLANES)
                    b_r = hbuf[pl.ds(row, SUBLANES), lo:lo + half]
                    b_i = hbuf[pl.ds(row, SUBLANES), KT_COLS + lo:KT_COLS + lo + half]
                    n_r = a_r * h_r - a_i * h_i + b_r
                    n_i = a_r * h_i + a_i * h_r + b_i
                    hbuf[pl.ds(row, SUBLANES), lo:lo + half] = n_r
                    hbuf[pl.ds(row, SUBLANES), KT_COLS + lo:KT_COLS + lo + half] = n_i
                    return n_r, n_i

                h_r, h_i = lax.fori_loop(0, npos, step, (h_r, h_i))
                hcre[pl.ds(b0, SUBLANES), c_lo + lo:c_lo + lo + half] = h_r
                hcim[pl.ds(b0, SUBLANES), c_lo + lo:c_lo + lo + half] = h_i
                return carry

            lax.fori_loop(0, nb // SUBLANES, seq_block, 0)

        def y_chunk(c, carry, kt=kt):
            r0 = pl.multiple_of(c * ROW_CHUNK, ROW_CHUNK)
            hb = hbuf[pl.ds(r0, ROW_CHUNK), :].astype(BF16)
            u = us[pl.ds(r0, ROW_CHUNK), kt * MXU_DIM:(kt + 1) * MXU_DIM]
            y = _dot(hb, cc_ref[kt]) + dsk_ref[:, kt * MXU_DIM:(kt + 1) * MXU_DIM] * u
            ya[pl.ds(r0, ROW_CHUNK), kt * MXU_DIM:(kt + 1) * MXU_DIM] = y
            return carry

        lax.fori_loop(0, n_chunks, y_chunk, 0)

    hfre_ref[...] = hcre[...]
    hfim_ref[...] = hcim[...]

    vb = vn[...].astype(BF16)
    head = lax.broadcasted_iota(jnp.int32, (1, GMLP_WIDTH), 1) // GMLP_HEAD_DIM
    s = bs_ref[...]
    for h in range(GMLP_HEADS):
        s = s + jnp.where(head == h, _dot(wk_ref[h], vb), 0.0)
    ycat_ref[:, SSM_WIDTH:SSM_WIDTH + GMLP_WIDTH] = (gu[...] * s).astype(BF16)
    vn_out_ref[...] = vn[...]

    base_pos = start_pos + tile * npos
    shift = int(math.log2(nb))

    def out_chunk(c, carry):
        r0 = pl.multiple_of(c * ROW_CHUNK, ROW_CHUNK)
        z = _gelu(ya[pl.ds(r0, ROW_CHUNK), :])
        o_a = z * _sigmoid(_dot(z.astype(BF16), wglu_ref[...]) + bglu_ref[...])
        ycat_ref[pl.ds(r0, ROW_CHUNK), 0:SSM_WIDTH] = o_a.astype(BF16)

        ridx = lax.broadcasted_iota(jnp.int32, (ROW_CHUNK, 1), 0) + r0
        pos = base_pos + (ridx >> shift)
        for g, win in enumerate(POOL_WINDOWS):
            cl = g * POOL_GROUP_CH
            cur = xp[pl.ds(past_rows + r0, ROW_CHUNK), cl:cl + POOL_GROUP_CH]
            wsum = cur
            for j in range(1, win):
                off = pl.multiple_of(past_rows - j * nb + r0, SUBLANES)
                wsum = wsum + xp[pl.ds(off, ROW_CHUNK), cl:cl + POOL_GROUP_CH]
            cnt = jnp.minimum(pos + 1, win).astype(F32)
            d = wsum / cnt - cur
            y_c = _dot(d.astype(BF16), poolw_ref[g]) * pscale_ref[:, cl:cl + POOL_GROUP_CH]
            col = SSM_WIDTH + GMLP_WIDTH + cl
            ycat_ref[pl.ds(r0, ROW_CHUNK), col:col + POOL_GROUP_CH] = y_c.astype(BF16)
        return carry

    lax.fori_loop(0, n_chunks, out_chunk, 0)

    pool_out_ref[...] = xp[rows:rows + past_rows, :]
    if n_tiles > 1:
        xp[0:past_rows, :] = xp[rows:rows + past_rows, :]


def _branches_call(x, h0re, h0im, ppast, p, wk, bs, *, nb, npos, start_pos):
    n_rows = x.shape[0]
    rows = nb * npos
    n_tiles = n_rows // rows
    past_rows = POOL_PAST * nb
    kern = functools.partial(_branches_kernel, nb=nb, npos=npos, n_tiles=n_tiles, start_pos=start_pos)
    row_tile = lambda w: pl.BlockSpec((rows, w), lambda i: (i, 0))
    fixed = lambda r, w: pl.BlockSpec((r, w), lambda i: (0, 0))
    consts = (p['w_in_br'], p['bb'], p['a_re'], p['a_im'], p['cc'], p['d_skip'],
              p['w_glu'], p['b_glu'], p['gln_g'], p['gln_b'], wk, bs, p['pool_w'], p['pool_scale'])
    return pl.pallas_call(
        kern,
        grid=(n_tiles,),
        in_specs=[row_tile(D_MODEL), _whole(), _whole(), _whole()] + [_whole() for _ in consts],
        out_specs=[row_tile(BRANCH_COLS - GMLP_WIDTH), fixed(nb, SSM_COLS), fixed(nb, SSM_COLS),
                   row_tile(GMLP_WIDTH), fixed(past_rows, POOL_WIDTH)],
        out_shape=[jax.ShapeDtypeStruct((n_rows, BRANCH_COLS - GMLP_WIDTH), BF16),
                   jax.ShapeDtypeStruct((nb, SSM_COLS), F32),
                   jax.ShapeDtypeStruct((nb, SSM_COLS), F32),
                   jax.ShapeDtypeStruct((n_rows, GMLP_WIDTH), F32),
                   jax.ShapeDtypeStruct((past_rows, POOL_WIDTH), F32)],
        scratch_shapes=[pltpu.VMEM((rows, 2 * KT_COLS), F32),
                        pltpu.VMEM((rows, SSM_WIDTH), F32),
                        pltpu.VMEM((rows, SSM_WIDTH), F32),
                        pltpu.VMEM((rows, GMLP_WIDTH), F32),
                        pltpu.VMEM((rows, GMLP_WIDTH), F32),
                        pltpu.VMEM((rows + past_rows, POOL_WIDTH), F32),
                        pltpu.VMEM((nb, SSM_COLS), F32),
                        pltpu.VMEM((nb, SSM_COLS), F32)],
        compiler_params=pltpu.CompilerParams(dimension_semantics=("arbitrary",),
                                             vmem_limit_bytes=VMEM_LIMIT_BYTES),
        name="branches",
    )(x, h0re, h0im, ppast, *consts)


def _merge_kernel(x_ref, ycat_ref, wg_ref, bg_ref, wbr_ref, wo_ref, g_ref, b_ref, x1_ref):
    x = x_ref[...]
    xb = x.astype(BF16)
    yc = ycat_ref[...]
    bounds = (0, SSM_WIDTH, SSM_WIDTH + GMLP_WIDTH, BRANCH_COLS - GMLP_WIDTH)
    merged = None
    for i in range(N_BRANCH):
        gate = _sigmoid(_dot(xb, wg_ref[:, i * D_MODEL:(i + 1) * D_MODEL])
                        + bg_ref[:, i * D_MODEL:(i + 1) * D_MODEL])
        br = _dot(yc[:, bounds[i]:bounds[i + 1]], wbr_ref[bounds[i]:bounds[i + 1], :])
        merged = gate * br if merged is None else merged + gate * br
    mix = _dot(merged.astype(BF16), wo_ref[...])
    x1_ref[...] = _layer_norm(ALPHA * x + mix, g_ref[...], b_ref[...])


def _merge_call(x, ycat, p):
    n_rows = x.shape[0]
    rows = ROW_CHUNK
    consts = (p['w_in_gate'], p['b_gate'], p['w_br'], p['w_o'], p['ln1_g'], p['ln1_b'])
    return pl.pallas_call(
        _merge_kernel,
        grid=(n_rows // rows,),
        in_specs=[pl.BlockSpec((rows, D_MODEL), lambda i: (i, 0)),
                  pl.BlockSpec((rows, BRANCH_COLS - GMLP_WIDTH), lambda i: (i, 0))]
                 + [_whole() for _ in consts],
        out_specs=pl.BlockSpec((rows, D_MODEL), lambda i: (i, 0)),
        out_shape=jax.ShapeDtypeStruct((n_rows, D_MODEL), F32),
        compiler_params=pltpu.CompilerParams(dimension_semantics=("arbitrary",),
                                             vmem_limit_bytes=VMEM_LIMIT_BYTES),
        name="merge",
    )(x, ycat, *consts)


def _ffn_kernel(x_ref, cpast_ref, wup_ref, cw_ref, cb_ref, wdn_ref, g_ref, b_ref,
                x2_ref, cout_ref, hbuf, act, *, nb, rows, n_tiles):
    tail = (FFN_CONV - 1) * nb
    n_chunks = rows // ROW_CHUNK
    tile = pl.program_id(0)

    @pl.when(tile == 0)
    def _init():
        hbuf[0:tail, :] = cpast_ref[...]

    def chunk(c, carry):
        r0 = pl.multiple_of(c * ROW_CHUNK, ROW_CHUNK)
        x = x_ref[pl.ds(r0, ROW_CHUNK), :]
        xb = x.astype(BF16)
        for cc in range(D_FF // FF_COL_CHUNK):
            conv = []
            for part in range(2):
                lo = part * D_FF + cc * FF_COL_CHUNK
                h = _dot(xb, wup_ref[:, lo:lo + FF_COL_CHUNK])
                hbuf[pl.ds(tail + r0, ROW_CHUNK), lo:lo + FF_COL_CHUNK] = h
                y = cb_ref[:, lo:lo + FF_COL_CHUNK] \
                    + hbuf[pl.ds(r0, ROW_CHUNK), lo:lo + FF_COL_CHUNK] * cw_ref[0:1, lo:lo + FF_COL_CHUNK]
                y = y + hbuf[pl.ds(pl.multiple_of(r0 + nb, SUBLANES), ROW_CHUNK), lo:lo + FF_COL_CHUNK] \
                    * cw_ref[1:2, lo:lo + FF_COL_CHUNK]
                y = y + h * cw_ref[2:3, lo:lo + FF_COL_CHUNK]
                conv.append(y)
            a = _gelu(conv[1]) * conv[0]
            act[:, cc * FF_COL_CHUNK:(cc + 1) * FF_COL_CHUNK] = a.astype(BF16)
        ff = _dot(act[...], wdn_ref[...])
        x2_ref[pl.ds(r0, ROW_CHUNK), :] = _layer_norm(ALPHA * x + ff, g_ref[...], b_ref[...])
        return carry

    lax.fori_loop(0, n_chunks, chunk, 0)

    cout_ref[...] = hbuf[rows:rows + tail, :]
    if n_tiles > 1:
        hbuf[0:tail, :] = hbuf[rows:rows + tail, :]


def _ffn_call(x1, cpast, p, *, nb, rows):
    n_rows = x1.shape[0]
    n_tiles = n_rows // rows
    tail = (FFN_CONV - 1) * nb
    kern = functools.partial(_ffn_kernel, nb=nb, rows=rows, n_tiles=n_tiles)
    consts = (p['w_up'], p['conv_w'], p['conv_b'], p['w_down'], p['ln2_g'], p['ln2_b'])
    return pl.pallas_call(
        kern,
        grid=(n_tiles,),
        in_specs=[pl.BlockSpec((rows, D_MODEL), lambda i: (i, 0)), _whole()] + [_whole() for _ in consts],
        out_specs=[pl.BlockSpec((rows, D_MODEL), lambda i: (i, 0)),
                   pl.BlockSpec((tail, F2), lambda i: (0, 0))],
        out_shape=[jax.ShapeDtypeStruct((n_rows, D_MODEL), F32),
                   jax.ShapeDtypeStruct((tail, F2), F32)],
        scratch_shapes=[pltpu.VMEM((rows + tail, F2), F32),
                        pltpu.VMEM((ROW_CHUNK, D_FF), BF16)],
        compiler_params=pltpu.CompilerParams(dimension_semantics=("arbitrary",),
                                             vmem_limit_bytes=VMEM_LIMIT_BYTES),
        name="convffn",
    )(x1, cpast, *consts)


def _ssm_discretise(a_re, a_im, log_step, b_re, b_im):
    dt = jnp.exp(log_step)[:, None]
    mag = jnp.exp(a_re * dt)
    ab_re = mag * jnp.cos(a_im * dt)
    ab_im = mag * jnp.sin(a_im * dt)
    den = jnp.square(a_re) + jnp.square(a_im)
    nr = ab_re - 1.0
    k_re = (nr * a_re + ab_im * a_im) / den
    k_im = (ab_im * a_re - nr * a_im) / den
    bb_re = k_re[..., None] * b_re - k_im[..., None] * b_im
    bb_im = k_re[..., None] * b_im + k_im[..., None] * b_re
    return ab_re, ab_im, bb_re, bb_im


def _block_diag_in(m):
    gpt = SSM_GROUPS // SSM_KT
    m = jnp.transpose(m, (0, 2, 1)).reshape(SSM_KT, gpt, SSM_GROUP_CH, SSM_STATE)
    eye = jnp.eye(gpt, dtype=m.dtype)
    return jnp.einsum('kgcn,gh->kgchn', m, eye).reshape(SSM_KT, gpt * SSM_GROUP_CH, gpt * SSM_STATE)


def _block_diag_out(m):
    gpt = SSM_GROUPS // SSM_KT
    m = jnp.transpose(m, (0, 2, 1)).reshape(SSM_KT, gpt, SSM_STATE, SSM_GROUP_CH)
    eye = jnp.eye(gpt, dtype=m.dtype)
    return jnp.einsum('kgnc,gh->kgnhc', m, eye).reshape(SSM_KT, gpt * SSM_STATE, gpt * SSM_GROUP_CH)


def _gmlp_mix(w_s, b_s, cl, nb):
    mask = jnp.tril(jnp.ones((cl, cl), dtype=bool))
    w = jnp.where(mask, w_s[:, :cl, :cl], 0)
    eye = jnp.eye(nb, dtype=w.dtype)
    wk = jnp.einsum('hts,bc->htbsc', w, eye).reshape(GMLP_HEADS, cl * nb, cl * nb).astype(BF16)
    bs = jnp.repeat(jnp.repeat(jnp.transpose(b_s[:, :cl]), GMLP_HEAD_DIM, axis=1), nb, axis=0)
    return wk, bs


def _t_major(a):
    return jnp.transpose(a, (1, 0, 2)).reshape(a.shape[0] * a.shape[1], a.shape[2])


def _seq_major(a, n_seq):
    return jnp.transpose(a.reshape(a.shape[0] // n_seq, n_seq, a.shape[1]), (1, 0, 2))


def _group_layer(x, st_re, st_im, st_pool, st_conv, p, wk, bs, *, nb, npos, start_pos, ffn_rows):
    ycat, h_re, h_im, vn, pool_new = _branches_call(x, st_re, st_im, st_pool, p, wk, bs,
                                                     nb=nb, npos=npos, start_pos=start_pos)
    x1 = _merge_call(x, ycat, p)
    x2, conv_new = _ffn_call(x1, st_conv, p, nb=nb, rows=ffn_rows)
    return x2, h_re, h_im, vn, pool_new, conv_new


def kernel(x_prompt, x_sample, state_ssm_re, state_ssm_im, state_pool, state_ffn_conv, w_in, b_gate, ssm_a_re, ssm_a_im, ssm_log_step, ssm_b_re, ssm_b_im, ssm_c_re, ssm_c_im, ssm_d, ssm_w_glu, ssm_b_glu, gmlp_ln_g, gmlp_ln_b, gmlp_w_s, gmlp_b_s, pool_w, pool_scale, w_br_ssm, w_br_gmlp, w_br_pool, w_o, ln1_g, ln1_b, ffn_w_up, ffn_conv_w, ffn_conv_b, ffn_w_down, ln2_g, ln2_b):
    n_p, t_p, _ = x_prompt.shape
    n_s, t_s, _ = x_sample.shape
    hp = _t_major(x_prompt)
    hs = _t_major(x_sample)
    npos_p = TILE_ROWS // n_p
    npos_s = TILE_ROWS // n_s
    assert npos_p == GMLP_CHUNK and npos_s == t_s and t_p % npos_p == 0

    row = lambda v: v.reshape(1, -1)
    outs = [[] for _ in range(9)]
    for l in range(DEPTH):
        ab_re, ab_im, bb_re, bb_im = _ssm_discretise(ssm_a_re[l], ssm_a_im[l], ssm_log_step[l],
                                                     ssm_b_re[l], ssm_b_im[l])
        p = dict(
            w_in_br=w_in[l][:, :BRANCH_COLS].astype(BF16),
            w_in_gate=w_in[l][:, BRANCH_COLS:].astype(BF16),
            b_gate=b_gate[l].reshape(1, N_BRANCH * D_MODEL),
            bb=jnp.concatenate([_block_diag_in(bb_re), _block_diag_in(bb_im)], axis=2).astype(BF16),
            a_re=row(ab_re), a_im=row(ab_im),
            cc=jnp.concatenate([_block_diag_out(ssm_c_re[l]), -_block_diag_out(ssm_c_im[l])],
                               axis=1).astype(BF16),
            d_skip=row(ssm_d[l]),
            w_glu=ssm_w_glu[l].astype(BF16), b_glu=row(ssm_b_glu[l]),
            gln_g=row(gmlp_ln_g[l]), gln_b=row(gmlp_ln_b[l]),
            pool_w=pool_w[l].astype(BF16), pool_scale=row(pool_scale[l]),
            w_br=jnp.concatenate([w_br_ssm[l], w_br_gmlp[l], w_br_pool[l]], axis=0).astype(BF16),
            w_o=w_o[l].astype(BF16), ln1_g=row(ln1_g[l]), ln1_b=row(ln1_b[l]),
            w_up=ffn_w_up[l].astype(BF16), conv_w=ffn_conv_w[l], conv_b=row(ffn_conv_b[l]),
            w_down=ffn_w_down[l].astype(BF16), ln2_g=row(ln2_g[l]), ln2_b=row(ln2_b[l]),
        )
        wk_p, bs_p = _gmlp_mix(gmlp_w_s[l], gmlp_b_s[l], npos_p, n_p)
        wk_s, bs_s = _gmlp_mix(gmlp_w_s[l], gmlp_b_s[l], npos_s, n_s)

        hp, a_re_n, a_im_n, _, pool_n, conv_n = _group_layer(
            hp, jnp.zeros((n_p, SSM_COLS), F32), jnp.zeros((n_p, SSM_COLS), F32),
            jnp.zeros((POOL_PAST * n_p, POOL_WIDTH), F32), jnp.zeros(((FFN_CONV - 1) * n_p, F2), F32),
            p, wk_p, bs_p, nb=n_p, npos=npos_p, start_pos=0, ffn_rows=2 * ROW_CHUNK)
        hs, b_re_n, b_im_n, v_n, pool_m, conv_m = _group_layer(
            hs, state_ssm_re[l].reshape(n_s, SSM_COLS), state_ssm_im[l].reshape(n_s, SSM_COLS),
            _t_major(state_pool[l]), _t_major(state_ffn_conv[l]),
            p, wk_s, bs_s, nb=n_s, npos=npos_s, start_pos=PAST_LEN, ffn_rows=ROW_CHUNK)

        vals = (a_re_n.reshape(n_p, SSM_GROUPS, SSM_STATE), a_im_n.reshape(n_p, SSM_GROUPS, SSM_STATE),
                _seq_major(pool_n, n_p), _seq_major(conv_n, n_p),
                b_re_n.reshape(n_s, SSM_GROUPS, SSM_STATE), b_im_n.reshape(n_s, SSM_GROUPS, SSM_STATE),
                _seq_major(v_n, n_s), _seq_major(pool_m, n_s), _seq_major(conv_m, n_s))
        for o, v in zip(outs, vals):
            o.append(v)

    return (_seq_major(hp, n_p), _seq_major(hs, n_s)) + tuple(jnp.stack(o) for o in outs)
```

```python
import functools
import math

import jax
import jax.numpy as jnp
from jax import lax
from jax.experimental import pallas as pl
from jax.experimental.pallas import tpu as pltpu

F32 = jnp.float32
BF16 = jnp.bfloat16

D_MODEL = 1024
DEPTH = 2
PAST_LEN = 16384
SSM_GROUP_CH = 16
SSM_WIDTH = 512
SSM_GROUPS = 32
SSM_STATE = 64
SSM_COLS = SSM_GROUPS * SSM_STATE
GMLP_HEADS = 4
GMLP_CHUNK = 128
GMLP_WIDTH = 256
GMLP_HEAD_DIM = 64
POOL_WINDOWS = (2, 4, 8, 16)
POOL_WIDTH = 512
POOL_GROUP_CH = 128
POOL_PAST = 15
N_BRANCH = 3
BRANCH_COLS = SSM_WIDTH + 2 * GMLP_WIDTH + POOL_WIDTH
D_FF = 2816
F2 = 2 * D_FF
FFN_CONV = 3
LN_EPS = 1e-5
ALPHA = (2 * DEPTH) ** 0.25

SUBLANES = 8
MXU_DIM = 256
VMEM_LIMIT_BYTES = 60 * 1024 * 1024

TILE_ROWS = 1024
ROW_CHUNK = 256
SSM_KT = SSM_WIDTH // MXU_DIM
KT_COLS = SSM_COLS // SSM_KT
FF_COL_CHUNK = MXU_DIM


def _gelu(x):
    c = math.sqrt(2.0 / math.pi)
    return 0.5 * x * (1.0 + jnp.tanh(c * (x + 0.044715 * (x * x * x))))


def _sigmoid(x):
    return 1.0 / (1.0 + jnp.exp(-x))


def _layer_norm(x, g, b):
    mu = jnp.mean(x, axis=-1, keepdims=True)
    xc = x - mu
    var = jnp.mean(xc * xc, axis=-1, keepdims=True)
    return xc * lax.rsqrt(var + LN_EPS) * g + b


def _dot(a, b):
    return jnp.dot(a, b, preferred_element_type=F32)


def _whole(shape=None):
    return pl.BlockSpec(memory_space=pltpu.VMEM)


def _expand_mix(ws_ref, wk, rows, shift, nb):
    cl = GMLP_CHUNK
    e = (lax.broadcasted_iota(jnp.int32, (rows, cl), 0) >> shift
         == lax.broadcasted_iota(jnp.int32, (rows, cl), 1)).astype(BF16)
    tril = (lax.broadcasted_iota(jnp.int32, (cl, cl), 0)
            >= lax.broadcasted_iota(jnp.int32, (cl, cl), 1))
    for h in range(GMLP_HEADS):
        w = jnp.where(tril, ws_ref[h], 0.0).astype(BF16)
        rows_w = _dot(e, w).astype(BF16)
        for c0 in range(0, rows, MXU_DIM):
            et = (lax.broadcasted_iota(jnp.int32, (cl, MXU_DIM), 0)
                  == (lax.broadcasted_iota(jnp.int32, (cl, MXU_DIM), 1) + c0) >> shift).astype(BF16)
            full = _dot(rows_w, et)
            same_seq = ((lax.broadcasted_iota(jnp.int32, (rows, MXU_DIM), 0) & (nb - 1))
                        == ((lax.broadcasted_iota(jnp.int32, (rows, MXU_DIM), 1) + c0) & (nb - 1)))
            wk[h, :, c0:c0 + MXU_DIM] = jnp.where(same_seq, full, 0.0).astype(BF16)


def _branches_kernel(x_ref, h0re_ref, h0im_ref, ppast_ref,
                     w_in_ref, bb_ref, are_ref, aim_ref, cc_ref, dsk_ref,
                     wglu_ref, bglu_ref, lng_ref, lnb_ref, ws_ref, bs_ref,
                     poolw_ref, pscale_ref,
                     ycat_ref, hfre_ref, hfim_ref, vn_out_ref, pool_out_ref,
                     hbuf, us, ya, gu, vn, xp, hcre, hcim, wk,
                     *, nb, npos, n_tiles, start_pos):
    rows = nb * npos
    n_chunks = rows // ROW_CHUNK
    past_rows = POOL_PAST * nb
    tile = pl.program_id(0)
    shift = int(math.log2(nb))

    @pl.when(tile == 0)
    def _init():
        hcre[...] = h0re_ref[...]
        hcim[...] = h0im_ref[...]
        xp[0:past_rows, :] = ppast_ref[...]
        _expand_mix(ws_ref, wk, rows, shift, nb)

    def proj_chunk(c, carry):
        r0 = pl.multiple_of(c * ROW_CHUNK, ROW_CHUNK)
        xb = x_ref[pl.ds(r0, ROW_CHUNK), :].astype(BF16)
        pa = _dot(xb, w_in_ref[...])
        us[pl.ds(r0, ROW_CHUNK), :] = pa[:, 0:SSM_WIDTH]
        gu[pl.ds(r0, ROW_CHUNK), :] = _gelu(pa[:, SSM_WIDTH:SSM_WIDTH + GMLP_WIDTH])
        v = _gelu(pa[:, SSM_WIDTH + GMLP_WIDTH:SSM_WIDTH + 2 * GMLP_WIDTH])
        vn[pl.ds(r0, ROW_CHUNK), :] = _layer_norm(v, lng_ref[...], lnb_ref[...])
        xp[pl.ds(past_rows + r0, ROW_CHUNK), :] = pa[:, SSM_WIDTH + 2 * GMLP_WIDTH:BRANCH_COLS]
        return carry

    lax.fori_loop(0, n_chunks, proj_chunk, 0)

    for kt in range(SSM_KT):
        c_lo = kt * KT_COLS

        def bu_chunk(c, carry, kt=kt):
            r0 = pl.multiple_of(c * ROW_CHUNK, ROW_CHUNK)
            ub = us[pl.ds(r0, ROW_CHUNK), kt * MXU_DIM:(kt + 1) * MXU_DIM].astype(BF16)
            hbuf[pl.ds(r0, ROW_CHUNK), :] = _dot(ub, bb_ref[kt])
            return carry

        lax.fori_loop(0, n_chunks, bu_chunk, 0)

        half = KT_COLS // 2
        for ch in range(2):
            lo = ch * half
            a_r = jnp.broadcast_to(are_ref[:, c_lo + lo:c_lo + lo + half], (SUBLANES, half))
            a_i = jnp.broadcast_to(aim_ref[:, c_lo + lo:c_lo + lo + half], (SUBLANES, half))

            def seq_block(rb, carry, lo=lo, a_r=a_r, a_i=a_i, c_lo=c_lo):
                b0 = pl.multiple_of(rb * SUBLANES, SUBLANES)
                h_r = hcre[pl.ds(b0, SUBLANES), c_lo + lo:c_lo + lo + half]
                h_i = hcim[pl.ds(b0, SUBLANES), c_lo + lo:c_lo + lo + half]

                def step(t, h):
                    h_r, h_i = h
                    row = pl.multiple_of(t * nb + b0, SUBLANES)
                    b_r = hbuf[pl.ds(row, SUBLANES), lo:lo + half]
                    b_i = hbuf[pl.ds(row, SUBLANES), KT_COLS + lo:KT_COLS + lo + half]
                    n_r = a_r * h_r - a_i * h_i + b_r
                    n_i = a_r * h_i + a_i * h_r + b_i
                    hbuf[pl.ds(row, SUBLANES), lo:lo + half] = n_r
                    hbuf[pl.ds(row, SUBLANES), KT_COLS + lo:KT_COLS + lo + half] = n_i
                    return n_r, n_i

                h_r, h_i = lax.fori_loop(0, npos, step, (h_r, h_i))
                hcre[pl.ds(b0, SUBLANES), c_lo + lo:c_lo + lo + half] = h_r
                hcim[pl.ds(b0, SUBLANES), c_lo + lo:c_lo + lo + half] = h_i
                return carry

            lax.fori_loop(0, nb // SUBLANES, seq_block, 0)

        def y_chunk(c, carry, kt=kt):
            r0 = pl.multiple_of(c * ROW_CHUNK, ROW_CHUNK)
            hb = hbuf[pl.ds(r0, ROW_CHUNK), :].astype(BF16)
            u = us[pl.ds(r0, ROW_CHUNK), kt * MXU_DIM:(kt + 1) * MXU_DIM]
            y = _dot(hb, cc_ref[kt]) + dsk_ref[:, kt * MXU_DIM:(kt + 1) * MXU_DIM] * u
            ya[pl.ds(r0, ROW_CHUNK), kt * MXU_DIM:(kt + 1) * MXU_DIM] = y
            return carry

        lax.fori_loop(0, n_chunks, y_chunk, 0)

    hfre_ref[...] = hcre[...]
    hfim_ref[...] = hcim[...]

    vb = vn[...].astype(BF16)
    head = lax.broadcasted_iota(jnp.int32, (1, GMLP_WIDTH), 1) // GMLP_HEAD_DIM
    s = bs_ref[...]
    for h in range(GMLP_HEADS):
        s = s + jnp.where(head == h, _dot(wk[h], vb), 0.0)
    ycat_ref[:, SSM_WIDTH:SSM_WIDTH + GMLP_WIDTH] = (gu[...] * s).astype(BF16)
    vn_out_ref[...] = vn[...]

    base_pos = start_pos + tile * npos

    def out_chunk(c, carry):
        r0 = pl.multiple_of(c * ROW_CHUNK, ROW_CHUNK)
        z = _gelu(ya[pl.ds(r0, ROW_CHUNK), :])
        o_a = z * _sigmoid(_dot(z.astype(BF16), wglu_ref[...]) + bglu_ref[...])
        ycat_ref[pl.ds(r0, ROW_CHUNK), 0:SSM_WIDTH] = o_a.astype(BF16)

        ridx = lax.broadcasted_iota(jnp.int32, (ROW_CHUNK, 1), 0) + r0
        pos = base_pos + (ridx >> shift)
        for g, win in enumerate(POOL_WINDOWS):
            cl = g * POOL_GROUP_CH
            cur = xp[pl.ds(past_rows + r0, ROW_CHUNK), cl:cl + POOL_GROUP_CH]
            wsum = cur
            for j in range(1, win):
                off = pl.multiple_of(past_rows - j * nb + r0, SUBLANES)
                wsum = wsum + xp[pl.ds(off, ROW_CHUNK), cl:cl + POOL_GROUP_CH]
            cnt = jnp.minimum(pos + 1, win).astype(F32)
            d = wsum / cnt - cur
            y_c = _dot(d.astype(BF16), poolw_ref[g]) * pscale_ref[:, cl:cl + POOL_GROUP_CH]
            col = SSM_WIDTH + GMLP_WIDTH + cl
            ycat_ref[pl.ds(r0, ROW_CHUNK), col:col + POOL_GROUP_CH] = y_c.astype(BF16)
        return carry

    lax.fori_loop(0, n_chunks, out_chunk, 0)

    pool_out_ref[...] = xp[rows:rows + past_rows, :]
    if n_tiles > 1:
        xp[0:past_rows, :] = xp[rows:rows + past_rows, :]


def _branches_call(x, h0re, h0im, ppast, p, bs, *, nb, npos, start_pos):
    n_rows = x.shape[0]
    rows = nb * npos
    n_tiles = n_rows // rows
    past_rows = POOL_PAST * nb
    kern = functools.partial(_branches_kernel, nb=nb, npos=npos, n_tiles=n_tiles, start_pos=start_pos)
    row_tile = lambda w: pl.BlockSpec((rows, w), lambda i: (i, 0))
    fixed = lambda r, w: pl.BlockSpec((r, w), lambda i: (0, 0))
    consts = (p['w_in_br'], p['bb'], p['a_re'], p['a_im'], p['cc'], p['d_skip'],
              p['w_glu'], p['b_glu'], p['gln_g'], p['gln_b'], p['w_s'], bs, p['pool_w'], p['pool_scale'])
    return pl.pallas_call(
        kern,
        grid=(n_tiles,),
        in_specs=[row_tile(D_MODEL), _whole(), _whole(), _whole()] + [_whole() for _ in consts],
        out_specs=[row_tile(BRANCH_COLS - GMLP_WIDTH), fixed(nb, SSM_COLS), fixed(nb, SSM_COLS),
                   row_tile(GMLP_WIDTH), fixed(past_rows, POOL_WIDTH)],
        out_shape=[jax.ShapeDtypeStruct((n_rows, BRANCH_COLS - GMLP_WIDTH), BF16),
                   jax.ShapeDtypeStruct((nb, SSM_COLS), F32),
                   jax.ShapeDtypeStruct((nb, SSM_COLS), F32),
                   jax.ShapeDtypeStruct((n_rows, GMLP_WIDTH), F32),
                   jax.ShapeDtypeStruct((past_rows, POOL_WIDTH), F32)],
        scratch_shapes=[pltpu.VMEM((rows, 2 * KT_COLS), F32),
                        pltpu.VMEM((rows, SSM_WIDTH), F32),
                        pltpu.VMEM((rows, SSM_WIDTH), F32),
                        pltpu.VMEM((rows, GMLP_WIDTH), F32),
                        pltpu.VMEM((rows, GMLP_WIDTH), F32),
                        pltpu.VMEM((rows + past_rows, POOL_WIDTH), F32),
                        pltpu.VMEM((nb, SSM_COLS), F32),
                        pltpu.VMEM((nb, SSM_COLS), F32),
                        pltpu.VMEM((GMLP_HEADS, rows, rows), BF16)],
        compiler_params=pltpu.CompilerParams(dimension_semantics=("arbitrary",),
                                             vmem_limit_bytes=VMEM_LIMIT_BYTES),
        name="branches",
    )(x, h0re, h0im, ppast, *consts)


def _merge_kernel(x_ref, ycat_ref, wg_ref, bg_ref, wbr_ref, wo_ref, g_ref, b_ref, x1_ref):
    x = x_ref[...]
    xb = x.astype(BF16)
    yc = ycat_ref[...]
    bounds = (0, SSM_WIDTH, SSM_WIDTH + GMLP_WIDTH, BRANCH_COLS - GMLP_WIDTH)
    merged = None
    for i in range(N_BRANCH):
        gate = _sigmoid(_dot(xb, wg_ref[:, i * D_MODEL:(i + 1) * D_MODEL])
                        + bg_ref[:, i * D_MODEL:(i + 1) * D_MODEL])
        br = _dot(yc[:, bounds[i]:bounds[i + 1]], wbr_ref[bounds[i]:bounds[i + 1], :])
        merged = gate * br if merged is None else merged + gate * br
    mix = _dot(merged.astype(BF16), wo_ref[...])
    x1_ref[...] = _layer_norm(ALPHA * x + mix, g_ref[...], b_ref[...])


def _merge_call(x, ycat, p):
    n_rows = x.shape[0]
    rows = ROW_CHUNK
    consts = (p['w_in_gate'], p['b_gate'], p['w_br'], p['w_o'], p['ln1_g'], p['ln1_b'])
    return pl.pallas_call(
        _merge_kernel,
        grid=(n_rows // rows,),
        in_specs=[pl.BlockSpec((rows, D_MODEL), lambda i: (i, 0)),
                  pl.BlockSpec((rows, BRANCH_COLS - GMLP_WIDTH), lambda i: (i, 0))]
                 + [_whole() for _ in consts],
        out_specs=pl.BlockSpec((rows, D_MODEL), lambda i: (i, 0)),
        out_shape=jax.ShapeDtypeStruct((n_rows, D_MODEL), F32),
        compiler_params=pltpu.CompilerParams(dimension_semantics=("arbitrary",),
                                             vmem_limit_bytes=VMEM_LIMIT_BYTES),
        name="merge",
    )(x, ycat, *consts)


def _ffn_kernel(x_ref, cpast_ref, wup_ref, cw_ref, cb_ref, wdn_ref, g_ref, b_ref,
                x2_ref, cout_ref, tailbuf, act, *, nb, rows):
    tail = (FFN_CONV - 1) * nb
    assert rows >= tail

    @pl.when(pl.program_id(0) == 0)
    def _init():
        tailbuf[...] = cpast_ref[...]

    x = x_ref[...]
    xb = x.astype(BF16)
    for cc in range(D_FF // FF_COL_CHUNK):
        conv = []
        for part in range(2):
            lo = part * D_FF + cc * FF_COL_CHUNK
            cols = slice(lo, lo + FF_COL_CHUNK)
            h = _dot(xb, wup_ref[:, cols])
            he = jnp.concatenate([tailbuf[:, cols], h], axis=0)
            tailbuf[:, cols] = h[rows - tail:rows, :]
            y = cb_ref[:, cols] + he[0:rows, :] * cw_ref[0:1, cols]
            y = y + he[nb:nb + rows, :] * cw_ref[1:2, cols]
            y = y + h * cw_ref[2:3, cols]
            conv.append(y)
        a = _gelu(conv[1]) * conv[0]
        act[:, cc * FF_COL_CHUNK:(cc + 1) * FF_COL_CHUNK] = a.astype(BF16)
    ff = _dot(act[...], wdn_ref[...])
    x2_ref[...] = _layer_norm(ALPHA * x + ff, g_ref[...], b_ref[...])
    cout_ref[...] = tailbuf[...]


def _ffn_call(x1, cpast, p, *, nb, rows):
    n_rows = x1.shape[0]
    n_tiles = n_rows // rows
    tail = (FFN_CONV - 1) * nb
    kern = functools.partial(_ffn_kernel, nb=nb, rows=rows)
    consts = (p['w_up'], p['conv_w'], p['conv_b'], p['w_down'], p['ln2_g'], p['ln2_b'])
    return pl.pallas_call(
        kern,
        grid=(n_tiles,),
        in_specs=[pl.BlockSpec((rows, D_MODEL), lambda i: (i, 0)), _whole()] + [_whole() for _ in consts],
        out_specs=[pl.BlockSpec((rows, D_MODEL), lambda i: (i, 0)),
                   pl.BlockSpec((tail, F2), lambda i: (0, 0))],
        out_shape=[jax.ShapeDtypeStruct((n_rows, D_MODEL), F32),
                   jax.ShapeDtypeStruct((tail, F2), F32)],
        scratch_shapes=[pltpu.VMEM((tail, F2), F32),
                        pltpu.VMEM((rows, D_FF), BF16)],
        compiler_params=pltpu.CompilerParams(dimension_semantics=("arbitrary",),
                                             vmem_limit_bytes=VMEM_LIMIT_BYTES),
        name="convffn",
    )(x1, cpast, *consts)


def _ssm_discretise(a_re, a_im, log_step, b_re, b_im):
    dt = jnp.exp(log_step)[:, None]
    mag = jnp.exp(a_re * dt)
    ab_re = mag * jnp.cos(a_im * dt)
    ab_im = mag * jnp.sin(a_im * dt)
    den = jnp.square(a_re) + jnp.square(a_im)
    nr = ab_re - 1.0
    k_re = (nr * a_re + ab_im * a_im) / den
    k_im = (ab_im * a_re - nr * a_im) / den
    bb_re = k_re[..., None] * b_re - k_im[..., None] * b_im
    bb_im = k_re[..., None] * b_im + k_im[..., None] * b_re
    return ab_re, ab_im, bb_re, bb_im


def _block_diag_in(m):
    gpt = SSM_GROUPS // SSM_KT
    m = jnp.transpose(m, (0, 2, 1)).reshape(SSM_KT, gpt, SSM_GROUP_CH, SSM_STATE)
    eye = jnp.eye(gpt, dtype=m.dtype)
    return jnp.einsum('kgcn,gh->kgchn', m, eye).reshape(SSM_KT, gpt * SSM_GROUP_CH, gpt * SSM_STATE)


def _block_diag_out(m):
    gpt = SSM_GROUPS // SSM_KT
    m = jnp.transpose(m, (0, 2, 1)).reshape(SSM_KT, gpt, SSM_STATE, SSM_GROUP_CH)
    eye = jnp.eye(gpt, dtype=m.dtype)
    return jnp.einsum('kgnc,gh->kgnhc', m, eye).reshape(SSM_KT, gpt * SSM_STATE, gpt * SSM_GROUP_CH)


def _gmlp_bias_rows(b_s, cl, nb):
    return jnp.repeat(jnp.repeat(jnp.transpose(b_s[:, :cl]), GMLP_HEAD_DIM, axis=1), nb, axis=0)


def _t_major(a):
    return jnp.transpose(a, (1, 0, 2)).reshape(a.shape[0] * a.shape[1], a.shape[2])


def _seq_major(a, n_seq):
    return jnp.transpose(a.reshape(a.shape[0] // n_seq, n_seq, a.shape[1]), (1, 0, 2))


def _group_layer(x, st_re, st_im, st_pool, st_conv, p, bs, *, nb, npos, start_pos, ffn_rows):
    ycat, h_re, h_im, vn, pool_new = _branches_call(x, st_re, st_im, st_pool, p, bs,
                                                     nb=nb, npos=npos, start_pos=start_pos)
    x1 = _merge_call(x, ycat, p)
    x2, conv_new = _ffn_call(x1, st_conv, p, nb=nb, rows=ffn_rows)
    return x2, h_re, h_im, vn, pool_new, conv_new


def kernel(x_prompt, x_sample, state_ssm_re, state_ssm_im, state_pool, state_ffn_conv, w_in, b_gate, ssm_a_re, ssm_a_im, ssm_log_step, ssm_b_re, ssm_b_im, ssm_c_re, ssm_c_im, ssm_d, ssm_w_glu, ssm_b_glu, gmlp_ln_g, gmlp_ln_b, gmlp_w_s, gmlp_b_s, pool_w, pool_scale, w_br_ssm, w_br_gmlp, w_br_pool, w_o, ln1_g, ln1_b, ffn_w_up, ffn_conv_w, ffn_conv_b, ffn_w_down, ln2_g, ln2_b):
    n_p, t_p, _ = x_prompt.shape
    n_s, t_s, _ = x_sample.shape
    hp = _t_major(x_prompt)
    hs = _t_major(x_sample)
    npos_p = TILE_ROWS // n_p
    npos_s = TILE_ROWS // n_s
    assert npos_p == GMLP_CHUNK and npos_s == t_s and t_p % npos_p == 0

    row = lambda v: v.reshape(1, -1)
    outs = [[] for _ in range(9)]
    for l in range(DEPTH):
        ab_re, ab_im, bb_re, bb_im = _ssm_discretise(ssm_a_re[l], ssm_a_im[l], ssm_log_step[l],
                                                     ssm_b_re[l], ssm_b_im[l])
        p = dict(
            w_in_br=w_in[l][:, :BRANCH_COLS].astype(BF16),
            w_in_gate=w_in[l][:, BRANCH_COLS:].astype(BF16),
            b_gate=b_gate[l].reshape(1, N_BRANCH * D_MODEL),
            bb=jnp.concatenate([_block_diag_in(bb_re), _block_diag_in(bb_im)], axis=2).astype(BF16),
            a_re=row(ab_re), a_im=row(ab_im),
            cc=jnp.concatenate([_block_diag_out(ssm_c_re[l]), -_block_diag_out(ssm_c_im[l])],
                               axis=1).astype(BF16),
            d_skip=row(ssm_d[l]),
            w_glu=ssm_w_glu[l].astype(BF16), b_glu=row(ssm_b_glu[l]),
            gln_g=row(gmlp_ln_g[l]), gln_b=row(gmlp_ln_b[l]),
            w_s=gmlp_w_s[l], pool_w=pool_w[l].astype(BF16), pool_scale=row(pool_scale[l]),
            w_br=jnp.concatenate([w_br_ssm[l], w_br_gmlp[l], w_br_pool[l]], axis=0).astype(BF16),
            w_o=w_o[l].astype(BF16), ln1_g=row(ln1_g[l]), ln1_b=row(ln1_b[l]),
            w_up=ffn_w_up[l].astype(BF16), conv_w=ffn_conv_w[l], conv_b=row(ffn_conv_b[l]),
            w_down=ffn_w_down[l].astype(BF16), ln2_g=row(ln2_g[l]), ln2_b=row(ln2_b[l]),
        )
        bs_p = _gmlp_bias_rows(gmlp_b_s[l], npos_p, n_p)
        bs_s = _gmlp_bias_rows(gmlp_b_s[l], npos_s, n_s)

        hp, a_re_n, a_im_n, _, pool_n, conv_n = _group_layer(
            hp, jnp.zeros((n_p, SSM_COLS), F32), jnp.zeros((n_p, SSM_COLS), F32),
            jnp.zeros((POOL_PAST * n_p, POOL_WIDTH), F32), jnp.zeros(((FFN_CONV - 1) * n_p, F2), F32),
            p, bs_p, nb=n_p, npos=npos_p, start_pos=0, ffn_rows=2 * ROW_CHUNK)
        hs, b_re_n, b_im_n, v_n, pool_m, conv_m = _group_layer(
            hs, state_ssm_re[l].reshape(n_s, SSM_COLS), state_ssm_im[l].reshape(n_s, SSM_COLS),
            _t_major(state_pool[l]), _t_major(state_ffn_conv[l]),
            p, bs_s, nb=n_s, npos=npos_s, start_pos=PAST_LEN, ffn_rows=ROW_CHUNK)

        vals = (a_re_n.reshape(n_p, SSM_GROUPS, SSM_STATE), a_im_n.reshape(n_p, SSM_GROUPS, SSM_STATE),
                _seq_major(pool_n, n_p), _seq_major(conv_n, n_p),
                b_re_n.reshape(n_s, SSM_GROUPS, SSM_STATE), b_im_n.reshape(n_s, SSM_GROUPS, SSM_STATE),
                _seq_major(v_n, n_s), _seq_major(pool_m, n_s), _seq_major(conv_m, n_s))
        for o, v in zip(outs, vals):
            o.append(v)

    return (_seq_major(hp, n_p), _seq_major(hs, n_s)) + tuple(jnp.stack(o) for o in outs)
```

```python
import functools
import math

import jax
import jax.numpy as jnp
from jax import lax
from jax.experimental import pallas as pl
from jax.experimental.pallas import tpu as pltpu

F32 = jnp.float32
BF16 = jnp.bfloat16

D_MODEL = 1024
DEPTH = 2
PAST_LEN = 16384
SSM_GROUP_CH = 16
SSM_WIDTH = 512
SSM_GROUPS = 32
SSM_STATE = 64
SSM_COLS = SSM_GROUPS * SSM_STATE
GMLP_HEADS = 4
GMLP_CHUNK = 128
GMLP_WIDTH = 256
GMLP_HEAD_DIM = 64
POOL_WINDOWS = (2, 4, 8, 16)
POOL_WIDTH = 512
POOL_GROUP_CH = 128
POOL_PAST = 15
N_BRANCH = 3
BRANCH_COLS = SSM_WIDTH + 2 * GMLP_WIDTH + POOL_WIDTH
D_FF = 2816
F2 = 2 * D_FF
FFN_CONV = 3
LN_EPS = 1e-5
ALPHA = (2 * DEPTH) ** 0.25

SUBLANES = 8
MXU_DIM = 256
VMEM_LIMIT_BYTES = 60 * 1024 * 1024

TILE_ROWS = 1024
ROW_CHUNK = 256
SSM_CHUNK = 512
SSM_KT = SSM_WIDTH // MXU_DIM
KT_COLS = SSM_COLS // SSM_KT
FF_COL_CHUNK = MXU_DIM


def _gelu(x):
    c = math.sqrt(2.0 / math.pi)
    return 0.5 * x * (1.0 + jnp.tanh(c * (x + 0.044715 * (x * x * x))))


def _sigmoid(x):
    return 1.0 / (1.0 + jnp.exp(-x))


def _layer_norm(x, g, b):
    mu = jnp.mean(x, axis=-1, keepdims=True)
    xc = x - mu
    var = jnp.mean(xc * xc, axis=-1, keepdims=True)
    return xc * lax.rsqrt(var + LN_EPS) * g + b


def _dot(a, b):
    return jnp.dot(a, b, preferred_element_type=F32)


def _whole():
    return pl.BlockSpec(memory_space=pltpu.VMEM)


def _layer_const(a, layer):
    nd = a.ndim - 1
    return pl.BlockSpec((None,) + a.shape[1:], lambda i: (layer,) + (0,) * nd,
                        pipeline_mode=pl.Buffered(1))


def _expand_mix(ws_ref, wk, rows, shift, nb):
    cl = GMLP_CHUNK
    e = (lax.broadcasted_iota(jnp.int32, (rows, cl), 0) >> shift
         == lax.broadcasted_iota(jnp.int32, (rows, cl), 1)).astype(BF16)
    tril = (lax.broadcasted_iota(jnp.int32, (cl, cl), 0)
            >= lax.broadcasted_iota(jnp.int32, (cl, cl), 1))
    for h in range(GMLP_HEADS):
        w = jnp.where(tril, ws_ref[h], 0.0).astype(BF16)
        rows_w = _dot(e, w).astype(BF16)
        for c0 in range(0, rows, MXU_DIM):
            et = (lax.broadcasted_iota(jnp.int32, (cl, MXU_DIM), 0)
                  == (lax.broadcasted_iota(jnp.int32, (cl, MXU_DIM), 1) + c0) >> shift).astype(BF16)
            full = _dot(rows_w, et)
            same_seq = ((lax.broadcasted_iota(jnp.int32, (rows, MXU_DIM), 0) & (nb - 1))
                        == ((lax.broadcasted_iota(jnp.int32, (rows, MXU_DIM), 1) + c0) & (nb - 1)))
            wk[h, :, c0:c0 + MXU_DIM] = jnp.where(same_seq, full, 0.0).astype(BF16)


def _branches_kernel(x_ref, h0re_ref, h0im_ref, ppast_ref,
                     w_in_ref, bb_ref, are_ref, aim_ref, cc_ref, dsk_ref,
                     wglu_ref, bglu_ref, lng_ref, lnb_ref, ws_ref, bs_ref,
                     poolw_ref, pscale_ref,
                     ycat_ref, hfre_ref, hfim_ref, vn_out_ref, pool_out_ref,
                     hbuf, us, ya, gu, vn, xp, hcre, hcim, wk,
                     *, nb, npos, n_tiles, start_pos):
    rows = nb * npos
    n_chunks = rows // ROW_CHUNK
    past_rows = POOL_PAST * nb
    tile = pl.program_id(0)
    shift = int(math.log2(nb))

    @pl.when(tile == 0)
    def _init():
        hcre[...] = h0re_ref[...]
        hcim[...] = h0im_ref[...]
        xp[0:past_rows, :] = ppast_ref[...]
        _expand_mix(ws_ref, wk, rows, shift, nb)

    def proj_chunk(c, carry):
        r0 = pl.multiple_of(c * ROW_CHUNK, ROW_CHUNK)
        xb = x_ref[pl.ds(r0, ROW_CHUNK), :].astype(BF16)
        pa = _dot(xb, w_in_ref[...])
        us[pl.ds(r0, ROW_CHUNK), :] = pa[:, 0:SSM_WIDTH]
        gu[pl.ds(r0, ROW_CHUNK), :] = _gelu(pa[:, SSM_WIDTH:SSM_WIDTH + GMLP_WIDTH])
        v = _gelu(pa[:, SSM_WIDTH + GMLP_WIDTH:SSM_WIDTH + 2 * GMLP_WIDTH])
        vn[pl.ds(r0, ROW_CHUNK), :] = _layer_norm(v, lng_ref[...], lnb_ref[...])
        xp[pl.ds(past_rows + r0, ROW_CHUNK), :] = pa[:, SSM_WIDTH + 2 * GMLP_WIDTH:BRANCH_COLS]
        return carry

    lax.fori_loop(0, n_chunks, proj_chunk, 0)

    for kt in range(SSM_KT):
        c_lo = kt * KT_COLS

        def bu_chunk(c, carry, kt=kt):
            r0 = pl.multiple_of(c * SSM_CHUNK, SSM_CHUNK)
            ub = us[pl.ds(r0, SSM_CHUNK), kt * MXU_DIM:(kt + 1) * MXU_DIM].astype(BF16)
            hbuf[pl.ds(r0, SSM_CHUNK), :] = _dot(ub, bb_ref[kt])
            return carry

        lax.fori_loop(0, rows // SSM_CHUNK, bu_chunk, 0)

        half = KT_COLS // 2
        for ch in range(2):
            lo = ch * half
            a_r = jnp.broadcast_to(are_ref[:, c_lo + lo:c_lo + lo + half], (SUBLANES, half))
            a_i = jnp.broadcast_to(aim_ref[:, c_lo + lo:c_lo + lo + half], (SUBLANES, half))

            def seq_block(rb, carry, lo=lo, a_r=a_r, a_i=a_i, c_lo=c_lo):
                b0 = pl.multiple_of(rb * SUBLANES, SUBLANES)
                h_r = hcre[pl.ds(b0, SUBLANES), c_lo + lo:c_lo + lo + half]
                h_i = hcim[pl.ds(b0, SUBLANES), c_lo + lo:c_lo + lo + half]

                def step(t, h):
                    h_r, h_i = h
                    row = pl.multiple_of(t * nb + b0, SUBLANES)
                    b_r = hbuf[pl.ds(row, SUBLANES), lo:lo + half]
                    b_i = hbuf[pl.ds(row, SUBLANES), KT_COLS + lo:KT_COLS + lo + half]
                    n_r = a_r * h_r - a_i * h_i + b_r
                    n_i = a_r * h_i + a_i * h_r + b_i
                    hbuf[pl.ds(row, SUBLANES), lo:lo + half] = n_r
                    hbuf[pl.ds(row, SUBLANES), KT_COLS + lo:KT_COLS + lo + half] = n_i
                    return n_r, n_i

                h_r, h_i = lax.fori_loop(0, npos, step, (h_r, h_i))
                hcre[pl.ds(b0, SUBLANES), c_lo + lo:c_lo + lo + half] = h_r
                hcim[pl.ds(b0, SUBLANES), c_lo + lo:c_lo + lo + half] = h_i
                return carry

            lax.fori_loop(0, nb // SUBLANES, seq_block, 0)

        def y_chunk(c, carry, kt=kt):
            r0 = pl.multiple_of(c * SSM_CHUNK, SSM_CHUNK)
            hb = hbuf[pl.ds(r0, SSM_CHUNK), :].astype(BF16)
            u = us[pl.ds(r0, SSM_CHUNK), kt * MXU_DIM:(kt + 1) * MXU_DIM]
            y = _dot(hb, cc_ref[kt]) + dsk_ref[:, kt * MXU_DIM:(kt + 1) * MXU_DIM] * u
            ya[pl.ds(r0, SSM_CHUNK), kt * MXU_DIM:(kt + 1) * MXU_DIM] = y
            return carry

        lax.fori_loop(0, rows // SSM_CHUNK, y_chunk, 0)

    hfre_ref[...] = hcre[...]
    hfim_ref[...] = hcim[...]

    vb = vn[...].astype(BF16)
    head = lax.broadcasted_iota(jnp.int32, (1, GMLP_WIDTH), 1) // GMLP_HEAD_DIM
    s = bs_ref[...]
    for h in range(GMLP_HEADS):
        s = s + jnp.where(head == h, _dot(wk[h], vb), 0.0)
    ycat_ref[:, SSM_WIDTH:SSM_WIDTH + GMLP_WIDTH] = (gu[...] * s).astype(BF16)
    vn_out_ref[...] = vn[...]

    base_pos = start_pos + tile * npos

    def out_chunk(c, carry):
        r0 = pl.multiple_of(c * ROW_CHUNK, ROW_CHUNK)
        z = _gelu(ya[pl.ds(r0, ROW_CHUNK), :])
        o_a = z * _sigmoid(_dot(z.astype(BF16), wglu_ref[...]) + bglu_ref[...])
        ycat_ref[pl.ds(r0, ROW_CHUNK), 0:SSM_WIDTH] = o_a.astype(BF16)

        ridx = lax.broadcasted_iota(jnp.int32, (ROW_CHUNK, 1), 0) + r0
        pos = base_pos + (ridx >> shift)
        for g, win in enumerate(POOL_WINDOWS):
            cl = g * POOL_GROUP_CH
            cur = xp[pl.ds(past_rows + r0, ROW_CHUNK), cl:cl + POOL_GROUP_CH]
            wsum = cur
            for j in range(1, win):
                off = pl.multiple_of(past_rows - j * nb + r0, SUBLANES)
                wsum = wsum + xp[pl.ds(off, ROW_CHUNK), cl:cl + POOL_GROUP_CH]
            cnt = jnp.minimum(pos + 1, win).astype(F32)
            d = wsum / cnt - cur
            y_c = _dot(d.astype(BF16), poolw_ref[g]) * pscale_ref[:, cl:cl + POOL_GROUP_CH]
            col = SSM_WIDTH + GMLP_WIDTH + cl
            ycat_ref[pl.ds(r0, ROW_CHUNK), col:col + POOL_GROUP_CH] = y_c.astype(BF16)
        return carry

    lax.fori_loop(0, n_chunks, out_chunk, 0)

    pool_out_ref[...] = xp[rows:rows + past_rows, :]
    if n_tiles > 1:
        xp[0:past_rows, :] = xp[rows:rows + past_rows, :]


def _branches_call(x, h0re, h0im, ppast, p, layer, bs, *, nb, npos, start_pos):
    n_rows = x.shape[0]
    rows = nb * npos
    n_tiles = n_rows // rows
    past_rows = POOL_PAST * nb
    kern = functools.partial(_branches_kernel, nb=nb, npos=npos, n_tiles=n_tiles, start_pos=start_pos)
    row_tile = lambda w: pl.BlockSpec((rows, w), lambda i: (i, 0))
    fixed = lambda r, w: pl.BlockSpec((r, w), lambda i: (0, 0))
    consts = (p['w_in_br'], p['bb'], p['a_re'], p['a_im'], p['cc'], p['d_skip'],
              p['w_glu'], p['b_glu'], p['gln_g'], p['gln_b'], p['w_s'], bs, p['pool_w'], p['pool_scale'])
    return pl.pallas_call(
        kern,
        grid=(n_tiles,),
        in_specs=[row_tile(D_MODEL), _whole(), _whole(), _whole()] + [_layer_const(c, layer) for c in consts],
        out_specs=[row_tile(BRANCH_COLS - GMLP_WIDTH), fixed(nb, SSM_COLS), fixed(nb, SSM_COLS),
                   row_tile(GMLP_WIDTH), fixed(past_rows, POOL_WIDTH)],
        out_shape=[jax.ShapeDtypeStruct((n_rows, BRANCH_COLS - GMLP_WIDTH), BF16),
                   jax.ShapeDtypeStruct((nb, SSM_COLS), F32),
                   jax.ShapeDtypeStruct((nb, SSM_COLS), F32),
                   jax.ShapeDtypeStruct((n_rows, GMLP_WIDTH), F32),
                   jax.ShapeDtypeStruct((past_rows, POOL_WIDTH), F32)],
        scratch_shapes=[pltpu.VMEM((rows, 2 * KT_COLS), F32),
                        pltpu.VMEM((rows, SSM_WIDTH), F32),
                        pltpu.VMEM((rows, SSM_WIDTH), F32),
                        pltpu.VMEM((rows, GMLP_WIDTH), F32),
                        pltpu.VMEM((rows, GMLP_WIDTH), F32),
                        pltpu.VMEM((rows + past_rows, POOL_WIDTH), F32),
                        pltpu.VMEM((nb, SSM_COLS), F32),
                        pltpu.VMEM((nb, SSM_COLS), F32),
                        pltpu.VMEM((GMLP_HEADS, rows, rows), BF16)],
        compiler_params=pltpu.CompilerParams(dimension_semantics=("arbitrary",),
                                             vmem_limit_bytes=VMEM_LIMIT_BYTES),
        name="branches",
    )(x, h0re, h0im, ppast, *consts)


def _merge_kernel(x_ref, ycat_ref, wg_ref, bg_ref, wbr_ref, wo_ref, g_ref, b_ref, x1_ref):
    x = x_ref[...]
    xb = x.astype(BF16)
    yc = ycat_ref[...]
    bounds = (0, SSM_WIDTH, SSM_WIDTH + GMLP_WIDTH, BRANCH_COLS - GMLP_WIDTH)
    merged = None
    for i in range(N_BRANCH):
        gate = _sigmoid(_dot(xb, wg_ref[:, i * D_MODEL:(i + 1) * D_MODEL])
                        + bg_ref[:, i * D_MODEL:(i + 1) * D_MODEL])
        br = _dot(yc[:, bounds[i]:bounds[i + 1]], wbr_ref[bounds[i]:bounds[i + 1], :])
        merged = gate * br if merged is None else merged + gate * br
    mix = _dot(merged.astype(BF16), wo_ref[...])
    x1_ref[...] = _layer_norm(ALPHA * x + mix, g_ref[...], b_ref[...])


def _merge_call(x, ycat, p, layer):
    n_rows = x.shape[0]
    rows = ROW_CHUNK
    consts = (p['w_in_gate'], p['b_gate'], p['w_br'], p['w_o'], p['ln1_g'], p['ln1_b'])
    return pl.pallas_call(
        _merge_kernel,
        grid=(n_rows // rows,),
        in_specs=[pl.BlockSpec((rows, D_MODEL), lambda i: (i, 0)),
                  pl.BlockSpec((rows, BRANCH_COLS - GMLP_WIDTH), lambda i: (i, 0))]
                 + [_layer_const(c, layer) for c in consts],
        out_specs=pl.BlockSpec((rows, D_MODEL), lambda i: (i, 0)),
        out_shape=jax.ShapeDtypeStruct((n_rows, D_MODEL), F32),
        compiler_params=pltpu.CompilerParams(dimension_semantics=("arbitrary",),
                                             vmem_limit_bytes=VMEM_LIMIT_BYTES),
        name="merge",
    )(x, ycat, *consts)


def _ffn_kernel(x_ref, cpast_ref, wup_ref, cw_ref, cb_ref, wdn_ref, g_ref, b_ref,
                x2_ref, cout_ref, tailbuf, act, *, nb, rows):
    tail = (FFN_CONV - 1) * nb
    assert rows >= tail

    @pl.when(pl.program_id(0) == 0)
    def _init():
        tailbuf[...] = cpast_ref[...]

    x = x_ref[...]
    xb = x.astype(BF16)
    for cc in range(D_FF // FF_COL_CHUNK):
        conv = []
        for part in range(2):
            lo = part * D_FF + cc * FF_COL_CHUNK
            cols = slice(lo, lo + FF_COL_CHUNK)
            h = _dot(xb, wup_ref[:, cols])
            he = jnp.concatenate([tailbuf[:, cols], h], axis=0)
            tailbuf[:, cols] = h[rows - tail:rows, :]
            y = cb_ref[:, cols] + he[0:rows, :] * cw_ref[0:1, cols]
            y = y + he[nb:nb + rows, :] * cw_ref[1:2, cols]
            y = y + h * cw_ref[2:3, cols]
            conv.append(y)
        a = _gelu(conv[1]) * conv[0]
        act[:, cc * FF_COL_CHUNK:(cc + 1) * FF_COL_CHUNK] = a.astype(BF16)
    ff = _dot(act[...], wdn_ref[...])
    x2_ref[...] = _layer_norm(ALPHA * x + ff, g_ref[...], b_ref[...])
    cout_ref[...] = tailbuf[...]


def _ffn_call(x1, cpast, p, layer, *, nb, rows):
    n_rows = x1.shape[0]
    n_tiles = n_rows // rows
    tail = (FFN_CONV - 1) * nb
    kern = functools.partial(_ffn_kernel, nb=nb, rows=rows)
    consts = (p['w_up'], p['conv_w'], p['conv_b'], p['w_down'], p['ln2_g'], p['ln2_b'])
    return pl.pallas_call(
        kern,
        grid=(n_tiles,),
        in_specs=[pl.BlockSpec((rows, D_MODEL), lambda i: (i, 0)), _whole()]
                 + [_layer_const(c, layer) for c in consts],
        out_specs=[pl.BlockSpec((rows, D_MODEL), lambda i: (i, 0)),
                   pl.BlockSpec((tail, F2), lambda i: (0, 0))],
        out_shape=[jax.ShapeDtypeStruct((n_rows, D_MODEL), F32),
                   jax.ShapeDtypeStruct((tail, F2), F32)],
        scratch_shapes=[pltpu.VMEM((tail, F2), F32),
                        pltpu.VMEM((rows, D_FF), BF16)],
        compiler_params=pltpu.CompilerParams(dimension_semantics=("arbitrary",),
                                             vmem_limit_bytes=VMEM_LIMIT_BYTES),
        name="convffn",
    )(x1, cpast, *consts)


def _ssm_discretise(a_re, a_im, log_step, b_re, b_im):
    dt = jnp.exp(log_step)[..., None]
    mag = jnp.exp(a_re * dt)
    ab_re = mag * jnp.cos(a_im * dt)
    ab_im = mag * jnp.sin(a_im * dt)
    den = jnp.square(a_re) + jnp.square(a_im)
    nr = ab_re - 1.0
    k_re = (nr * a_re + ab_im * a_im) / den
    k_im = (ab_im * a_re - nr * a_im) / den
    bb_re = k_re[..., None] * b_re - k_im[..., None] * b_im
    bb_im = k_re[..., None] * b_im + k_im[..., None] * b_re
    return ab_re, ab_im, bb_re, bb_im


def _block_diag_in(m):
    gpt = SSM_GROUPS // SSM_KT
    m = jnp.transpose(m, (0, 1, 3, 2)).reshape(-1, SSM_KT, gpt, SSM_GROUP_CH, SSM_STATE)
    eye = jnp.eye(gpt, dtype=m.dtype)
    return jnp.einsum('lkgcn,gh->lkgchn', m, eye).reshape(-1, SSM_KT, gpt * SSM_GROUP_CH, gpt * SSM_STATE)


def _block_diag_out(m):
    gpt = SSM_GROUPS // SSM_KT
    m = jnp.transpose(m, (0, 1, 3, 2)).reshape(-1, SSM_KT, gpt, SSM_STATE, SSM_GROUP_CH)
    eye = jnp.eye(gpt, dtype=m.dtype)
    return jnp.einsum('lkgnc,gh->lkgnhc', m, eye).reshape(-1, SSM_KT, gpt * SSM_STATE, gpt * SSM_GROUP_CH)


def _gmlp_bias_rows(b_s, cl, nb):
    b = jnp.transpose(b_s[:, :, :cl], (0, 2, 1))
    return jnp.repeat(jnp.repeat(b, GMLP_HEAD_DIM, axis=2), nb, axis=1)


def _t_major(a):
    return jnp.transpose(a, (1, 0, 2)).reshape(a.shape[0] * a.shape[1], a.shape[2])


def _seq_major(a, n_seq):
    return jnp.transpose(a.reshape(a.shape[0] // n_seq, n_seq, a.shape[1]), (1, 0, 2))


def _group_layer(x, st_re, st_im, st_pool, st_conv, p, layer, bs, *, nb, npos, start_pos, ffn_rows):
    ycat, h_re, h_im, vn, pool_new = _branches_call(x, st_re, st_im, st_pool, p, layer, bs,
                                                     nb=nb, npos=npos, start_pos=start_pos)
    x1 = _merge_call(x, ycat, p, layer)
    x2, conv_new = _ffn_call(x1, st_conv, p, layer, nb=nb, rows=ffn_rows)
    return x2, h_re, h_im, vn, pool_new, conv_new


def kernel(x_prompt, x_sample, state_ssm_re, state_ssm_im, state_pool, state_ffn_conv, w_in, b_gate, ssm_a_re, ssm_a_im, ssm_log_step, ssm_b_re, ssm_b_im, ssm_c_re, ssm_c_im, ssm_d, ssm_w_glu, ssm_b_glu, gmlp_ln_g, gmlp_ln_b, gmlp_w_s, gmlp_b_s, pool_w, pool_scale, w_br_ssm, w_br_gmlp, w_br_pool, w_o, ln1_g, ln1_b, ffn_w_up, ffn_conv_w, ffn_conv_b, ffn_w_down, ln2_g, ln2_b):
    n_p, t_p, _ = x_prompt.shape
    n_s, t_s, _ = x_sample.shape
    hp = _t_major(x_prompt)
    hs = _t_major(x_sample)
    npos_p = TILE_ROWS // n_p
    npos_s = TILE_ROWS // n_s
    assert npos_p == GMLP_CHUNK and npos_s == t_s and t_p % npos_p == 0

    row = lambda v: v.reshape(DEPTH, 1, -1)
    ab_re, ab_im, bb_re, bb_im = _ssm_discretise(ssm_a_re, ssm_a_im, ssm_log_step, ssm_b_re, ssm_b_im)
    p = dict(
        w_in_br=w_in[:, :, :BRANCH_COLS].astype(BF16),
        w_in_gate=w_in[:, :, BRANCH_COLS:].astype(BF16),
        b_gate=row(b_gate),
        bb=jnp.concatenate([_block_diag_in(bb_re), _block_diag_in(bb_im)], axis=3).astype(BF16),
        a_re=row(ab_re), a_im=row(ab_im),
        cc=jnp.concatenate([_block_diag_out(ssm_c_re), -_block_diag_out(ssm_c_im)], axis=2).astype(BF16),
        d_skip=row(ssm_d),
        w_glu=ssm_w_glu.astype(BF16), b_glu=row(ssm_b_glu),
        gln_g=row(gmlp_ln_g), gln_b=row(gmlp_ln_b),
        w_s=gmlp_w_s, pool_w=pool_w.astype(BF16), pool_scale=row(pool_scale),
        w_br=jnp.concatenate([w_br_ssm, w_br_gmlp, w_br_pool], axis=1).astype(BF16),
        w_o=w_o.astype(BF16), ln1_g=row(ln1_g), ln1_b=row(ln1_b),
        w_up=ffn_w_up.astype(BF16), conv_w=ffn_conv_w, conv_b=row(ffn_conv_b),
        w_down=ffn_w_down.astype(BF16), ln2_g=row(ln2_g), ln2_b=row(ln2_b),
    )
    bs_p = _gmlp_bias_rows(gmlp_b_s, npos_p, n_p)
    bs_s = _gmlp_bias_rows(gmlp_b_s, npos_s, n_s)
    zeros_p = (jnp.zeros((n_p, SSM_COLS), F32), jnp.zeros((n_p, SSM_COLS), F32),
               jnp.zeros((POOL_PAST * n_p, POOL_WIDTH), F32), jnp.zeros(((FFN_CONV - 1) * n_p, F2), F32))

    outs = [[] for _ in range(9)]
    for l in range(DEPTH):
        hp, a_re_n, a_im_n, _, pool_n, conv_n = _group_layer(
            hp, *zeros_p, p, l, bs_p, nb=n_p, npos=npos_p, start_pos=0, ffn_rows=2 * ROW_CHUNK)
        hs, b_re_n, b_im_n, v_n, pool_m, conv_m = _group_layer(
            hs, state_ssm_re[l].reshape(n_s, SSM_COLS), state_ssm_im[l].reshape(n_s, SSM_COLS),
            _t_major(state_pool[l]), _t_major(state_ffn_conv[l]),
            p, l, bs_s, nb=n_s, npos=npos_s, start_pos=PAST_LEN, ffn_rows=ROW_CHUNK)

        vals = (a_re_n.reshape(n_p, SSM_GROUPS, SSM_STATE), a_im_n.reshape(n_p, SSM_GROUPS, SSM_STATE),
                _seq_major(pool_n, n_p), _seq_major(conv_n, n_p),
                b_re_n.reshape(n_s, SSM_GROUPS, SSM_STATE), b_im_n.reshape(n_s, SSM_GROUPS, SSM_STATE),
                _seq_major(v_n, n_s), _seq_major(pool_m, n_s), _seq_major(conv_m, n_s))
        for o, v in zip(outs, vals):
            o.append(v)

    return (_seq_major(hp, n_p), _seq_major(hs, n_s)) + tuple(jnp.stack(o) for o in outs)
```

```python
import functools
import math

import jax
import jax.numpy as jnp
from jax import lax
from jax.experimental import pallas as pl
from jax.experimental.pallas import tpu as pltpu

F32 = jnp.float32
BF16 = jnp.bfloat16

D_MODEL = 1024
DEPTH = 2
PAST_LEN = 16384
SSM_GROUP_CH = 16
SSM_WIDTH = 512
SSM_GROUPS = 32
SSM_STATE = 64
SSM_COLS = SSM_GROUPS * SSM_STATE
GMLP_HEADS = 4
GMLP_CHUNK = 128
GMLP_WIDTH = 256
GMLP_HEAD_DIM = 64
POOL_WINDOWS = (2, 4, 8, 16)
POOL_WIDTH = 512
POOL_GROUP_CH = 128
POOL_PAST = 15
N_BRANCH = 3
BRANCH_COLS = SSM_WIDTH + 2 * GMLP_WIDTH + POOL_WIDTH
D_FF = 2816
F2 = 2 * D_FF
FFN_CONV = 3
LN_EPS = 1e-5
ALPHA = (2 * DEPTH) ** 0.25

SUBLANES = 8
MXU_DIM = 256
VMEM_LIMIT_BYTES = 60 * 1024 * 1024

TILE_ROWS = 1024
ROW_CHUNK = 256
SSM_CHUNK = 512
SSM_KT = SSM_WIDTH // MXU_DIM
KT_COLS = SSM_COLS // SSM_KT
FF_COL_CHUNK = MXU_DIM


def _gelu(x):
    c = math.sqrt(2.0 / math.pi)
    return 0.5 * x * (1.0 + jnp.tanh(c * (x + 0.044715 * (x * x * x))))


def _sigmoid(x):
    return 1.0 / (1.0 + jnp.exp(-x))


def _layer_norm(x, g, b):
    mu = jnp.mean(x, axis=-1, keepdims=True)
    xc = x - mu
    var = jnp.mean(xc * xc, axis=-1, keepdims=True)
    return xc * lax.rsqrt(var + LN_EPS) * g + b


def _dot(a, b):
    return jnp.dot(a, b, preferred_element_type=F32)


def _whole():
    return pl.BlockSpec(memory_space=pltpu.VMEM)


def _layer_const(a, layer):
    nd = a.ndim - 1
    return pl.BlockSpec((None,) + a.shape[1:], lambda i: (layer,) + (0,) * nd,
                        pipeline_mode=pl.Buffered(1))


def _expand_mix(ws_ref, wk, rows, shift, nb):
    cl = GMLP_CHUNK
    e = (lax.broadcasted_iota(jnp.int32, (rows, cl), 0) >> shift
         == lax.broadcasted_iota(jnp.int32, (rows, cl), 1)).astype(BF16)
    tril = (lax.broadcasted_iota(jnp.int32, (cl, cl), 0)
            >= lax.broadcasted_iota(jnp.int32, (cl, cl), 1))
    for h in range(GMLP_HEADS):
        w = jnp.where(tril, ws_ref[h], 0.0).astype(BF16)
        rows_w = _dot(e, w).astype(BF16)
        for c0 in range(0, rows, MXU_DIM):
            et = (lax.broadcasted_iota(jnp.int32, (cl, MXU_DIM), 0)
                  == (lax.broadcasted_iota(jnp.int32, (cl, MXU_DIM), 1) + c0) >> shift).astype(BF16)
            full = _dot(rows_w, et)
            same_seq = ((lax.broadcasted_iota(jnp.int32, (rows, MXU_DIM), 0) & (nb - 1))
                        == ((lax.broadcasted_iota(jnp.int32, (rows, MXU_DIM), 1) + c0) & (nb - 1)))
            wk[h, :, c0:c0 + MXU_DIM] = jnp.where(same_seq, full, 0.0).astype(BF16)


def _branches_kernel(x_ref, h0re_ref, h0im_ref, ppast_ref,
                     w_in_ref, bb_ref, are_ref, aim_ref, cc_ref, dsk_ref,
                     wglu_ref, bglu_ref, lng_ref, lnb_ref, ws_ref, bs_ref,
                     poolw_ref, pscale_ref,
                     ycat_ref, hfre_ref, hfim_ref, vn_out_ref, pool_out_ref,
                     hbuf, us, ya, gu, vn, xp, hcre, hcim, wk,
                     *, nb, npos, n_tiles, start_pos):
    rows = nb * npos
    n_chunks = rows // ROW_CHUNK
    past_rows = POOL_PAST * nb
    tile = pl.program_id(0)
    shift = int(math.log2(nb))

    @pl.when(tile == 0)
    def _init():
        hcre[...] = h0re_ref[...]
        hcim[...] = h0im_ref[...]
        xp[0:past_rows, :] = ppast_ref[...]
        _expand_mix(ws_ref, wk, rows, shift, nb)

    def proj_chunk(c, carry):
        r0 = pl.multiple_of(c * ROW_CHUNK, ROW_CHUNK)
        xb = x_ref[pl.ds(r0, ROW_CHUNK), :].astype(BF16)
        pa = _dot(xb, w_in_ref[...])
        us[pl.ds(r0, ROW_CHUNK), :] = pa[:, 0:SSM_WIDTH]
        gu[pl.ds(r0, ROW_CHUNK), :] = _gelu(pa[:, SSM_WIDTH:SSM_WIDTH + GMLP_WIDTH])
        v = _gelu(pa[:, SSM_WIDTH + GMLP_WIDTH:SSM_WIDTH + 2 * GMLP_WIDTH])
        vn[pl.ds(r0, ROW_CHUNK), :] = _layer_norm(v, lng_ref[...], lnb_ref[...])
        xp[pl.ds(past_rows + r0, ROW_CHUNK), :] = pa[:, SSM_WIDTH + 2 * GMLP_WIDTH:BRANCH_COLS]
        return carry

    lax.fori_loop(0, n_chunks, proj_chunk, 0)

    for kt in range(SSM_KT):
        c_lo = kt * KT_COLS

        def bu_chunk(c, carry, kt=kt):
            r0 = pl.multiple_of(c * SSM_CHUNK, SSM_CHUNK)
            ub = us[pl.ds(r0, SSM_CHUNK), kt * MXU_DIM:(kt + 1) * MXU_DIM].astype(BF16)
            hbuf[pl.ds(r0, SSM_CHUNK), :] = _dot(ub, bb_ref[kt])
            return carry

        lax.fori_loop(0, rows // SSM_CHUNK, bu_chunk, 0)

        half = KT_COLS // 2
        for ch in range(2):
            lo = ch * half
            a_r = jnp.broadcast_to(are_ref[:, c_lo + lo:c_lo + lo + half], (SUBLANES, half))
            a_i = jnp.broadcast_to(aim_ref[:, c_lo + lo:c_lo + lo + half], (SUBLANES, half))

            def seq_block(rb, carry, lo=lo, a_r=a_r, a_i=a_i, c_lo=c_lo):
                b0 = pl.multiple_of(rb * SUBLANES, SUBLANES)
                h_r = hcre[pl.ds(b0, SUBLANES), c_lo + lo:c_lo + lo + half]
                h_i = hcim[pl.ds(b0, SUBLANES), c_lo + lo:c_lo + lo + half]

                def step(t, h):
                    h_r, h_i = h
                    row = pl.multiple_of(t * nb + b0, SUBLANES)
                    b_r = hbuf[pl.ds(row, SUBLANES), lo:lo + half]
                    b_i = hbuf[pl.ds(row, SUBLANES), KT_COLS + lo:KT_COLS + lo + half]
                    n_r = a_r * h_r - a_i * h_i + b_r
                    n_i = a_r * h_i + a_i * h_r + b_i
                    hbuf[pl.ds(row, SUBLANES), lo:lo + half] = n_r
                    hbuf[pl.ds(row, SUBLANES), KT_COLS + lo:KT_COLS + lo + half] = n_i
                    return n_r, n_i

                h_r, h_i = lax.fori_loop(0, npos, step, (h_r, h_i))
                hcre[pl.ds(b0, SUBLANES), c_lo + lo:c_lo + lo + half] = h_r
                hcim[pl.ds(b0, SUBLANES), c_lo + lo:c_lo + lo + half] = h_i
                return carry

            lax.fori_loop(0, nb // SUBLANES, seq_block, 0)

        def y_chunk(c, carry, kt=kt):
            r0 = pl.multiple_of(c * SSM_CHUNK, SSM_CHUNK)
            hb = hbuf[pl.ds(r0, SSM_CHUNK), :].astype(BF16)
            u = us[pl.ds(r0, SSM_CHUNK), kt * MXU_DIM:(kt + 1) * MXU_DIM]
            y = _dot(hb, cc_ref[kt]) + dsk_ref[:, kt * MXU_DIM:(kt + 1) * MXU_DIM] * u
            ya[pl.ds(r0, SSM_CHUNK), kt * MXU_DIM:(kt + 1) * MXU_DIM] = y
            return carry

        lax.fori_loop(0, rows // SSM_CHUNK, y_chunk, 0)

    hfre_ref[...] = hcre[...]
    hfim_ref[...] = hcim[...]

    vb = vn[...].astype(BF16)
    head = lax.broadcasted_iota(jnp.int32, (1, GMLP_WIDTH), 1) // GMLP_HEAD_DIM
    s = bs_ref[...]
    for h in range(GMLP_HEADS):
        s = s + jnp.where(head == h, _dot(wk[h], vb), 0.0)
    ycat_ref[:, SSM_WIDTH:SSM_WIDTH + GMLP_WIDTH] = (gu[...] * s).astype(BF16)
    vn_out_ref[...] = vn[...]

    base_pos = start_pos + tile * npos

    def out_chunk(c, carry):
        r0 = pl.multiple_of(c * ROW_CHUNK, ROW_CHUNK)
        z = _gelu(ya[pl.ds(r0, ROW_CHUNK), :])
        o_a = z * _sigmoid(_dot(z.astype(BF16), wglu_ref[...]) + bglu_ref[...])
        ycat_ref[pl.ds(r0, ROW_CHUNK), 0:SSM_WIDTH] = o_a.astype(BF16)

        ridx = lax.broadcasted_iota(jnp.int32, (ROW_CHUNK, 1), 0) + r0
        pos = base_pos + (ridx >> shift)
        for g, win in enumerate(POOL_WINDOWS):
            cl = g * POOL_GROUP_CH
            cur = xp[pl.ds(past_rows + r0, ROW_CHUNK), cl:cl + POOL_GROUP_CH]
            wsum = cur
            for j in range(1, win):
                off = pl.multiple_of(past_rows - j * nb + r0, SUBLANES)
                wsum = wsum + xp[pl.ds(off, ROW_CHUNK), cl:cl + POOL_GROUP_CH]
            cnt = jnp.minimum(pos + 1, win).astype(F32)
            d = wsum / cnt - cur
            y_c = _dot(d.astype(BF16), poolw_ref[g]) * pscale_ref[:, cl:cl + POOL_GROUP_CH]
            col = SSM_WIDTH + GMLP_WIDTH + cl
            ycat_ref[pl.ds(r0, ROW_CHUNK), col:col + POOL_GROUP_CH] = y_c.astype(BF16)
        return carry

    lax.fori_loop(0, n_chunks, out_chunk, 0)

    pool_out_ref[...] = xp[rows:rows + past_rows, :]
    if n_tiles > 1:
        xp[0:past_rows, :] = xp[rows:rows + past_rows, :]


def _branches_call(x, h0re, h0im, ppast, p, layer, bs, *, nb, npos, start_pos):
    n_rows = x.shape[0]
    rows = nb * npos
    n_tiles = n_rows // rows
    past_rows = POOL_PAST * nb
    kern = functools.partial(_branches_kernel, nb=nb, npos=npos, n_tiles=n_tiles, start_pos=start_pos)
    row_tile = lambda w: pl.BlockSpec((rows, w), lambda i: (i, 0))
    fixed = lambda r, w: pl.BlockSpec((r, w), lambda i: (0, 0))
    consts = (p['w_in_br'], p['bb'], p['a_re'], p['a_im'], p['cc'], p['d_skip'],
              p['w_glu'], p['b_glu'], p['gln_g'], p['gln_b'], p['w_s'], bs, p['pool_w'], p['pool_scale'])
    return pl.pallas_call(
        kern,
        grid=(n_tiles,),
        in_specs=[row_tile(D_MODEL), _whole(), _whole(), _whole()] + [_layer_const(c, layer) for c in consts],
        out_specs=[row_tile(BRANCH_COLS - GMLP_WIDTH), fixed(nb, SSM_COLS), fixed(nb, SSM_COLS),
                   row_tile(GMLP_WIDTH), fixed(past_rows, POOL_WIDTH)],
        out_shape=[jax.ShapeDtypeStruct((n_rows, BRANCH_COLS - GMLP_WIDTH), BF16),
                   jax.ShapeDtypeStruct((nb, SSM_COLS), F32),
                   jax.ShapeDtypeStruct((nb, SSM_COLS), F32),
                   jax.ShapeDtypeStruct((n_rows, GMLP_WIDTH), F32),
                   jax.ShapeDtypeStruct((past_rows, POOL_WIDTH), F32)],
        scratch_shapes=[pltpu.VMEM((rows, 2 * KT_COLS), F32),
                        pltpu.VMEM((rows, SSM_WIDTH), F32),
                        pltpu.VMEM((rows, SSM_WIDTH), F32),
                        pltpu.VMEM((rows, GMLP_WIDTH), F32),
                        pltpu.VMEM((rows, GMLP_WIDTH), F32),
                        pltpu.VMEM((rows + past_rows, POOL_WIDTH), F32),
                        pltpu.VMEM((nb, SSM_COLS), F32),
                        pltpu.VMEM((nb, SSM_COLS), F32),
                        pltpu.VMEM((GMLP_HEADS, rows, rows), BF16)],
        compiler_params=pltpu.CompilerParams(dimension_semantics=("arbitrary",),
                                             vmem_limit_bytes=VMEM_LIMIT_BYTES),
        name="branches",
    )(x, h0re, h0im, ppast, *consts)


def _merge_ffn_kernel(x_ref, ycat_ref, cpast_ref,
                      wg_ref, bg_ref, wbr_ref, wo_ref, g1_ref, b1_ref,
                      wup_ref, cw_ref, cb_ref, wdn_ref, g2_ref, b2_ref,
                      x2_ref, cout_ref, tailbuf, act, *, nb, rows):
    tail = (FFN_CONV - 1) * nb
    assert rows >= tail

    @pl.when(pl.program_id(0) == 0)
    def _init():
        tailbuf[...] = cpast_ref[...]

    x = x_ref[...]
    xb = x.astype(BF16)
    yc = ycat_ref[...]
    bounds = (0, SSM_WIDTH, SSM_WIDTH + GMLP_WIDTH, BRANCH_COLS - GMLP_WIDTH)
    merged = None
    for i in range(N_BRANCH):
        gate = _sigmoid(_dot(xb, wg_ref[:, i * D_MODEL:(i + 1) * D_MODEL])
                        + bg_ref[:, i * D_MODEL:(i + 1) * D_MODEL])
        br = _dot(yc[:, bounds[i]:bounds[i + 1]], wbr_ref[bounds[i]:bounds[i + 1], :])
        merged = gate * br if merged is None else merged + gate * br
    mix = _dot(merged.astype(BF16), wo_ref[...])
    x1 = _layer_norm(ALPHA * x + mix, g1_ref[...], b1_ref[...])

    x1b = x1.astype(BF16)
    for cc in range(D_FF // FF_COL_CHUNK):
        conv = []
        for part in range(2):
            lo = part * D_FF + cc * FF_COL_CHUNK
            cols = slice(lo, lo + FF_COL_CHUNK)
            h = _dot(x1b, wup_ref[:, cols])
            he = jnp.concatenate([tailbuf[:, cols], h], axis=0)
            tailbuf[:, cols] = h[rows - tail:rows, :]
            y = cb_ref[:, cols] + he[0:rows, :] * cw_ref[0:1, cols]
            y = y + he[nb:nb + rows, :] * cw_ref[1:2, cols]
            y = y + h * cw_ref[2:3, cols]
            conv.append(y)
        a = _gelu(conv[1]) * conv[0]
        act[:, cc * FF_COL_CHUNK:(cc + 1) * FF_COL_CHUNK] = a.astype(BF16)
    ff = _dot(act[...], wdn_ref[...])
    x2_ref[...] = _layer_norm(ALPHA * x1 + ff, g2_ref[...], b2_ref[...])
    cout_ref[...] = tailbuf[...]


def _merge_ffn_call(x, ycat, cpast, p, layer, *, nb, rows):
    n_rows = x.shape[0]
    tail = (FFN_CONV - 1) * nb
    kern = functools.partial(_merge_ffn_kernel, nb=nb, rows=rows)
    consts = (p['w_in_gate'], p['b_gate'], p['w_br'], p['w_o'], p['ln1_g'], p['ln1_b'],
              p['w_up'], p['conv_w'], p['conv_b'], p['w_down'], p['ln2_g'], p['ln2_b'])
    return pl.pallas_call(
        kern,
        grid=(n_rows // rows,),
        in_specs=[pl.BlockSpec((rows, D_MODEL), lambda i: (i, 0)),
                  pl.BlockSpec((rows, BRANCH_COLS - GMLP_WIDTH), lambda i: (i, 0)), _whole()]
                 + [_layer_const(c, layer) for c in consts],
        out_specs=[pl.BlockSpec((rows, D_MODEL), lambda i: (i, 0)),
                   pl.BlockSpec((tail, F2), lambda i: (0, 0))],
        out_shape=[jax.ShapeDtypeStruct((n_rows, D_MODEL), F32),
                   jax.ShapeDtypeStruct((tail, F2), F32)],
        scratch_shapes=[pltpu.VMEM((tail, F2), F32),
                        pltpu.VMEM((rows, D_FF), BF16)],
        compiler_params=pltpu.CompilerParams(dimension_semantics=("arbitrary",),
                                             vmem_limit_bytes=VMEM_LIMIT_BYTES),
        name="merge_convffn",
    )(x, ycat, cpast, *consts)


def _ssm_discretise(a_re, a_im, log_step, b_re, b_im):
    dt = jnp.exp(log_step)[..., None]
    mag = jnp.exp(a_re * dt)
    ab_re = mag * jnp.cos(a_im * dt)
    ab_im = mag * jnp.sin(a_im * dt)
    den = jnp.square(a_re) + jnp.square(a_im)
    nr = ab_re - 1.0
    k_re = (nr * a_re + ab_im * a_im) / den
    k_im = (ab_im * a_re - nr * a_im) / den
    bb_re = k_re[..., None] * b_re - k_im[..., None] * b_im
    bb_im = k_re[..., None] * b_im + k_im[..., None] * b_re
    return ab_re, ab_im, bb_re, bb_im


def _block_diag_in(m):
    gpt = SSM_GROUPS // SSM_KT
    m = jnp.transpose(m, (0, 1, 3, 2)).reshape(-1, SSM_KT, gpt, SSM_GROUP_CH, SSM_STATE)
    eye = jnp.eye(gpt, dtype=m.dtype)
    return jnp.einsum('lkgcn,gh->lkgchn', m, eye).reshape(-1, SSM_KT, gpt * SSM_GROUP_CH, gpt * SSM_STATE)


def _block_diag_out(m):
    gpt = SSM_GROUPS // SSM_KT
    m = jnp.transpose(m, (0, 1, 3, 2)).reshape(-1, SSM_KT, gpt, SSM_STATE, SSM_GROUP_CH)
    eye = jnp.eye(gpt, dtype=m.dtype)
    return jnp.einsum('lkgnc,gh->lkgnhc', m, eye).reshape(-1, SSM_KT, gpt * SSM_STATE, gpt * SSM_GROUP_CH)


def _gmlp_bias_rows(b_s, cl, nb):
    b = jnp.transpose(b_s[:, :, :cl], (0, 2, 1))
    return jnp.repeat(jnp.repeat(b, GMLP_HEAD_DIM, axis=2), nb, axis=1)


def _t_major(a):
    return jnp.transpose(a, (1, 0, 2)).reshape(a.shape[0] * a.shape[1], a.shape[2])


def _seq_major(a, n_seq):
    return jnp.transpose(a.reshape(a.shape[0] // n_seq, n_seq, a.shape[1]), (1, 0, 2))


def _group_layer(x, st_re, st_im, st_pool, st_conv, p, layer, bs, *, nb, npos, start_pos, ffn_rows):
    ycat, h_re, h_im, vn, pool_new = _branches_call(x, st_re, st_im, st_pool, p, layer, bs,
                                                     nb=nb, npos=npos, start_pos=start_pos)
    x2, conv_new = _merge_ffn_call(x, ycat, st_conv, p, layer, nb=nb, rows=ffn_rows)
    return x2, h_re, h_im, vn, pool_new, conv_new


def kernel(x_prompt, x_sample, state_ssm_re, state_ssm_im, state_pool, state_ffn_conv, w_in, b_gate, ssm_a_re, ssm_a_im, ssm_log_step, ssm_b_re, ssm_b_im, ssm_c_re, ssm_c_im, ssm_d, ssm_w_glu, ssm_b_glu, gmlp_ln_g, gmlp_ln_b, gmlp_w_s, gmlp_b_s, pool_w, pool_scale, w_br_ssm, w_br_gmlp, w_br_pool, w_o, ln1_g, ln1_b, ffn_w_up, ffn_conv_w, ffn_conv_b, ffn_w_down, ln2_g, ln2_b):
    n_p, t_p, _ = x_prompt.shape
    n_s, t_s, _ = x_sample.shape
    hp = _t_major(x_prompt)
    hs = _t_major(x_sample)
    npos_p = TILE_ROWS // n_p
    npos_s = TILE_ROWS // n_s
    assert npos_p == GMLP_CHUNK and npos_s == t_s and t_p % npos_p == 0

    row = lambda v: v.reshape(DEPTH, 1, -1)
    ab_re, ab_im, bb_re, bb_im = _ssm_discretise(ssm_a_re, ssm_a_im, ssm_log_step, ssm_b_re, ssm_b_im)
    p = dict(
        w_in_br=w_in[:, :, :BRANCH_COLS].astype(BF16),
        w_in_gate=w_in[:, :, BRANCH_COLS:].astype(BF16),
        b_gate=row(b_gate),
        bb=jnp.concatenate([_block_diag_in(bb_re), _block_diag_in(bb_im)], axis=3).astype(BF16),
        a_re=row(ab_re), a_im=row(ab_im),
        cc=jnp.concatenate([_block_diag_out(ssm_c_re), -_block_diag_out(ssm_c_im)], axis=2).astype(BF16),
        d_skip=row(ssm_d),
        w_glu=ssm_w_glu.astype(BF16), b_glu=row(ssm_b_glu),
        gln_g=row(gmlp_ln_g), gln_b=row(gmlp_ln_b),
        w_s=gmlp_w_s, pool_w=pool_w.astype(BF16), pool_scale=row(pool_scale),
        w_br=jnp.concatenate([w_br_ssm, w_br_gmlp, w_br_pool], axis=1).astype(BF16),
        w_o=w_o.astype(BF16), ln1_g=row(ln1_g), ln1_b=row(ln1_b),
        w_up=ffn_w_up.astype(BF16), conv_w=ffn_conv_w, conv_b=row(ffn_conv_b),
        w_down=ffn_w_down.astype(BF16), ln2_g=row(ln2_g), ln2_b=row(ln2_b),
    )
    bs_p = _gmlp_bias_rows(gmlp_b_s, npos_p, n_p)
    bs_s = _gmlp_bias_rows(gmlp_b_s, npos_s, n_s)
    zeros_p = (jnp.zeros((n_p, SSM_COLS), F32), jnp.zeros((n_p, SSM_COLS), F32),
               jnp.zeros((POOL_PAST * n_p, POOL_WIDTH), F32), jnp.zeros(((FFN_CONV - 1) * n_p, F2), F32))

    outs = [[] for _ in range(9)]
    for l in range(DEPTH):
        hp, a_re_n, a_im_n, _, pool_n, conv_n = _group_layer(
            hp, *zeros_p, p, l, bs_p, nb=n_p, npos=npos_p, start_pos=0, ffn_rows=2 * ROW_CHUNK)
        hs, b_re_n, b_im_n, v_n, pool_m, conv_m = _group_layer(
            hs, state_ssm_re[l].reshape(n_s, SSM_COLS), state_ssm_im[l].reshape(n_s, SSM_COLS),
            _t_major(state_pool[l]), _t_major(state_ffn_conv[l]),
            p, l, bs_s, nb=n_s, npos=npos_s, start_pos=PAST_LEN, ffn_rows=ROW_CHUNK)

        vals = (a_re_n.reshape(n_p, SSM_GROUPS, SSM_STATE), a_im_n.reshape(n_p, SSM_GROUPS, SSM_STATE),
                _seq_major(pool_n, n_p), _seq_major(conv_n, n_p),
                b_re_n.reshape(n_s, SSM_GROUPS, SSM_STATE), b_im_n.reshape(n_s, SSM_GROUPS, SSM_STATE),
                _seq_major(v_n, n_s), _seq_major(pool_m, n_s), _seq_major(conv_m, n_s))
        for o, v in zip(outs, vals):
            o.append(v)

    return (_seq_major(hp, n_p), _seq_major(hs, n_s)) + tuple(jnp.stack(o) for o in outs)
```

```python
import functools
import math

import jax
import jax.numpy as jnp
from jax import lax
from jax.experimental import pallas as pl
from jax.experimental.pallas import tpu as pltpu

F32 = jnp.float32
BF16 = jnp.bfloat16

D_MODEL = 1024
DEPTH = 2
PAST_LEN = 16384
SSM_GROUP_CH = 16
SSM_WIDTH = 512
SSM_GROUPS = 32
SSM_STATE = 64
SSM_COLS = SSM_GROUPS * SSM_STATE
GMLP_HEADS = 4
GMLP_CHUNK = 128
GMLP_WIDTH = 256
GMLP_HEAD_DIM = 64
POOL_WINDOWS = (2, 4, 8, 16)
POOL_WIDTH = 512
POOL_GROUP_CH = 128
POOL_PAST = 15
N_BRANCH = 3
BRANCH_COLS = SSM_WIDTH + 2 * GMLP_WIDTH + POOL_WIDTH
D_FF = 2816
F2 = 2 * D_FF
FFN_CONV = 3
LN_EPS = 1e-5
ALPHA = (2 * DEPTH) ** 0.25

SUBLANES = 8
MXU_DIM = 256
VMEM_LIMIT_BYTES = 60 * 1024 * 1024

TILE_ROWS = 1024
ROW_CHUNK = 256
SSM_CHUNK = 512
SCAN_BLOCKS = 2
SSM_KT = SSM_WIDTH // MXU_DIM
KT_COLS = SSM_COLS // SSM_KT
FF_COL_CHUNK = MXU_DIM
assert 2 * SSM_KT == GMLP_HEADS


def _gelu(x):
    c = math.sqrt(2.0 / math.pi)
    return 0.5 * x * (1.0 + jnp.tanh(c * (x + 0.044715 * (x * x * x))))


def _sigmoid(x):
    return 1.0 / (1.0 + jnp.exp(-x))


def _layer_norm(x, g, b):
    mu = jnp.mean(x, axis=-1, keepdims=True)
    xc = x - mu
    var = jnp.mean(xc * xc, axis=-1, keepdims=True)
    return xc * lax.rsqrt(var + LN_EPS) * g + b


def _dot(a, b):
    return jnp.dot(a, b, preferred_element_type=F32)


def _whole():
    return pl.BlockSpec(memory_space=pltpu.VMEM)


def _layer_const(a, layer):
    nd = a.ndim - 1
    return pl.BlockSpec((None,) + a.shape[1:], lambda i: (layer,) + (0,) * nd,
                        pipeline_mode=pl.Buffered(1))


def _expand_mix(ws_ref, wk, rows, shift, nb):
    cl = GMLP_CHUNK
    e = (lax.broadcasted_iota(jnp.int32, (rows, cl), 0) >> shift
         == lax.broadcasted_iota(jnp.int32, (rows, cl), 1)).astype(BF16)
    tril = (lax.broadcasted_iota(jnp.int32, (cl, cl), 0)
            >= lax.broadcasted_iota(jnp.int32, (cl, cl), 1))
    for h in range(GMLP_HEADS):
        w = jnp.where(tril, ws_ref[h], 0.0).astype(BF16)
        rows_w = _dot(e, w).astype(BF16)
        for c0 in range(0, rows, MXU_DIM):
            et = (lax.broadcasted_iota(jnp.int32, (cl, MXU_DIM), 0)
                  == (lax.broadcasted_iota(jnp.int32, (cl, MXU_DIM), 1) + c0) >> shift).astype(BF16)
            full = _dot(rows_w, et)
            same_seq = ((lax.broadcasted_iota(jnp.int32, (rows, MXU_DIM), 0) & (nb - 1))
                        == ((lax.broadcasted_iota(jnp.int32, (rows, MXU_DIM), 1) + c0) & (nb - 1)))
            wk[h, :, c0:c0 + MXU_DIM] = jnp.where(same_seq, full, 0.0).astype(BF16)


def _branches_kernel(x_ref, h0re_ref, h0im_ref, ppast_ref,
                     w_in_ref, bb_ref, are_ref, aim_ref, cc_ref, dsk_ref,
                     wglu_ref, bglu_ref, lng_ref, lnb_ref, ws_ref, bs_ref,
                     poolw_ref, pscale_ref,
                     ycat_ref, hcre, hcim, vn_out_ref, pool_out_ref,
                     hbuf, us, ya, gu, vnb, sacc, xp, wk,
                     *, nb, npos, n_tiles, start_pos):
    rows = nb * npos
    n_chunks = rows // ROW_CHUNK
    past_rows = POOL_PAST * nb
    tile = pl.program_id(0)
    shift = int(math.log2(nb))
    slabs_per_block = rows // SUBLANES // SCAN_BLOCKS
    steps = min(npos, slabs_per_block)
    groups_per_block = slabs_per_block // steps
    assert groups_per_block == 1 or steps == npos
    mix_rows = rows // SCAN_BLOCKS

    @pl.when(tile == 0)
    def _init():
        hcre[...] = h0re_ref[...]
        hcim[...] = h0im_ref[...]
        xp[0:past_rows, :] = ppast_ref[...]
        _expand_mix(ws_ref, wk, rows, shift, nb)

    def proj_chunk(c, carry):
        r0 = pl.multiple_of(c * ROW_CHUNK, ROW_CHUNK)
        xb = x_ref[pl.ds(r0, ROW_CHUNK), :].astype(BF16)
        pa = _dot(xb, w_in_ref[...])
        us[pl.ds(r0, ROW_CHUNK), :] = pa[:, 0:SSM_WIDTH]
        gu[pl.ds(r0, ROW_CHUNK), :] = _gelu(pa[:, SSM_WIDTH:SSM_WIDTH + GMLP_WIDTH])
        v = _gelu(pa[:, SSM_WIDTH + GMLP_WIDTH:SSM_WIDTH + 2 * GMLP_WIDTH])
        v = _layer_norm(v, lng_ref[...], lnb_ref[...])
        vn_out_ref[pl.ds(r0, ROW_CHUNK), :] = v
        vnb[pl.ds(r0, ROW_CHUNK), :] = v.astype(BF16)
        sacc[pl.ds(r0, ROW_CHUNK), :] = bs_ref[pl.ds(r0, ROW_CHUNK), :]
        xp[pl.ds(past_rows + r0, ROW_CHUNK), :] = pa[:, SSM_WIDTH + 2 * GMLP_WIDTH:BRANCH_COLS]
        return carry

    lax.fori_loop(0, n_chunks, proj_chunk, 0)

    for kt in range(SSM_KT):
        c_lo = kt * KT_COLS

        def bu_chunk(c, carry, kt=kt):
            r0 = pl.multiple_of(c * SSM_CHUNK, SSM_CHUNK)
            ub = us[pl.ds(r0, SSM_CHUNK), kt * MXU_DIM:(kt + 1) * MXU_DIM].astype(BF16)
            hbuf[pl.ds(r0, SSM_CHUNK), :] = _dot(ub, bb_ref[kt])
            return carry

        lax.fori_loop(0, rows // SSM_CHUNK, bu_chunk, 0)

        half = KT_COLS // 2
        for ch in range(2):
            lo = ch * half
            mix_head = kt * 2 + ch
            a_r = jnp.broadcast_to(are_ref[:, c_lo + lo:c_lo + lo + half], (SUBLANES, half))
            a_i = jnp.broadcast_to(aim_ref[:, c_lo + lo:c_lo + lo + half], (SUBLANES, half))

            def scan_block(j, carry, lo=lo, a_r=a_r, a_i=a_i, c_lo=c_lo, mix_head=mix_head):
                for q in range(groups_per_block):
                    if groups_per_block == 1 and steps < npos:
                        t0, b0 = j * steps, 0
                    else:
                        t0, b0 = 0, pl.multiple_of((j * groups_per_block + q) * SUBLANES, SUBLANES)
                    h_r = hcre[pl.ds(b0, SUBLANES), c_lo + lo:c_lo + lo + half]
                    h_i = hcim[pl.ds(b0, SUBLANES), c_lo + lo:c_lo + lo + half]
                    for t in range(steps):
                        row = pl.multiple_of((t0 + t) * nb + b0, SUBLANES)
                        b_r = hbuf[pl.ds(row, SUBLANES), lo:lo + half]
                        b_i = hbuf[pl.ds(row, SUBLANES), KT_COLS + lo:KT_COLS + lo + half]
                        h_r, h_i = (a_r * h_r - a_i * h_i + b_r,
                                    a_r * h_i + a_i * h_r + b_i)
                        hbuf[pl.ds(row, SUBLANES), lo:lo + half] = h_r
                        hbuf[pl.ds(row, SUBLANES), KT_COLS + lo:KT_COLS + lo + half] = h_i
                    hcre[pl.ds(b0, SUBLANES), c_lo + lo:c_lo + lo + half] = h_r
                    hcim[pl.ds(b0, SUBLANES), c_lo + lo:c_lo + lo + half] = h_i
                m0 = pl.multiple_of(j * mix_rows, mix_rows)
                mixed = _dot(wk[mix_head, pl.ds(m0, mix_rows), :], vnb[...])
                head = lax.broadcasted_iota(jnp.int32, (1, GMLP_WIDTH), 1) // GMLP_HEAD_DIM
                sacc[pl.ds(m0, mix_rows), :] = (sacc[pl.ds(m0, mix_rows), :]
                                                + jnp.where(head == mix_head, mixed, 0.0))
                return carry

            lax.fori_loop(0, SCAN_BLOCKS, scan_block, 0)

        def y_chunk(c, carry, kt=kt):
            r0 = pl.multiple_of(c * SSM_CHUNK, SSM_CHUNK)
            hb = hbuf[pl.ds(r0, SSM_CHUNK), :].astype(BF16)
            u = us[pl.ds(r0, SSM_CHUNK), kt * MXU_DIM:(kt + 1) * MXU_DIM]
            y = _dot(hb, cc_ref[kt]) + dsk_ref[:, kt * MXU_DIM:(kt + 1) * MXU_DIM] * u
            ya[pl.ds(r0, SSM_CHUNK), kt * MXU_DIM:(kt + 1) * MXU_DIM] = y
            return carry

        lax.fori_loop(0, rows // SSM_CHUNK, y_chunk, 0)

    base_pos = start_pos + tile * npos

    def out_chunk(c, carry):
        r0 = pl.multiple_of(c * ROW_CHUNK, ROW_CHUNK)
        z = _gelu(ya[pl.ds(r0, ROW_CHUNK), :])
        o_a = z * _sigmoid(_dot(z.astype(BF16), wglu_ref[...]) + bglu_ref[...])
        ycat_ref[pl.ds(r0, ROW_CHUNK), 0:SSM_WIDTH] = o_a.astype(BF16)
        y_b = gu[pl.ds(r0, ROW_CHUNK), :] * sacc[pl.ds(r0, ROW_CHUNK), :]
        ycat_ref[pl.ds(r0, ROW_CHUNK), SSM_WIDTH:SSM_WIDTH + GMLP_WIDTH] = y_b.astype(BF16)

        ridx = lax.broadcasted_iota(jnp.int32, (ROW_CHUNK, 1), 0) + r0
        pos = base_pos + (ridx >> shift)
        for g, win in enumerate(POOL_WINDOWS):
            cl = g * POOL_GROUP_CH
            cur = xp[pl.ds(past_rows + r0, ROW_CHUNK), cl:cl + POOL_GROUP_CH]
            wsum = cur
            for j in range(1, win):
                off = pl.multiple_of(past_rows - j * nb + r0, SUBLANES)
                wsum = wsum + xp[pl.ds(off, ROW_CHUNK), cl:cl + POOL_GROUP_CH]
            cnt = jnp.minimum(pos + 1, win).astype(F32)
            d = wsum / cnt - cur
            y_c = _dot(d.astype(BF16), poolw_ref[g]) * pscale_ref[:, cl:cl + POOL_GROUP_CH]
            col = SSM_WIDTH + GMLP_WIDTH + cl
            ycat_ref[pl.ds(r0, ROW_CHUNK), col:col + POOL_GROUP_CH] = y_c.astype(BF16)
        return carry

    lax.fori_loop(0, n_chunks, out_chunk, 0)

    pool_out_ref[...] = xp[rows:rows + past_rows, :]
    if n_tiles > 1:
        xp[0:past_rows, :] = xp[rows:rows + past_rows, :]


def _branches_call(x, h0re, h0im, ppast, p, layer, bs, *, nb, npos, start_pos):
    n_rows = x.shape[0]
    rows = nb * npos
    n_tiles = n_rows // rows
    past_rows = POOL_PAST * nb
    kern = functools.partial(_branches_kernel, nb=nb, npos=npos, n_tiles=n_tiles, start_pos=start_pos)
    row_tile = lambda w: pl.BlockSpec((rows, w), lambda i: (i, 0))
    fixed = lambda r, w: pl.BlockSpec((r, w), lambda i: (0, 0))
    consts = (p['w_in_br'], p['bb'], p['a_re'], p['a_im'], p['cc'], p['d_skip'],
              p['w_glu'], p['b_glu'], p['gln_g'], p['gln_b'], p['w_s'], bs, p['pool_w'], p['pool_scale'])
    return pl.pallas_call(
        kern,
        grid=(n_tiles,),
        in_specs=[row_tile(D_MODEL), _whole(), _whole(), _whole()] + [_layer_const(c, layer) for c in consts],
        out_specs=[row_tile(BRANCH_COLS - GMLP_WIDTH), fixed(nb, SSM_COLS), fixed(nb, SSM_COLS),
                   row_tile(GMLP_WIDTH), fixed(past_rows, POOL_WIDTH)],
        out_shape=[jax.ShapeDtypeStruct((n_rows, BRANCH_COLS - GMLP_WIDTH), BF16),
                   jax.ShapeDtypeStruct((nb, SSM_COLS), F32),
                   jax.ShapeDtypeStruct((nb, SSM_COLS), F32),
                   jax.ShapeDtypeStruct((n_rows, GMLP_WIDTH), F32),
                   jax.ShapeDtypeStruct((past_rows, POOL_WIDTH), F32)],
        scratch_shapes=[pltpu.VMEM((rows, 2 * KT_COLS), F32),
                        pltpu.VMEM((rows, SSM_WIDTH), F32),
                        pltpu.VMEM((rows, SSM_WIDTH), F32),
                        pltpu.VMEM((rows, GMLP_WIDTH), F32),
                        pltpu.VMEM((rows, GMLP_WIDTH), BF16),
                        pltpu.VMEM((rows, GMLP_WIDTH), F32),
                        pltpu.VMEM((rows + past_rows, POOL_WIDTH), F32),
                        pltpu.VMEM((GMLP_HEADS, rows, rows), BF16)],
        compiler_params=pltpu.CompilerParams(dimension_semantics=("arbitrary",),
                                             vmem_limit_bytes=VMEM_LIMIT_BYTES),
        name="branches",
    )(x, h0re, h0im, ppast, *consts)


def _merge_ffn_kernel(x_ref, ycat_ref, cpast_ref,
                      wg_ref, bg_ref, wbr_ref, wo_ref, g1_ref, b1_ref,
                      wup_ref, cw_ref, cb_ref, wdn_ref, g2_ref, b2_ref,
                      x2_ref, cout_ref, tailbuf, act, *, nb, rows):
    tail = (FFN_CONV - 1) * nb
    assert rows >= tail

    @pl.when(pl.program_id(0) == 0)
    def _init():
        tailbuf[...] = cpast_ref[...]

    x = x_ref[...]
    xb = x.astype(BF16)
    yc = ycat_ref[...]
    bounds = (0, SSM_WIDTH, SSM_WIDTH + GMLP_WIDTH, BRANCH_COLS - GMLP_WIDTH)
    merged = None
    for i in range(N_BRANCH):
        gate = _sigmoid(_dot(xb, wg_ref[:, i * D_MODEL:(i + 1) * D_MODEL])
                        + bg_ref[:, i * D_MODEL:(i + 1) * D_MODEL])
        br = _dot(yc[:, bounds[i]:bounds[i + 1]], wbr_ref[bounds[i]:bounds[i + 1], :])
        merged = gate * br if merged is None else merged + gate * br
    mix = _dot(merged.astype(BF16), wo_ref[...])
    x1 = _layer_norm(ALPHA * x + mix, g1_ref[...], b1_ref[...])

    x1b = x1.astype(BF16)
    for cc in range(D_FF // FF_COL_CHUNK):
        conv = []
        for part in range(2):
            lo = part * D_FF + cc * FF_COL_CHUNK
            cols = slice(lo, lo + FF_COL_CHUNK)
            h = _dot(x1b, wup_ref[:, cols])
            he = jnp.concatenate([tailbuf[:, cols], h], axis=0)
            tailbuf[:, cols] = h[rows - tail:rows, :]
            y = cb_ref[:, cols] + he[0:rows, :] * cw_ref[0:1, cols]
            y = y + he[nb:nb + rows, :] * cw_ref[1:2, cols]
            y = y + h * cw_ref[2:3, cols]
            conv.append(y)
        a = _gelu(conv[1]) * conv[0]
        act[:, cc * FF_COL_CHUNK:(cc + 1) * FF_COL_CHUNK] = a.astype(BF16)
    ff = _dot(act[...], wdn_ref[...])
    x2_ref[...] = _layer_norm(ALPHA * x1 + ff, g2_ref[...], b2_ref[...])
    cout_ref[...] = tailbuf[...]


def _merge_ffn_call(x, ycat, cpast, p, layer, *, nb, rows):
    n_rows = x.shape[0]
    tail = (FFN_CONV - 1) * nb
    kern = functools.partial(_merge_ffn_kernel, nb=nb, rows=rows)
    consts = (p['w_in_gate'], p['b_gate'], p['w_br'], p['w_o'], p['ln1_g'], p['ln1_b'],
              p['w_up'], p['conv_w'], p['conv_b'], p['w_down'], p['ln2_g'], p['ln2_b'])
    return pl.pallas_call(
        kern,
        grid=(n_rows // rows,),
        in_specs=[pl.BlockSpec((rows, D_MODEL), lambda i: (i, 0)),
                  pl.BlockSpec((rows, BRANCH_COLS - GMLP_WIDTH), lambda i: (i, 0)), _whole()]
                 + [_layer_const(c, layer) for c in consts],
        out_specs=[pl.BlockSpec((rows, D_MODEL), lambda i: (i, 0)),
                   pl.BlockSpec((tail, F2), lambda i: (0, 0))],
        out_shape=[jax.ShapeDtypeStruct((n_rows, D_MODEL), F32),
                   jax.ShapeDtypeStruct((tail, F2), F32)],
        scratch_shapes=[pltpu.VMEM((tail, F2), F32),
                        pltpu.VMEM((rows, D_FF), BF16)],
        compiler_params=pltpu.CompilerParams(dimension_semantics=("arbitrary",),
                                             vmem_limit_bytes=VMEM_LIMIT_BYTES),
        name="merge_convffn",
    )(x, ycat, cpast, *consts)


def _ssm_discretise(a_re, a_im, log_step, b_re, b_im):
    dt = jnp.exp(log_step)[..., None]
    mag = jnp.exp(a_re * dt)
    ab_re = mag * jnp.cos(a_im * dt)
    ab_im = mag * jnp.sin(a_im * dt)
    den = jnp.square(a_re) + jnp.square(a_im)
    nr = ab_re - 1.0
    k_re = (nr * a_re + ab_im * a_im) / den
    k_im = (ab_im * a_re - nr * a_im) / den
    bb_re = k_re[..., None] * b_re - k_im[..., None] * b_im
    bb_im = k_re[..., None] * b_im + k_im[..., None] * b_re
    return ab_re, ab_im, bb_re, bb_im


def _block_diag_in(m):
    gpt = SSM_GROUPS // SSM_KT
    m = jnp.transpose(m, (0, 1, 3, 2)).reshape(-1, SSM_KT, gpt, SSM_GROUP_CH, SSM_STATE)
    eye = jnp.eye(gpt, dtype=m.dtype)
    return jnp.einsum('lkgcn,gh->lkgchn', m, eye).reshape(-1, SSM_KT, gpt * SSM_GROUP_CH, gpt * SSM_STATE)


def _block_diag_out(m):
    gpt = SSM_GROUPS // SSM_KT
    m = jnp.transpose(m, (0, 1, 3, 2)).reshape(-1, SSM_KT, gpt, SSM_STATE, SSM_GROUP_CH)
    eye = jnp.eye(gpt, dtype=m.dtype)
    return jnp.einsum('lkgnc,gh->lkgnhc', m, eye).reshape(-1, SSM_KT, gpt * SSM_STATE, gpt * SSM_GROUP_CH)


def _gmlp_bias_rows(b_s, cl, nb):
    b = jnp.transpose(b_s[:, :, :cl], (0, 2, 1))
    return jnp.repeat(jnp.repeat(b, GMLP_HEAD_DIM, axis=2), nb, axis=1)


def _t_major(a):
    return jnp.transpose(a, (1, 0, 2)).reshape(a.shape[0] * a.shape[1], a.shape[2])


def _seq_major(a, n_seq):
    return jnp.transpose(a.reshape(a.shape[0] // n_seq, n_seq, a.shape[1]), (1, 0, 2))


def _group_layer(x, st_re, st_im, st_pool, st_conv, p, layer, bs, *, nb, npos, start_pos, ffn_rows):
    ycat, h_re, h_im, vn, pool_new = _branches_call(x, st_re, st_im, st_pool, p, layer, bs,
                                                     nb=nb, npos=npos, start_pos=start_pos)
    x2, conv_new = _merge_ffn_call(x, ycat, st_conv, p, layer, nb=nb, rows=ffn_rows)
    return x2, h_re, h_im, vn, pool_new, conv_new


def kernel(x_prompt, x_sample, state_ssm_re, state_ssm_im, state_pool, state_ffn_conv, w_in, b_gate, ssm_a_re, ssm_a_im, ssm_log_step, ssm_b_re, ssm_b_im, ssm_c_re, ssm_c_im, ssm_d, ssm_w_glu, ssm_b_glu, gmlp_ln_g, gmlp_ln_b, gmlp_w_s, gmlp_b_s, pool_w, pool_scale, w_br_ssm, w_br_gmlp, w_br_pool, w_o, ln1_g, ln1_b, ffn_w_up, ffn_conv_w, ffn_conv_b, ffn_w_down, ln2_g, ln2_b):
    n_p, t_p, _ = x_prompt.shape
    n_s, t_s, _ = x_sample.shape
    hp = _t_major(x_prompt)
    hs = _t_major(x_sample)
    npos_p = TILE_ROWS // n_p
    npos_s = TILE_ROWS // n_s
    assert npos_p == GMLP_CHUNK and npos_s == t_s and t_p % npos_p == 0

    row = lambda v: v.reshape(DEPTH, 1, -1)
    ab_re, ab_im, bb_re, bb_im = _ssm_discretise(ssm_a_re, ssm_a_im, ssm_log_step, ssm_b_re, ssm_b_im)
    p = dict(
        w_in_br=w_in[:, :, :BRANCH_COLS].astype(BF16),
        w_in_gate=w_in[:, :, BRANCH_COLS:].astype(BF16),
        b_gate=row(b_gate),
        bb=jnp.concatenate([_block_diag_in(bb_re), _block_diag_in(bb_im)], axis=3).astype(BF16),
        a_re=row(ab_re), a_im=row(ab_im),
        cc=jnp.concatenate([_block_diag_out(ssm_c_re), -_block_diag_out(ssm_c_im)], axis=2).astype(BF16),
        d_skip=row(ssm_d),
        w_glu=ssm_w_glu.astype(BF16), b_glu=row(ssm_b_glu),
        gln_g=row(gmlp_ln_g), gln_b=row(gmlp_ln_b),
        w_s=gmlp_w_s, pool_w=pool_w.astype(BF16), pool_scale=row(pool_scale),
        w_br=jnp.concatenate([w_br_ssm, w_br_gmlp, w_br_pool], axis=1).astype(BF16),
        w_o=w_o.astype(BF16), ln1_g=row(ln1_g), ln1_b=row(ln1_b),
        w_up=ffn_w_up.astype(BF16), conv_w=ffn_conv_w, conv_b=row(ffn_conv_b),
        w_down=ffn_w_down.astype(BF16), ln2_g=row(ln2_g), ln2_b=row(ln2_b),
    )
    bs_p = _gmlp_bias_rows(gmlp_b_s, npos_p, n_p)
    bs_s = _gmlp_bias_rows(gmlp_b_s, npos_s, n_s)
    zeros_p = (jnp.zeros((n_p, SSM_COLS), F32), jnp.zeros((n_p, SSM_COLS), F32),
               jnp.zeros((POOL_PAST * n_p, POOL_WIDTH), F32), jnp.zeros(((FFN_CONV - 1) * n_p, F2), F32))

    outs = [[] for _ in range(9)]
    for l in range(DEPTH):
        hp, a_re_n, a_im_n, _, pool_n, conv_n = _group_layer(
            hp, *zeros_p, p, l, bs_p, nb=n_p, npos=npos_p, start_pos=0, ffn_rows=2 * ROW_CHUNK)
        hs, b_re_n, b_im_n, v_n, pool_m, conv_m = _group_layer(
            hs, state_ssm_re[l].reshape(n_s, SSM_COLS), state_ssm_im[l].reshape(n_s, SSM_COLS),
            _t_major(state_pool[l]), _t_major(state_ffn_conv[l]),
            p, l, bs_s, nb=n_s, npos=npos_s, start_pos=PAST_LEN, ffn_rows=ROW_CHUNK)

        vals = (a_re_n.reshape(n_p, SSM_GROUPS, SSM_STATE), a_im_n.reshape(n_p, SSM_GROUPS, SSM_STATE),
                _seq_major(pool_n, n_p), _seq_major(conv_n, n_p),
                b_re_n.reshape(n_s, SSM_GROUPS, SSM_STATE), b_im_n.reshape(n_s, SSM_GROUPS, SSM_STATE),
                _seq_major(v_n, n_s), _seq_major(pool_m, n_s), _seq_major(conv_m, n_s))
        for o, v in zip(outs, vals):
            o.append(v)

    return (_seq_major(hp, n_p), _seq_major(hs, n_s)) + tuple(jnp.stack(o) for o in outs)
```

```python
import functools
import math

import jax
import jax.numpy as jnp
from jax import lax
from jax.experimental import pallas as pl
from jax.experimental.pallas import tpu as pltpu

F32 = jnp.float32
BF16 = jnp.bfloat16

D_MODEL = 1024
DEPTH = 2
PAST_LEN = 16384
SSM_GROUP_CH = 16
SSM_WIDTH = 512
SSM_GROUPS = 32
SSM_STATE = 64
SSM_COLS = SSM_GROUPS * SSM_STATE
GMLP_HEADS = 4
GMLP_CHUNK = 128
GMLP_WIDTH = 256
GMLP_HEAD_DIM = 64
POOL_WINDOWS = (2, 4, 8, 16)
POOL_WIDTH = 512
POOL_GROUP_CH = 128
POOL_PAST = 15
N_BRANCH = 3
BRANCH_COLS = SSM_WIDTH + 2 * GMLP_WIDTH + POOL_WIDTH
D_FF = 2816
F2 = 2 * D_FF
FFN_CONV = 3
LN_EPS = 1e-5
ALPHA = (2 * DEPTH) ** 0.25

SUBLANES = 8
MXU_DIM = 256
VMEM_LIMIT_BYTES = 60 * 1024 * 1024

TILE_ROWS = 1024
ROW_CHUNK = 256
SSM_CHUNK = 512
SCAN_BLOCKS = 2
SSM_KT = SSM_WIDTH // MXU_DIM
KT_COLS = SSM_COLS // SSM_KT
FF_COL_CHUNK = MXU_DIM
assert 2 * SSM_KT == GMLP_HEADS


def _gelu(x):
    c = math.sqrt(2.0 / math.pi)
    return 0.5 * x * (1.0 + jnp.tanh(c * (x + 0.044715 * (x * x * x))))


def _sigmoid(x):
    return 1.0 / (1.0 + jnp.exp(-x))


def _layer_norm(x, g, b):
    mu = jnp.mean(x, axis=-1, keepdims=True)
    xc = x - mu
    var = jnp.mean(xc * xc, axis=-1, keepdims=True)
    return xc * lax.rsqrt(var + LN_EPS) * g + b


def _dot(a, b):
    return jnp.dot(a, b, preferred_element_type=F32)


def _load_rows_t_major(x_ref, r0, n, nb, seq_major):
    if not seq_major:
        return x_ref[pl.ds(r0, n), :]
    p0 = r0 // nb if isinstance(r0, int) else pl.multiple_of(r0 >> int(math.log2(nb)), SUBLANES)
    blk = x_ref[:, pl.ds(p0, n // nb), :]
    return jnp.swapaxes(blk, 0, 1).reshape(n, blk.shape[-1])


def _whole():
    return pl.BlockSpec(memory_space=pltpu.VMEM)


def _layer_const(a, layer):
    nd = a.ndim - 1
    return pl.BlockSpec((None,) + a.shape[1:], lambda i: (layer,) + (0,) * nd,
                        pipeline_mode=pl.Buffered(1))


def _expand_mix(ws_ref, wk, rows, shift, nb):
    cl = GMLP_CHUNK
    e = (lax.broadcasted_iota(jnp.int32, (rows, cl), 0) >> shift
         == lax.broadcasted_iota(jnp.int32, (rows, cl), 1)).astype(BF16)
    tril = (lax.broadcasted_iota(jnp.int32, (cl, cl), 0)
            >= lax.broadcasted_iota(jnp.int32, (cl, cl), 1))
    for h in range(GMLP_HEADS):
        w = jnp.where(tril, ws_ref[h], 0.0).astype(BF16)
        rows_w = _dot(e, w).astype(BF16)
        for c0 in range(0, rows, MXU_DIM):
            et = (lax.broadcasted_iota(jnp.int32, (cl, MXU_DIM), 0)
                  == (lax.broadcasted_iota(jnp.int32, (cl, MXU_DIM), 1) + c0) >> shift).astype(BF16)
            full = _dot(rows_w, et)
            same_seq = ((lax.broadcasted_iota(jnp.int32, (rows, MXU_DIM), 0) & (nb - 1))
                        == ((lax.broadcasted_iota(jnp.int32, (rows, MXU_DIM), 1) + c0) & (nb - 1)))
            wk[h, :, c0:c0 + MXU_DIM] = jnp.where(same_seq, full, 0.0).astype(BF16)


def _branches_kernel(x_ref, h0re_ref, h0im_ref, ppast_ref,
                     w_in_ref, bb_ref, are_ref, aim_ref, cc_ref, dsk_ref,
                     wglu_ref, bglu_ref, lng_ref, lnb_ref, ws_ref, bs_ref,
                     poolw_ref, pscale_ref,
                     ycat_ref, hcre, hcim, vn_out_ref, pool_out_ref,
                     hbuf, us, ya, gu, vnb, sacc, xp, wk,
                     *, nb, npos, n_tiles, start_pos, seq_major_in):
    rows = nb * npos
    n_chunks = rows // ROW_CHUNK
    past_rows = POOL_PAST * nb
    tile = pl.program_id(0)
    shift = int(math.log2(nb))
    slabs_per_block = rows // SUBLANES // SCAN_BLOCKS
    steps = min(npos, slabs_per_block)
    groups_per_block = slabs_per_block // steps
    assert groups_per_block == 1 or steps == npos
    mix_rows = rows // SCAN_BLOCKS

    @pl.when(tile == 0)
    def _init():
        hcre[...] = h0re_ref[...]
        hcim[...] = h0im_ref[...]
        xp[0:past_rows, :] = ppast_ref[...]
        _expand_mix(ws_ref, wk, rows, shift, nb)

    def proj_chunk(c, carry):
        r0 = pl.multiple_of(c * ROW_CHUNK, ROW_CHUNK)
        xb = _load_rows_t_major(x_ref, r0, ROW_CHUNK, nb, seq_major_in).astype(BF16)
        pa = _dot(xb, w_in_ref[...])
        us[pl.ds(r0, ROW_CHUNK), :] = pa[:, 0:SSM_WIDTH]
        gu[pl.ds(r0, ROW_CHUNK), :] = _gelu(pa[:, SSM_WIDTH:SSM_WIDTH + GMLP_WIDTH])
        v = _gelu(pa[:, SSM_WIDTH + GMLP_WIDTH:SSM_WIDTH + 2 * GMLP_WIDTH])
        v = _layer_norm(v, lng_ref[...], lnb_ref[...])
        vn_out_ref[pl.ds(r0, ROW_CHUNK), :] = v
        vnb[pl.ds(r0, ROW_CHUNK), :] = v.astype(BF16)
        sacc[pl.ds(r0, ROW_CHUNK), :] = bs_ref[pl.ds(r0, ROW_CHUNK), :]
        xp[pl.ds(past_rows + r0, ROW_CHUNK), :] = pa[:, SSM_WIDTH + 2 * GMLP_WIDTH:BRANCH_COLS]
        return carry

    lax.fori_loop(0, n_chunks, proj_chunk, 0)

    for kt in range(SSM_KT):
        c_lo = kt * KT_COLS

        def bu_chunk(c, carry, kt=kt):
            r0 = pl.multiple_of(c * SSM_CHUNK, SSM_CHUNK)
            ub = us[pl.ds(r0, SSM_CHUNK), kt * MXU_DIM:(kt + 1) * MXU_DIM].astype(BF16)
            hbuf[pl.ds(r0, SSM_CHUNK), :] = _dot(ub, bb_ref[kt])
            return carry

        lax.fori_loop(0, rows // SSM_CHUNK, bu_chunk, 0)

        half = KT_COLS // 2
        for ch in range(2):
            lo = ch * half
            mix_head = kt * 2 + ch
            a_r = jnp.broadcast_to(are_ref[:, c_lo + lo:c_lo + lo + half], (SUBLANES, half))
            a_i = jnp.broadcast_to(aim_ref[:, c_lo + lo:c_lo + lo + half], (SUBLANES, half))

            def scan_block(j, carry, lo=lo, a_r=a_r, a_i=a_i, c_lo=c_lo, mix_head=mix_head):
                for q in range(groups_per_block):
                    if groups_per_block == 1 and steps < npos:
                        t0, b0 = j * steps, 0
                    else:
                        t0, b0 = 0, pl.multiple_of((j * groups_per_block + q) * SUBLANES, SUBLANES)
                    h_r = hcre[pl.ds(b0, SUBLANES), c_lo + lo:c_lo + lo + half]
                    h_i = hcim[pl.ds(b0, SUBLANES), c_lo + lo:c_lo + lo + half]
                    for t in range(steps):
                        row = pl.multiple_of((t0 + t) * nb + b0, SUBLANES)
                        b_r = hbuf[pl.ds(row, SUBLANES), lo:lo + half]
                        b_i = hbuf[pl.ds(row, SUBLANES), KT_COLS + lo:KT_COLS + lo + half]
                        h_r, h_i = (a_r * h_r - a_i * h_i + b_r,
                                    a_r * h_i + a_i * h_r + b_i)
                        hbuf[pl.ds(row, SUBLANES), lo:lo + half] = h_r
                        hbuf[pl.ds(row, SUBLANES), KT_COLS + lo:KT_COLS + lo + half] = h_i
                    hcre[pl.ds(b0, SUBLANES), c_lo + lo:c_lo + lo + half] = h_r
                    hcim[pl.ds(b0, SUBLANES), c_lo + lo:c_lo + lo + half] = h_i
                m0 = pl.multiple_of(j * mix_rows, mix_rows)
                mixed = _dot(wk[mix_head, pl.ds(m0, mix_rows), :], vnb[...])
                head = lax.broadcasted_iota(jnp.int32, (1, GMLP_WIDTH), 1) // GMLP_HEAD_DIM
                sacc[pl.ds(m0, mix_rows), :] = (sacc[pl.ds(m0, mix_rows), :]
                                                + jnp.where(head == mix_head, mixed, 0.0))
                return carry

            lax.fori_loop(0, SCAN_BLOCKS, scan_block, 0)

        def y_chunk(c, carry, kt=kt):
            r0 = pl.multiple_of(c * SSM_CHUNK, SSM_CHUNK)
            hb = hbuf[pl.ds(r0, SSM_CHUNK), :].astype(BF16)
            u = us[pl.ds(r0, SSM_CHUNK), kt * MXU_DIM:(kt + 1) * MXU_DIM]
            y = _dot(hb, cc_ref[kt]) + dsk_ref[:, kt * MXU_DIM:(kt + 1) * MXU_DIM] * u
            ya[pl.ds(r0, SSM_CHUNK), kt * MXU_DIM:(kt + 1) * MXU_DIM] = y
            return carry

        lax.fori_loop(0, rows // SSM_CHUNK, y_chunk, 0)

    base_pos = start_pos + tile * npos

    def out_chunk(c, carry):
        r0 = pl.multiple_of(c * ROW_CHUNK, ROW_CHUNK)
        z = _gelu(ya[pl.ds(r0, ROW_CHUNK), :])
        o_a = z * _sigmoid(_dot(z.astype(BF16), wglu_ref[...]) + bglu_ref[...])
        ycat_ref[pl.ds(r0, ROW_CHUNK), 0:SSM_WIDTH] = o_a.astype(BF16)
        y_b = gu[pl.ds(r0, ROW_CHUNK), :] * sacc[pl.ds(r0, ROW_CHUNK), :]
        ycat_ref[pl.ds(r0, ROW_CHUNK), SSM_WIDTH:SSM_WIDTH + GMLP_WIDTH] = y_b.astype(BF16)

        ridx = lax.broadcasted_iota(jnp.int32, (ROW_CHUNK, 1), 0) + r0
        pos = base_pos + (ridx >> shift)
        for g, win in enumerate(POOL_WINDOWS):
            cl = g * POOL_GROUP_CH
            cur = xp[pl.ds(past_rows + r0, ROW_CHUNK), cl:cl + POOL_GROUP_CH]
            wsum = cur
            for j in range(1, win):
                off = pl.multiple_of(past_rows - j * nb + r0, SUBLANES)
                wsum = wsum + xp[pl.ds(off, ROW_CHUNK), cl:cl + POOL_GROUP_CH]
            cnt = jnp.minimum(pos + 1, win).astype(F32)
            d = wsum / cnt - cur
            y_c = _dot(d.astype(BF16), poolw_ref[g]) * pscale_ref[:, cl:cl + POOL_GROUP_CH]
            col = SSM_WIDTH + GMLP_WIDTH + cl
            ycat_ref[pl.ds(r0, ROW_CHUNK), col:col + POOL_GROUP_CH] = y_c.astype(BF16)
        return carry

    lax.fori_loop(0, n_chunks, out_chunk, 0)

    pool_out_ref[...] = xp[rows:rows + past_rows, :]
    if n_tiles > 1:
        xp[0:past_rows, :] = xp[rows:rows + past_rows, :]


def _branches_call(x, h0re, h0im, ppast, p, layer, bs, *, nb, npos, start_pos):
    seq_major_in = x.ndim == 3
    n_rows = x.shape[0] * x.shape[1] if seq_major_in else x.shape[0]
    rows = nb * npos
    n_tiles = n_rows // rows
    past_rows = POOL_PAST * nb
    kern = functools.partial(_branches_kernel, nb=nb, npos=npos, n_tiles=n_tiles, start_pos=start_pos,
                             seq_major_in=seq_major_in)
    row_tile = lambda w: pl.BlockSpec((rows, w), lambda i: (i, 0))
    x_spec = pl.BlockSpec((nb, npos, D_MODEL), lambda i: (0, i, 0)) if seq_major_in else row_tile(D_MODEL)
    fixed = lambda r, w: pl.BlockSpec((r, w), lambda i: (0, 0))
    consts = (p['w_in_br'], p['bb'], p['a_re'], p['a_im'], p['cc'], p['d_skip'],
              p['w_glu'], p['b_glu'], p['gln_g'], p['gln_b'], p['w_s'], bs, p['pool_w'], p['pool_scale'])
    return pl.pallas_call(
        kern,
        grid=(n_tiles,),
        in_specs=[x_spec, _whole(), _whole(), _whole()] + [_layer_const(c, layer) for c in consts],
        out_specs=[row_tile(BRANCH_COLS - GMLP_WIDTH), fixed(nb, SSM_COLS), fixed(nb, SSM_COLS),
                   row_tile(GMLP_WIDTH), fixed(past_rows, POOL_WIDTH)],
        out_shape=[jax.ShapeDtypeStruct((n_rows, BRANCH_COLS - GMLP_WIDTH), BF16),
                   jax.ShapeDtypeStruct((nb, SSM_COLS), F32),
                   jax.ShapeDtypeStruct((nb, SSM_COLS), F32),
                   jax.ShapeDtypeStruct((n_rows, GMLP_WIDTH), F32),
                   jax.ShapeDtypeStruct((past_rows, POOL_WIDTH), F32)],
        scratch_shapes=[pltpu.VMEM((rows, 2 * KT_COLS), F32),
                        pltpu.VMEM((rows, SSM_WIDTH), F32),
                        pltpu.VMEM((rows, SSM_WIDTH), F32),
                        pltpu.VMEM((rows, GMLP_WIDTH), F32),
                        pltpu.VMEM((rows, GMLP_WIDTH), BF16),
                        pltpu.VMEM((rows, GMLP_WIDTH), F32),
                        pltpu.VMEM((rows + past_rows, POOL_WIDTH), F32),
                        pltpu.VMEM((GMLP_HEADS, rows, rows), BF16)],
        compiler_params=pltpu.CompilerParams(dimension_semantics=("arbitrary",),
                                             vmem_limit_bytes=VMEM_LIMIT_BYTES),
        name="branches",
    )(x, h0re, h0im, ppast, *consts)


def _merge_ffn_kernel(x_ref, ycat_ref, cpast_ref,
                      wg_ref, bg_ref, wbr_ref, wo_ref, g1_ref, b1_ref,
                      wup_ref, cw_ref, cb_ref, wdn_ref, g2_ref, b2_ref,
                      x2_ref, cout_ref, tailbuf, act, *, nb, rows, seq_major_in, seq_major_out):
    tail = (FFN_CONV - 1) * nb
    assert rows >= tail

    @pl.when(pl.program_id(0) == 0)
    def _init():
        tailbuf[...] = cpast_ref[...]

    x = _load_rows_t_major(x_ref, 0, rows, nb, seq_major_in)
    xb = x.astype(BF16)
    yc = ycat_ref[...]
    bounds = (0, SSM_WIDTH, SSM_WIDTH + GMLP_WIDTH, BRANCH_COLS - GMLP_WIDTH)
    merged = None
    for i in range(N_BRANCH):
        gate = _sigmoid(_dot(xb, wg_ref[:, i * D_MODEL:(i + 1) * D_MODEL])
                        + bg_ref[:, i * D_MODEL:(i + 1) * D_MODEL])
        br = _dot(yc[:, bounds[i]:bounds[i + 1]], wbr_ref[bounds[i]:bounds[i + 1], :])
        merged = gate * br if merged is None else merged + gate * br
    mix = _dot(merged.astype(BF16), wo_ref[...])
    x1 = _layer_norm(ALPHA * x + mix, g1_ref[...], b1_ref[...])

    x1b = x1.astype(BF16)
    for cc in range(D_FF // FF_COL_CHUNK):
        conv = []
        for part in range(2):
            lo = part * D_FF + cc * FF_COL_CHUNK
            cols = slice(lo, lo + FF_COL_CHUNK)
            h = _dot(x1b, wup_ref[:, cols])
            he = jnp.concatenate([tailbuf[:, cols], h], axis=0)
            tailbuf[:, cols] = h[rows - tail:rows, :]
            y = cb_ref[:, cols] + he[0:rows, :] * cw_ref[0:1, cols]
            y = y + he[nb:nb + rows, :] * cw_ref[1:2, cols]
            y = y + h * cw_ref[2:3, cols]
            conv.append(y)
        a = _gelu(conv[1]) * conv[0]
        act[:, cc * FF_COL_CHUNK:(cc + 1) * FF_COL_CHUNK] = a.astype(BF16)
    ff = _dot(act[...], wdn_ref[...])
    x2 = _layer_norm(ALPHA * x1 + ff, g2_ref[...], b2_ref[...])
    if seq_major_out:
        x2_ref[...] = jnp.swapaxes(x2.reshape(rows // nb, nb, D_MODEL), 0, 1)
    else:
        x2_ref[...] = x2
    cout_ref[...] = tailbuf[...]


def _merge_ffn_call(x, ycat, cpast, p, layer, *, nb, rows, seq_major_out):
    seq_major_in = x.ndim == 3
    n_rows = ycat.shape[0]
    tail = (FFN_CONV - 1) * nb
    kern = functools.partial(_merge_ffn_kernel, nb=nb, rows=rows,
                             seq_major_in=seq_major_in, seq_major_out=seq_major_out)
    consts = (p['w_in_gate'], p['b_gate'], p['w_br'], p['w_o'], p['ln1_g'], p['ln1_b'],
              p['w_up'], p['conv_w'], p['conv_b'], p['w_down'], p['ln2_g'], p['ln2_b'])
    t_major_spec = pl.BlockSpec((rows, D_MODEL), lambda i: (i, 0))
    seq_major_spec = pl.BlockSpec((nb, rows // nb, D_MODEL), lambda i: (0, i, 0))
    return pl.pallas_call(
        kern,
        grid=(n_rows // rows,),
        in_specs=[seq_major_spec if seq_major_in else t_major_spec,
                  pl.BlockSpec((rows, BRANCH_COLS - GMLP_WIDTH), lambda i: (i, 0)), _whole()]
                 + [_layer_const(c, layer) for c in consts],
        out_specs=[seq_major_spec if seq_major_out else t_major_spec,
                   pl.BlockSpec((tail, F2), lambda i: (0, 0))],
        out_shape=[jax.ShapeDtypeStruct((nb, n_rows // nb, D_MODEL) if seq_major_out else (n_rows, D_MODEL), F32),
                   jax.ShapeDtypeStruct((tail, F2), F32)],
        scratch_shapes=[pltpu.VMEM((tail, F2), F32),
                        pltpu.VMEM((rows, D_FF), BF16)],
        compiler_params=pltpu.CompilerParams(dimension_semantics=("arbitrary",),
                                             vmem_limit_bytes=VMEM_LIMIT_BYTES),
        name="merge_convffn",
    )(x, ycat, cpast, *consts)


def _ssm_discretise(a_re, a_im, log_step, b_re, b_im):
    dt = jnp.exp(log_step)[..., None]
    mag = jnp.exp(a_re * dt)
    ab_re = mag * jnp.cos(a_im * dt)
    ab_im = mag * jnp.sin(a_im * dt)
    den = jnp.square(a_re) + jnp.square(a_im)
    nr = ab_re - 1.0
    k_re = (nr * a_re + ab_im * a_im) / den
    k_im = (ab_im * a_re - nr * a_im) / den
    bb_re = k_re[..., None] * b_re - k_im[..., None] * b_im
    bb_im = k_re[..., None] * b_im + k_im[..., None] * b_re
    return ab_re, ab_im, bb_re, bb_im


def _block_diag_in(m):
    gpt = SSM_GROUPS // SSM_KT
    m = jnp.transpose(m, (0, 1, 3, 2)).reshape(-1, SSM_KT, gpt, SSM_GROUP_CH, SSM_STATE)
    eye = jnp.eye(gpt, dtype=m.dtype)
    return jnp.einsum('lkgcn,gh->lkgchn', m, eye).reshape(-1, SSM_KT, gpt * SSM_GROUP_CH, gpt * SSM_STATE)


def _block_diag_out(m):
    gpt = SSM_GROUPS // SSM_KT
    m = jnp.transpose(m, (0, 1, 3, 2)).reshape(-1, SSM_KT, gpt, SSM_STATE, SSM_GROUP_CH)
    eye = jnp.eye(gpt, dtype=m.dtype)
    return jnp.einsum('lkgnc,gh->lkgnhc', m, eye).reshape(-1, SSM_KT, gpt * SSM_STATE, gpt * SSM_GROUP_CH)


def _gmlp_bias_rows(b_s, cl, nb):
    b = jnp.transpose(b_s[:, :, :cl], (0, 2, 1))
    return jnp.repeat(jnp.repeat(b, GMLP_HEAD_DIM, axis=2), nb, axis=1)


def _t_major(a):
    return jnp.transpose(a, (1, 0, 2)).reshape(a.shape[0] * a.shape[1], a.shape[2])


def _seq_major(a, n_seq):
    return jnp.transpose(a.reshape(a.shape[0] // n_seq, n_seq, a.shape[1]), (1, 0, 2))


def _group_layer(x, st_re, st_im, st_pool, st_conv, p, layer, bs, *, nb, npos, start_pos, ffn_rows,
                 seq_major_out=False):
    ycat, h_re, h_im, vn, pool_new = _branches_call(x, st_re, st_im, st_pool, p, layer, bs,
                                                     nb=nb, npos=npos, start_pos=start_pos)
    x2, conv_new = _merge_ffn_call(x, ycat, st_conv, p, layer, nb=nb, rows=ffn_rows,
                                   seq_major_out=seq_major_out)
    return x2, h_re, h_im, vn, pool_new, conv_new


def kernel(x_prompt, x_sample, state_ssm_re, state_ssm_im, state_pool, state_ffn_conv, w_in, b_gate, ssm_a_re, ssm_a_im, ssm_log_step, ssm_b_re, ssm_b_im, ssm_c_re, ssm_c_im, ssm_d, ssm_w_glu, ssm_b_glu, gmlp_ln_g, gmlp_ln_b, gmlp_w_s, gmlp_b_s, pool_w, pool_scale, w_br_ssm, w_br_gmlp, w_br_pool, w_o, ln1_g, ln1_b, ffn_w_up, ffn_conv_w, ffn_conv_b, ffn_w_down, ln2_g, ln2_b):
    n_p, t_p, _ = x_prompt.shape
    n_s, t_s, _ = x_sample.shape
    hp = x_prompt
    hs = _t_major(x_sample)
    npos_p = TILE_ROWS // n_p
    npos_s = TILE_ROWS // n_s
    assert npos_p == GMLP_CHUNK and npos_s == t_s and t_p % npos_p == 0

    row = lambda v: v.reshape(DEPTH, 1, -1)
    ab_re, ab_im, bb_re, bb_im = _ssm_discretise(ssm_a_re, ssm_a_im, ssm_log_step, ssm_b_re, ssm_b_im)
    p = dict(
        w_in_br=w_in[:, :, :BRANCH_COLS].astype(BF16),
        w_in_gate=w_in[:, :, BRANCH_COLS:].astype(BF16),
        b_gate=row(b_gate),
        bb=jnp.concatenate([_block_diag_in(bb_re), _block_diag_in(bb_im)], axis=3).astype(BF16),
        a_re=row(ab_re), a_im=row(ab_im),
        cc=jnp.concatenate([_block_diag_out(ssm_c_re), -_block_diag_out(ssm_c_im)], axis=2).astype(BF16),
        d_skip=row(ssm_d),
        w_glu=ssm_w_glu.astype(BF16), b_glu=row(ssm_b_glu),
        gln_g=row(gmlp_ln_g), gln_b=row(gmlp_ln_b),
        w_s=gmlp_w_s, pool_w=pool_w.astype(BF16), pool_scale=row(pool_scale),
        w_br=jnp.concatenate([w_br_ssm, w_br_gmlp, w_br_pool], axis=1).astype(BF16),
        w_o=w_o.astype(BF16), ln1_g=row(ln1_g), ln1_b=row(ln1_b),
        w_up=ffn_w_up.astype(BF16), conv_w=ffn_conv_w, conv_b=row(ffn_conv_b),
        w_down=ffn_w_down.astype(BF16), ln2_g=row(ln2_g), ln2_b=row(ln2_b),
    )
    bs_p = _gmlp_bias_rows(gmlp_b_s, npos_p, n_p)
    bs_s = _gmlp_bias_rows(gmlp_b_s, npos_s, n_s)
    zeros_p = (jnp.zeros((n_p, SSM_COLS), F32), jnp.zeros((n_p, SSM_COLS), F32),
               jnp.zeros((POOL_PAST * n_p, POOL_WIDTH), F32), jnp.zeros(((FFN_CONV - 1) * n_p, F2), F32))

    outs = [[] for _ in range(9)]
    for l in range(DEPTH):
        hp, a_re_n, a_im_n, _, pool_n, conv_n = _group_layer(
            hp, *zeros_p, p, l, bs_p, nb=n_p, npos=npos_p, start_pos=0, ffn_rows=2 * ROW_CHUNK,
            seq_major_out=(l == DEPTH - 1))
        hs, b_re_n, b_im_n, v_n, pool_m, conv_m = _group_layer(
            hs, state_ssm_re[l].reshape(n_s, SSM_COLS), state_ssm_im[l].reshape(n_s, SSM_COLS),
            _t_major(state_pool[l]), _t_major(state_ffn_conv[l]),
            p, l, bs_s, nb=n_s, npos=npos_s, start_pos=PAST_LEN, ffn_rows=ROW_CHUNK)

        vals = (a_re_n.reshape(n_p, SSM_GROUPS, SSM_STATE), a_im_n.reshape(n_p, SSM_GROUPS, SSM_STATE),
                _seq_major(pool_n, n_p), _seq_major(conv_n, n_p),
                b_re_n.reshape(n_s, SSM_GROUPS, SSM_STATE), b_im_n.reshape(n_s, SSM_GROUPS, SSM_STATE),
                _seq_major(v_n, n_s), _seq_major(pool_m, n_s), _seq_major(conv_m, n_s))
        for o, v in zip(outs, vals):
            o.append(v)

    return (hp, _seq_major(hs, n_s)) + tuple(jnp.stack(o) for o in outs)
```

```python
import functools
import math

import jax
import jax.numpy as jnp
from jax import lax
from jax.experimental import pallas as pl
from jax.experimental.pallas import tpu as pltpu

F32 = jnp.float32
BF16 = jnp.bfloat16

D_MODEL = 1024
DEPTH = 2
PAST_LEN = 16384
SSM_GROUP_CH = 16
SSM_WIDTH = 512
SSM_GROUPS = 32
SSM_STATE = 64
SSM_COLS = SSM_GROUPS * SSM_STATE
GMLP_HEADS = 4
GMLP_CHUNK = 128
GMLP_WIDTH = 256
GMLP_HEAD_DIM = 64
POOL_WINDOWS = (2, 4, 8, 16)
POOL_WIDTH = 512
POOL_GROUP_CH = 128
POOL_PAST = 15
N_BRANCH = 3
BRANCH_COLS = SSM_WIDTH + 2 * GMLP_WIDTH + POOL_WIDTH
D_FF = 2816
F2 = 2 * D_FF
FFN_CONV = 3
LN_EPS = 1e-5
ALPHA = (2 * DEPTH) ** 0.25

SUBLANES = 8
MXU_DIM = 256
VMEM_LIMIT_BYTES = 60 * 1024 * 1024

TILE_ROWS = 1024
ROW_CHUNK = 256
SSM_CHUNK = 512
SCAN_PASSES = 2
SCAN_BLOCKS = 2
SSM_KT = SSM_WIDTH // MXU_DIM
KT_COLS = SSM_COLS // SSM_KT
FF_COL_CHUNK = MXU_DIM
HEADS_PER_PASS = GMLP_HEADS // (SSM_KT * SCAN_PASSES)
assert HEADS_PER_PASS * SSM_KT * SCAN_PASSES == GMLP_HEADS


def _gelu(x):
    c = math.sqrt(2.0 / math.pi)
    return 0.5 * x * (1.0 + jnp.tanh(c * (x + 0.044715 * (x * x * x))))


def _sigmoid(x):
    return 1.0 / (1.0 + jnp.exp(-x))


def _layer_norm(x, g, b):
    mu = jnp.mean(x, axis=-1, keepdims=True)
    xc = x - mu
    var = jnp.mean(xc * xc, axis=-1, keepdims=True)
    return xc * lax.rsqrt(var + LN_EPS) * g + b


def _dot(a, b):
    return jnp.dot(a, b, preferred_element_type=F32)


def _load_rows_t_major(x_ref, r0, n, nb, seq_major):
    if not seq_major:
        return x_ref[pl.ds(r0, n), :]
    p0 = r0 // nb if isinstance(r0, int) else pl.multiple_of(r0 >> int(math.log2(nb)), SUBLANES)
    blk = x_ref[:, pl.ds(p0, n // nb), :]
    return jnp.swapaxes(blk, 0, 1).reshape(n, blk.shape[-1])


def _whole():
    return pl.BlockSpec(memory_space=pltpu.VMEM)


def _layer_const(a, layer):
    nd = a.ndim - 1
    return pl.BlockSpec((None,) + a.shape[1:], lambda i: (layer,) + (0,) * nd,
                        pipeline_mode=pl.Buffered(1))


def _expand_mix(ws_ref, wk, rows, shift, nb):
    cl = GMLP_CHUNK
    e = (lax.broadcasted_iota(jnp.int32, (rows, cl), 0) >> shift
         == lax.broadcasted_iota(jnp.int32, (rows, cl), 1)).astype(BF16)
    tril = (lax.broadcasted_iota(jnp.int32, (cl, cl), 0)
            >= lax.broadcasted_iota(jnp.int32, (cl, cl), 1))
    for h in range(GMLP_HEADS):
        w = jnp.where(tril, ws_ref[h], 0.0).astype(BF16)
        rows_w = _dot(e, w).astype(BF16)
        for c0 in range(0, rows, MXU_DIM):
            et = (lax.broadcasted_iota(jnp.int32, (cl, MXU_DIM), 0)
                  == (lax.broadcasted_iota(jnp.int32, (cl, MXU_DIM), 1) + c0) >> shift).astype(BF16)
            full = _dot(rows_w, et)
            same_seq = ((lax.broadcasted_iota(jnp.int32, (rows, MXU_DIM), 0) & (nb - 1))
                        == ((lax.broadcasted_iota(jnp.int32, (rows, MXU_DIM), 1) + c0) & (nb - 1)))
            wk[h, :, c0:c0 + MXU_DIM] = jnp.where(same_seq, full, 0.0).astype(BF16)


def _branches_kernel(x_ref, h0re_ref, h0im_ref, ppast_ref,
                     w_in_ref, bb_ref, are_ref, aim_ref, cc_ref, dsk_ref,
                     wglu_ref, bglu_ref, lng_ref, lnb_ref, ws_ref, bs_ref,
                     poolw_ref, pscale_ref,
                     ycat_ref, hcre, hcim, vn_out_ref, pool_out_ref,
                     *scratch, nb, npos, n_tiles, start_pos, seq_major_in):
    *hbufs, us, ya, gu, vnb, sacc, xp, wk = scratch
    rows = nb * npos
    n_chunks = rows // ROW_CHUNK
    past_rows = POOL_PAST * nb
    tile = pl.program_id(0)
    shift = int(math.log2(nb))
    slabs_per_block = rows // SUBLANES // SCAN_BLOCKS
    steps = min(npos, slabs_per_block)
    groups_per_block = slabs_per_block // steps
    assert groups_per_block == 1 or steps == npos
    mix_rows = rows // SCAN_BLOCKS

    @pl.when(tile == 0)
    def _init():
        hcre[...] = h0re_ref[...]
        hcim[...] = h0im_ref[...]
        xp[0:past_rows, :] = ppast_ref[...]
        _expand_mix(ws_ref, wk, rows, shift, nb)

    base_pos = start_pos + tile * npos

    def proj_chunk(c, carry):
        r0 = pl.multiple_of(c * ROW_CHUNK, ROW_CHUNK)
        xb = _load_rows_t_major(x_ref, r0, ROW_CHUNK, nb, seq_major_in).astype(BF16)
        pa = _dot(xb, w_in_ref[...])
        us[pl.ds(r0, ROW_CHUNK), :] = pa[:, 0:SSM_WIDTH]
        gu[pl.ds(r0, ROW_CHUNK), :] = _gelu(pa[:, SSM_WIDTH:SSM_WIDTH + GMLP_WIDTH])
        v = _gelu(pa[:, SSM_WIDTH + GMLP_WIDTH:SSM_WIDTH + 2 * GMLP_WIDTH])
        v = _layer_norm(v, lng_ref[...], lnb_ref[...])
        vn_out_ref[pl.ds(r0, ROW_CHUNK), :] = v
        vnb[pl.ds(r0, ROW_CHUNK), :] = v.astype(BF16)
        sacc[pl.ds(r0, ROW_CHUNK), :] = bs_ref[pl.ds(r0, ROW_CHUNK), :]
        xp[pl.ds(past_rows + r0, ROW_CHUNK), :] = pa[:, SSM_WIDTH + 2 * GMLP_WIDTH:BRANCH_COLS]
        return carry

    lax.fori_loop(0, n_chunks, proj_chunk, 0)

    n_hbuf = len(hbufs)
    overlap = n_hbuf > 1

    def bu_rows(kt, r0):
        ub = us[pl.ds(r0, SSM_CHUNK), kt * MXU_DIM:(kt + 1) * MXU_DIM].astype(BF16)
        hbufs[kt % n_hbuf][pl.ds(r0, SSM_CHUNK), :] = _dot(ub, bb_ref[kt])

    def y_rows(kt, r0):
        hb = hbufs[kt % n_hbuf][pl.ds(r0, SSM_CHUNK), :].astype(BF16)
        u = us[pl.ds(r0, SSM_CHUNK), kt * MXU_DIM:(kt + 1) * MXU_DIM]
        y = _dot(hb, cc_ref[kt]) + dsk_ref[:, kt * MXU_DIM:(kt + 1) * MXU_DIM] * u
        ya[pl.ds(r0, SSM_CHUNK), kt * MXU_DIM:(kt + 1) * MXU_DIM] = y

    def chunk_loop(fn, kt):
        def body(c, carry):
            fn(kt, pl.multiple_of(c * SSM_CHUNK, SSM_CHUNK))
            return carry
        lax.fori_loop(0, rows // SSM_CHUNK, body, 0)

    if overlap:
        assert rows // SSM_CHUNK == SCAN_BLOCKS
        chunk_loop(bu_rows, 0)
    for kt in range(SSM_KT):
        c_lo = kt * KT_COLS
        hbuf = hbufs[kt % n_hbuf]
        if not overlap:
            chunk_loop(bu_rows, kt)

        width = KT_COLS // SCAN_PASSES
        for ch in range(SCAN_PASSES):
            lo = ch * width
            if overlap:
                heads_kt = GMLP_HEADS // SSM_KT
                mix_heads = range(kt * heads_kt, (kt + 1) * heads_kt) if ch == SCAN_PASSES - 1 else ()
            else:
                first = (kt * SCAN_PASSES + ch) * HEADS_PER_PASS
                mix_heads = range(first, first + HEADS_PER_PASS)
            a_r = jnp.broadcast_to(are_ref[:, c_lo + lo:c_lo + lo + width], (SUBLANES, width))
            a_i = jnp.broadcast_to(aim_ref[:, c_lo + lo:c_lo + lo + width], (SUBLANES, width))

            def scan_block(j, carry, kt=kt, hbuf=hbuf, ch=ch, lo=lo, a_r=a_r, a_i=a_i, c_lo=c_lo, mix_heads=mix_heads):
                for q in range(groups_per_block):
                    if groups_per_block == 1 and steps < npos:
                        t0, b0 = j * steps, 0
                    else:
                        t0, b0 = 0, pl.multiple_of((j * groups_per_block + q) * SUBLANES, SUBLANES)
                    h_r = hcre[pl.ds(b0, SUBLANES), c_lo + lo:c_lo + lo + width]
                    h_i = hcim[pl.ds(b0, SUBLANES), c_lo + lo:c_lo + lo + width]
                    for t in range(steps):
                        row = pl.multiple_of((t0 + t) * nb + b0, SUBLANES)
                        b_r = hbuf[pl.ds(row, SUBLANES), lo:lo + width]
                        b_i = hbuf[pl.ds(row, SUBLANES), KT_COLS + lo:KT_COLS + lo + width]
                        h_r, h_i = (a_r * h_r - a_i * h_i + b_r,
                                    a_r * h_i + a_i * h_r + b_i)
                        hbuf[pl.ds(row, SUBLANES), lo:lo + width] = h_r
                        hbuf[pl.ds(row, SUBLANES), KT_COLS + lo:KT_COLS + lo + width] = h_i
                    hcre[pl.ds(b0, SUBLANES), c_lo + lo:c_lo + lo + width] = h_r
                    hcim[pl.ds(b0, SUBLANES), c_lo + lo:c_lo + lo + width] = h_i
                if mix_heads:
                    m0 = pl.multiple_of(j * mix_rows, mix_rows)
                    head = lax.broadcasted_iota(jnp.int32, (1, GMLP_WIDTH), 1) // GMLP_HEAD_DIM
                    acc = sacc[pl.ds(m0, mix_rows), :]
                    for mh in mix_heads:
                        mixed = _dot(wk[mh, pl.ds(m0, mix_rows), :], vnb[...])
                        acc = acc + jnp.where(head == mh, mixed, 0.0)
                    sacc[pl.ds(m0, mix_rows), :] = acc
                if overlap and ch == 0:
                    r0 = pl.multiple_of(j * SSM_CHUNK, SSM_CHUNK)
                    if kt + 1 < SSM_KT:
                        bu_rows(kt + 1, r0)
                    if kt > 0:
                        y_rows(kt - 1, r0)
                return carry

            lax.fori_loop(0, SCAN_BLOCKS, scan_block, 0)

        if not overlap:
            chunk_loop(y_rows, kt)
    if overlap:
        chunk_loop(y_rows, SSM_KT - 1)

    def out_chunk(c, carry):
        r0 = pl.multiple_of(c * ROW_CHUNK, ROW_CHUNK)
        z = _gelu(ya[pl.ds(r0, ROW_CHUNK), :])
        o_a = z * _sigmoid(_dot(z.astype(BF16), wglu_ref[...]) + bglu_ref[...])
        ycat_ref[pl.ds(r0, ROW_CHUNK), 0:SSM_WIDTH] = o_a.astype(BF16)
        y_b = gu[pl.ds(r0, ROW_CHUNK), :] * sacc[pl.ds(r0, ROW_CHUNK), :]
        ycat_ref[pl.ds(r0, ROW_CHUNK), SSM_WIDTH:SSM_WIDTH + GMLP_WIDTH] = y_b.astype(BF16)

        ridx = lax.broadcasted_iota(jnp.int32, (ROW_CHUNK, 1), 0) + r0
        pos = base_pos + (ridx >> shift)
        for g, win in enumerate(POOL_WINDOWS):
            cl = g * POOL_GROUP_CH
            cur = xp[pl.ds(past_rows + r0, ROW_CHUNK), cl:cl + POOL_GROUP_CH]
            wsum = cur
            for j in range(1, win):
                off = pl.multiple_of(past_rows - j * nb + r0, SUBLANES)
                wsum = wsum + xp[pl.ds(off, ROW_CHUNK), cl:cl + POOL_GROUP_CH]
            cnt = jnp.minimum(pos + 1, win).astype(F32)
            d = wsum / cnt - cur
            y_c = _dot(d.astype(BF16), poolw_ref[g]) * pscale_ref[:, cl:cl + POOL_GROUP_CH]
            col = SSM_WIDTH + GMLP_WIDTH + cl
            ycat_ref[pl.ds(r0, ROW_CHUNK), col:col + POOL_GROUP_CH] = y_c.astype(BF16)
        return carry

    lax.fori_loop(0, n_chunks, out_chunk, 0)

    pool_out_ref[...] = xp[rows:rows + past_rows, :]
    if n_tiles > 1:
        xp[0:past_rows, :] = xp[rows:rows + past_rows, :]


def _branches_call(x, h0re, h0im, ppast, p, layer, bs, *, nb, npos, start_pos):
    seq_major_in = x.ndim == 3
    n_rows = x.shape[0] * x.shape[1] if seq_major_in else x.shape[0]
    rows = nb * npos
    n_tiles = n_rows // rows
    past_rows = POOL_PAST * nb
    kern = functools.partial(_branches_kernel, nb=nb, npos=npos, n_tiles=n_tiles, start_pos=start_pos,
                             seq_major_in=seq_major_in)
    row_tile = lambda w: pl.BlockSpec((rows, w), lambda i: (i, 0))
    x_spec = pl.BlockSpec((nb, npos, D_MODEL), lambda i: (0, i, 0)) if seq_major_in else row_tile(D_MODEL)
    n_hbuf = SSM_KT if n_tiles > 1 else 1
    fixed = lambda r, w: pl.BlockSpec((r, w), lambda i: (0, 0))
    consts = (p['w_in_br'], p['bb'], p['a_re'], p['a_im'], p['cc'], p['d_skip'],
              p['w_glu'], p['b_glu'], p['gln_g'], p['gln_b'], p['w_s'], bs, p['pool_w'], p['pool_scale'])
    return pl.pallas_call(
        kern,
        grid=(n_tiles,),
        in_specs=[x_spec, _whole(), _whole(), _whole()] + [_layer_const(c, layer) for c in consts],
        out_specs=[row_tile(BRANCH_COLS - GMLP_WIDTH), fixed(nb, SSM_COLS), fixed(nb, SSM_COLS),
                   row_tile(GMLP_WIDTH), fixed(past_rows, POOL_WIDTH)],
        out_shape=[jax.ShapeDtypeStruct((n_rows, BRANCH_COLS - GMLP_WIDTH), BF16),
                   jax.ShapeDtypeStruct((nb, SSM_COLS), F32),
                   jax.ShapeDtypeStruct((nb, SSM_COLS), F32),
                   jax.ShapeDtypeStruct((n_rows, GMLP_WIDTH), F32),
                   jax.ShapeDtypeStruct((past_rows, POOL_WIDTH), F32)],
        scratch_shapes=[pltpu.VMEM((rows, 2 * KT_COLS), F32)] * n_hbuf
                       + [pltpu.VMEM((rows, SSM_WIDTH), F32),
                        pltpu.VMEM((rows, SSM_WIDTH), F32),
                        pltpu.VMEM((rows, GMLP_WIDTH), F32),
                        pltpu.VMEM((rows, GMLP_WIDTH), BF16),
                        pltpu.VMEM((rows, GMLP_WIDTH), F32),
                        pltpu.VMEM((rows + past_rows, POOL_WIDTH), F32),
                        pltpu.VMEM((GMLP_HEADS, rows, rows), BF16)],
        compiler_params=pltpu.CompilerParams(dimension_semantics=("arbitrary",),
                                             vmem_limit_bytes=VMEM_LIMIT_BYTES),
        name="branches",
    )(x, h0re, h0im, ppast, *consts)


def _merge_ffn_kernel(x_ref, ycat_ref, cpast_ref,
                      wg_ref, bg_ref, wbr_ref, wo_ref, g1_ref, b1_ref,
                      wup_ref, cw_ref, cb_ref, wdn_ref, g2_ref, b2_ref,
                      x2_ref, cout_ref, tailbuf, act, *, nb, rows, seq_major_in, seq_major_out):
    tail = (FFN_CONV - 1) * nb
    assert rows >= tail

    @pl.when(pl.program_id(0) == 0)
    def _init():
        tailbuf[...] = cpast_ref[...]

    x = _load_rows_t_major(x_ref, 0, rows, nb, seq_major_in)
    xb = x.astype(BF16)
    yc = ycat_ref[...]
    bounds = (0, SSM_WIDTH, SSM_WIDTH + GMLP_WIDTH, BRANCH_COLS - GMLP_WIDTH)
    merged = None
    for i in range(N_BRANCH):
        gate = _sigmoid(_dot(xb, wg_ref[:, i * D_MODEL:(i + 1) * D_MODEL])
                        + bg_ref[:, i * D_MODEL:(i + 1) * D_MODEL])
        br = _dot(yc[:, bounds[i]:bounds[i + 1]], wbr_ref[bounds[i]:bounds[i + 1], :])
        merged = gate * br if merged is None else merged + gate * br
    mix = _dot(merged.astype(BF16), wo_ref[...])
    x1 = _layer_norm(ALPHA * x + mix, g1_ref[...], b1_ref[...])

    x1b = x1.astype(BF16)
    for cc in range(D_FF // FF_COL_CHUNK):
        conv = []
        for part in range(2):
            lo = part * D_FF + cc * FF_COL_CHUNK
            cols = slice(lo, lo + FF_COL_CHUNK)
            h = _dot(x1b, wup_ref[:, cols])
            he = jnp.concatenate([tailbuf[:, cols], h], axis=0)
            tailbuf[:, cols] = h[rows - tail:rows, :]
            y = cb_ref[:, cols] + he[0:rows, :] * cw_ref[0:1, cols]
            y = y + he[nb:nb + rows, :] * cw_ref[1:2, cols]
            y = y + h * cw_ref[2:3, cols]
            conv.append(y)
        a = _gelu(conv[1]) * conv[0]
        act[:, cc * FF_COL_CHUNK:(cc + 1) * FF_COL_CHUNK] = a.astype(BF16)
    ff = _dot(act[...], wdn_ref[...])
    x2 = _layer_norm(ALPHA * x1 + ff, g2_ref[...], b2_ref[...])
    if seq_major_out:
        x2_ref[...] = jnp.swapaxes(x2.reshape(rows // nb, nb, D_MODEL), 0, 1)
    else:
        x2_ref[...] = x2
    cout_ref[...] = tailbuf[...]


def _merge_ffn_call(x, ycat, cpast, p, layer, *, nb, rows, seq_major_out):
    seq_major_in = x.ndim == 3
    n_rows = ycat.shape[0]
    tail = (FFN_CONV - 1) * nb
    kern = functools.partial(_merge_ffn_kernel, nb=nb, rows=rows,
                             seq_major_in=seq_major_in, seq_major_out=seq_major_out)
    consts = (p['w_in_gate'], p['b_gate'], p['w_br'], p['w_o'], p['ln1_g'], p['ln1_b'],
              p['w_up'], p['conv_w'], p['conv_b'], p['w_down'], p['ln2_g'], p['ln2_b'])
    t_major_spec = pl.BlockSpec((rows, D_MODEL), lambda i: (i, 0))
    seq_major_spec = pl.BlockSpec((nb, rows // nb, D_MODEL), lambda i: (0, i, 0))
    return pl.pallas_call(
        kern,
        grid=(n_rows // rows,),
        in_specs=[seq_major_spec if seq_major_in else t_major_spec,
                  pl.BlockSpec((rows, BRANCH_COLS - GMLP_WIDTH), lambda i: (i, 0)), _whole()]
                 + [_layer_const(c, layer) for c in consts],
        out_specs=[seq_major_spec if seq_major_out else t_major_spec,
                   pl.BlockSpec((tail, F2), lambda i: (0, 0))],
        out_shape=[jax.ShapeDtypeStruct((nb, n_rows // nb, D_MODEL) if seq_major_out else (n_rows, D_MODEL), F32),
                   jax.ShapeDtypeStruct((tail, F2), F32)],
        scratch_shapes=[pltpu.VMEM((tail, F2), F32),
                        pltpu.VMEM((rows, D_FF), BF16)],
        compiler_params=pltpu.CompilerParams(dimension_semantics=("arbitrary",),
                                             vmem_limit_bytes=VMEM_LIMIT_BYTES),
        name="merge_convffn",
    )(x, ycat, cpast, *consts)


def _ssm_discretise(a_re, a_im, log_step, b_re, b_im):
    dt = jnp.exp(log_step)[..., None]
    mag = jnp.exp(a_re * dt)
    ab_re = mag * jnp.cos(a_im * dt)
    ab_im = mag * jnp.sin(a_im * dt)
    den = jnp.square(a_re) + jnp.square(a_im)
    nr = ab_re - 1.0
    k_re = (nr * a_re + ab_im * a_im) / den
    k_im = (ab_im * a_re - nr * a_im) / den
    bb_re = k_re[..., None] * b_re - k_im[..., None] * b_im
    bb_im = k_re[..., None] * b_im + k_im[..., None] * b_re
    return ab_re, ab_im, bb_re, bb_im


def _block_diag_in(m):
    gpt = SSM_GROUPS // SSM_KT
    m = jnp.transpose(m, (0, 1, 3, 2)).reshape(-1, SSM_KT, gpt, SSM_GROUP_CH, SSM_STATE)
    eye = jnp.eye(gpt, dtype=m.dtype)
    return jnp.einsum('lkgcn,gh->lkgchn', m, eye).reshape(-1, SSM_KT, gpt * SSM_GROUP_CH, gpt * SSM_STATE)


def _block_diag_out(m):
    gpt = SSM_GROUPS // SSM_KT
    m = jnp.transpose(m, (0, 1, 3, 2)).reshape(-1, SSM_KT, gpt, SSM_STATE, SSM_GROUP_CH)
    eye = jnp.eye(gpt, dtype=m.dtype)
    return jnp.einsum('lkgnc,gh->lkgnhc', m, eye).reshape(-1, SSM_KT, gpt * SSM_STATE, gpt * SSM_GROUP_CH)


def _gmlp_bias_rows(b_s, cl, nb):
    b = jnp.transpose(b_s[:, :, :cl], (0, 2, 1))
    return jnp.repeat(jnp.repeat(b, GMLP_HEAD_DIM, axis=2), nb, axis=1)


def _t_major(a):
    return jnp.transpose(a, (1, 0, 2)).reshape(a.shape[0] * a.shape[1], a.shape[2])


def _seq_major(a, n_seq):
    return jnp.transpose(a.reshape(a.shape[0] // n_seq, n_seq, a.shape[1]), (1, 0, 2))


def _group_layer(x, st_re, st_im, st_pool, st_conv, p, layer, bs, *, nb, npos, start_pos, ffn_rows,
                 seq_major_out=False):
    ycat, h_re, h_im, vn, pool_new = _branches_call(x, st_re, st_im, st_pool, p, layer, bs,
                                                     nb=nb, npos=npos, start_pos=start_pos)
    x2, conv_new = _merge_ffn_call(x, ycat, st_conv, p, layer, nb=nb, rows=ffn_rows,
                                   seq_major_out=seq_major_out)
    return x2, h_re, h_im, vn, pool_new, conv_new


def kernel(x_prompt, x_sample, state_ssm_re, state_ssm_im, state_pool, state_ffn_conv, w_in, b_gate, ssm_a_re, ssm_a_im, ssm_log_step, ssm_b_re, ssm_b_im, ssm_c_re, ssm_c_im, ssm_d, ssm_w_glu, ssm_b_glu, gmlp_ln_g, gmlp_ln_b, gmlp_w_s, gmlp_b_s, pool_w, pool_scale, w_br_ssm, w_br_gmlp, w_br_pool, w_o, ln1_g, ln1_b, ffn_w_up, ffn_conv_w, ffn_conv_b, ffn_w_down, ln2_g, ln2_b):
    n_p, t_p, _ = x_prompt.shape
    n_s, t_s, _ = x_sample.shape
    hp = x_prompt
    hs = _t_major(x_sample)
    npos_p = TILE_ROWS // n_p
    npos_s = TILE_ROWS // n_s
    assert npos_p == GMLP_CHUNK and npos_s == t_s and t_p % npos_p == 0

    row = lambda v: v.reshape(DEPTH, 1, -1)
    ab_re, ab_im, bb_re, bb_im = _ssm_discretise(ssm_a_re, ssm_a_im, ssm_log_step, ssm_b_re, ssm_b_im)
    p = dict(
        w_in_br=w_in[:, :, :BRANCH_COLS].astype(BF16),
        w_in_gate=w_in[:, :, BRANCH_COLS:].astype(BF16),
        b_gate=row(b_gate),
        bb=jnp.concatenate([_block_diag_in(bb_re), _block_diag_in(bb_im)], axis=3).astype(BF16),
        a_re=row(ab_re), a_im=row(ab_im),
        cc=jnp.concatenate([_block_diag_out(ssm_c_re), -_block_diag_out(ssm_c_im)], axis=2).astype(BF16),
        d_skip=row(ssm_d),
        w_glu=ssm_w_glu.astype(BF16), b_glu=row(ssm_b_glu),
        gln_g=row(gmlp_ln_g), gln_b=row(gmlp_ln_b),
        w_s=gmlp_w_s, pool_w=pool_w.astype(BF16), pool_scale=row(pool_scale),
        w_br=jnp.concatenate([w_br_ssm, w_br_gmlp, w_br_pool], axis=1).astype(BF16),
        w_o=w_o.astype(BF16), ln1_g=row(ln1_g), ln1_b=row(ln1_b),
        w_up=ffn_w_up.astype(BF16), conv_w=ffn_conv_w, conv_b=row(ffn_conv_b),
        w_down=ffn_w_down.astype(BF16), ln2_g=row(ln2_g), ln2_b=row(ln2_b),
    )
    bs_p = _gmlp_bias_rows(gmlp_b_s, npos_p, n_p)
    bs_s = _gmlp_bias_rows(gmlp_b_s, npos_s, n_s)
    zeros_p = (jnp.zeros((n_p, SSM_COLS), F32), jnp.zeros((n_p, SSM_COLS), F32),
               jnp.zeros((POOL_PAST * n_p, POOL_WIDTH), F32), jnp.zeros(((FFN_CONV - 1) * n_p, F2), F32))

    outs = [[] for _ in range(9)]
    for l in range(DEPTH):
        hp, a_re_n, a_im_n, _, pool_n, conv_n = _group_layer(
            hp, *zeros_p, p, l, bs_p, nb=n_p, npos=npos_p, start_pos=0, ffn_rows=2 * ROW_CHUNK,
            seq_major_out=(l == DEPTH - 1))
        hs, b_re_n, b_im_n, v_n, pool_m, conv_m = _group_layer(
            hs, state_ssm_re[l].reshape(n_s, SSM_COLS), state_ssm_im[l].reshape(n_s, SSM_COLS),
            _t_major(state_pool[l]), _t_major(state_ffn_conv[l]),
            p, l, bs_s, nb=n_s, npos=npos_s, start_pos=PAST_LEN, ffn_rows=ROW_CHUNK)

        vals = (a_re_n.reshape(n_p, SSM_GROUPS, SSM_STATE), a_im_n.reshape(n_p, SSM_GROUPS, SSM_STATE),
                _seq_major(pool_n, n_p), _seq_major(conv_n, n_p),
                b_re_n.reshape(n_s, SSM_GROUPS, SSM_STATE), b_im_n.reshape(n_s, SSM_GROUPS, SSM_STATE),
                _seq_major(v_n, n_s), _seq_major(pool_m, n_s), _seq_major(conv_m, n_s))
        for o, v in zip(outs, vals):
            o.append(v)

    return (hp, _seq_major(hs, n_s)) + tuple(jnp.stack(o) for o in outs)
```

```python
import functools
import math

import jax
import jax.numpy as jnp
from jax import lax
from jax.experimental import pallas as pl
from jax.experimental.pallas import tpu as pltpu

F32 = jnp.float32
BF16 = jnp.bfloat16

D_MODEL = 1024
DEPTH = 2
PAST_LEN = 16384
SSM_GROUP_CH = 16
SSM_WIDTH = 512
SSM_GROUPS = 32
SSM_STATE = 64
SSM_COLS = SSM_GROUPS * SSM_STATE
GMLP_HEADS = 4
GMLP_CHUNK = 128
GMLP_WIDTH = 256
GMLP_HEAD_DIM = 64
POOL_WINDOWS = (2, 4, 8, 16)
POOL_WIDTH = 512
POOL_GROUP_CH = 128
POOL_PAST = 15
N_BRANCH = 3
BRANCH_COLS = SSM_WIDTH + 2 * GMLP_WIDTH + POOL_WIDTH
D_FF = 2816
F2 = 2 * D_FF
FFN_CONV = 3
LN_EPS = 1e-5
ALPHA = (2 * DEPTH) ** 0.25

SUBLANES = 8
MXU_DIM = 256
VMEM_LIMIT_BYTES = 60 * 1024 * 1024

TILE_ROWS = 1024
ROW_CHUNK = 256
SSM_CHUNK = 512
SCAN_PASSES = 2
SCAN_BLOCKS = 2
SSM_KT = SSM_WIDTH // MXU_DIM
KT_COLS = SSM_COLS // SSM_KT
FF_COL_CHUNK = MXU_DIM
HEADS_PER_PASS = GMLP_HEADS // (SSM_KT * SCAN_PASSES)
assert HEADS_PER_PASS * SSM_KT * SCAN_PASSES == GMLP_HEADS


def _gelu(x):
    c = math.sqrt(2.0 / math.pi)
    return 0.5 * x * (1.0 + jnp.tanh(c * (x + 0.044715 * (x * x * x))))


def _sigmoid(x):
    return 1.0 / (1.0 + jnp.exp(-x))


def _layer_norm(x, g, b):
    mu = jnp.mean(x, axis=-1, keepdims=True)
    xc = x - mu
    var = jnp.mean(xc * xc, axis=-1, keepdims=True)
    return xc * lax.rsqrt(var + LN_EPS) * g + b


def _dot(a, b):
    return jnp.dot(a, b, preferred_element_type=F32)


def _load_rows_t_major(x_ref, r0, n, nb, seq_major):
    if not seq_major:
        return x_ref[pl.ds(r0, n), :]
    p0 = r0 // nb if isinstance(r0, int) else pl.multiple_of(r0 >> int(math.log2(nb)), SUBLANES)
    blk = x_ref[:, pl.ds(p0, n // nb), :]
    return jnp.swapaxes(blk, 0, 1).reshape(n, blk.shape[-1])


def _whole():
    return pl.BlockSpec(memory_space=pltpu.VMEM)


def _layer_const(a, layer):
    nd = a.ndim - 1
    return pl.BlockSpec((None,) + a.shape[1:], lambda i: (layer,) + (0,) * nd,
                        pipeline_mode=pl.Buffered(1))


def _expand_mix(ws_ref, wk, rows, shift, nb):
    cl = GMLP_CHUNK
    e = (lax.broadcasted_iota(jnp.int32, (rows, cl), 0) >> shift
         == lax.broadcasted_iota(jnp.int32, (rows, cl), 1)).astype(BF16)
    tril = (lax.broadcasted_iota(jnp.int32, (cl, cl), 0)
            >= lax.broadcasted_iota(jnp.int32, (cl, cl), 1))
    for h in range(GMLP_HEADS):
        w = jnp.where(tril, ws_ref[h], 0.0).astype(BF16)
        rows_w = _dot(e, w).astype(BF16)
        for c0 in range(0, rows, MXU_DIM):
            et = (lax.broadcasted_iota(jnp.int32, (cl, MXU_DIM), 0)
                  == (lax.broadcasted_iota(jnp.int32, (cl, MXU_DIM), 1) + c0) >> shift).astype(BF16)
            full = _dot(rows_w, et)
            same_seq = ((lax.broadcasted_iota(jnp.int32, (rows, MXU_DIM), 0) & (nb - 1))
                        == ((lax.broadcasted_iota(jnp.int32, (rows, MXU_DIM), 1) + c0) & (nb - 1)))
            wk[h, :, c0:c0 + MXU_DIM] = jnp.where(same_seq, full, 0.0).astype(BF16)


def _branches_kernel(x_ref, h0re_ref, h0im_ref, ppast_ref,
                     w_in_ref, bb_ref, are_ref, aim_ref, cc_ref, dsk_ref,
                     wglu_ref, bglu_ref, lng_ref, lnb_ref, ws_ref, bs_ref,
                     poolw_ref, pscale_ref,
                     ycat_ref, hcre, hcim, pool_out_ref,
                     *scratch, nb, npos, n_tiles, start_pos, seq_major_in, emit_vn):
    vn_out_ref = scratch[0] if emit_vn else None
    *hbufs, us, ya, gu, vnb, sacc, xp, wk = scratch[1:] if emit_vn else scratch
    rows = nb * npos
    n_chunks = rows // ROW_CHUNK
    past_rows = POOL_PAST * nb
    tile = pl.program_id(0)
    shift = int(math.log2(nb))
    slabs_per_block = rows // SUBLANES // SCAN_BLOCKS
    steps = min(npos, slabs_per_block)
    groups_per_block = slabs_per_block // steps
    assert groups_per_block == 1 or steps == npos
    mix_rows = rows // SCAN_BLOCKS

    @pl.when(tile == 0)
    def _init():
        hcre[...] = h0re_ref[...]
        hcim[...] = h0im_ref[...]
        xp[0:past_rows, :] = ppast_ref[...]
        _expand_mix(ws_ref, wk, rows, shift, nb)

    base_pos = start_pos + tile * npos

    def proj_chunk(c, carry):
        r0 = pl.multiple_of(c * ROW_CHUNK, ROW_CHUNK)
        xb = _load_rows_t_major(x_ref, r0, ROW_CHUNK, nb, seq_major_in).astype(BF16)
        pa = _dot(xb, w_in_ref[...])
        us[pl.ds(r0, ROW_CHUNK), :] = pa[:, 0:SSM_WIDTH]
        gu[pl.ds(r0, ROW_CHUNK), :] = _gelu(pa[:, SSM_WIDTH:SSM_WIDTH + GMLP_WIDTH])
        v = _gelu(pa[:, SSM_WIDTH + GMLP_WIDTH:SSM_WIDTH + 2 * GMLP_WIDTH])
        v = _layer_norm(v, lng_ref[...], lnb_ref[...])
        if emit_vn:
            vn_out_ref[pl.ds(r0, ROW_CHUNK), :] = v
        vnb[pl.ds(r0, ROW_CHUNK), :] = v.astype(BF16)
        sacc[pl.ds(r0, ROW_CHUNK), :] = bs_ref[pl.ds(r0, ROW_CHUNK), :]
        xp[pl.ds(past_rows + r0, ROW_CHUNK), :] = pa[:, SSM_WIDTH + 2 * GMLP_WIDTH:BRANCH_COLS]
        return carry

    lax.fori_loop(0, n_chunks, proj_chunk, 0)

    n_hbuf = len(hbufs)
    overlap = n_hbuf > 1

    def bu_rows(kt, r0):
        ub = us[pl.ds(r0, SSM_CHUNK), kt * MXU_DIM:(kt + 1) * MXU_DIM].astype(BF16)
        hbufs[kt % n_hbuf][pl.ds(r0, SSM_CHUNK), :] = _dot(ub, bb_ref[kt])

    def y_rows(kt, r0):
        hb = hbufs[kt % n_hbuf][pl.ds(r0, SSM_CHUNK), :].astype(BF16)
        u = us[pl.ds(r0, SSM_CHUNK), kt * MXU_DIM:(kt + 1) * MXU_DIM]
        y = _dot(hb, cc_ref[kt]) + dsk_ref[:, kt * MXU_DIM:(kt + 1) * MXU_DIM] * u
        ya[pl.ds(r0, SSM_CHUNK), kt * MXU_DIM:(kt + 1) * MXU_DIM] = y

    def chunk_loop(fn, kt):
        def body(c, carry):
            fn(kt, pl.multiple_of(c * SSM_CHUNK, SSM_CHUNK))
            return carry
        lax.fori_loop(0, rows // SSM_CHUNK, body, 0)

    if overlap:
        assert rows // SSM_CHUNK == SCAN_BLOCKS
        chunk_loop(bu_rows, 0)
    for kt in range(SSM_KT):
        c_lo = kt * KT_COLS
        hbuf = hbufs[kt % n_hbuf]
        if not overlap:
            chunk_loop(bu_rows, kt)

        width = KT_COLS // SCAN_PASSES
        for ch in range(SCAN_PASSES):
            lo = ch * width
            if overlap:
                heads_kt = GMLP_HEADS // SSM_KT
                mix_heads = range(kt * heads_kt, (kt + 1) * heads_kt) if ch == SCAN_PASSES - 1 else ()
            else:
                first = (kt * SCAN_PASSES + ch) * HEADS_PER_PASS
                mix_heads = range(first, first + HEADS_PER_PASS)
            a_r = jnp.broadcast_to(are_ref[:, c_lo + lo:c_lo + lo + width], (SUBLANES, width))
            a_i = jnp.broadcast_to(aim_ref[:, c_lo + lo:c_lo + lo + width], (SUBLANES, width))

            def scan_block(j, carry, kt=kt, hbuf=hbuf, ch=ch, lo=lo, a_r=a_r, a_i=a_i, c_lo=c_lo, mix_heads=mix_heads):
                for q in range(groups_per_block):
                    if groups_per_block == 1 and steps < npos:
                        t0, b0 = j * steps, 0
                    else:
                        t0, b0 = 0, (j * groups_per_block + q) * SUBLANES
                    h_r = hcre[pl.ds(b0, SUBLANES), c_lo + lo:c_lo + lo + width]
                    h_i = hcim[pl.ds(b0, SUBLANES), c_lo + lo:c_lo + lo + width]
                    for t in range(steps):
                        row = (t0 + t) * nb + b0
                        b_r = hbuf[pl.ds(row, SUBLANES), lo:lo + width]
                        b_i = hbuf[pl.ds(row, SUBLANES), KT_COLS + lo:KT_COLS + lo + width]
                        h_r, h_i = (a_r * h_r - a_i * h_i + b_r,
                                    a_r * h_i + a_i * h_r + b_i)
                        hbuf[pl.ds(row, SUBLANES), lo:lo + width] = h_r
                        hbuf[pl.ds(row, SUBLANES), KT_COLS + lo:KT_COLS + lo + width] = h_i
                    hcre[pl.ds(b0, SUBLANES), c_lo + lo:c_lo + lo + width] = h_r
                    hcim[pl.ds(b0, SUBLANES), c_lo + lo:c_lo + lo + width] = h_i
                if mix_heads:
                    m0 = j * mix_rows
                    head = lax.broadcasted_iota(jnp.int32, (1, GMLP_WIDTH), 1) // GMLP_HEAD_DIM
                    acc = sacc[pl.ds(m0, mix_rows), :]
                    for mh in mix_heads:
                        mixed = _dot(wk[mh, pl.ds(m0, mix_rows), :], vnb[...])
                        acc = acc + jnp.where(head == mh, mixed, 0.0)
                    sacc[pl.ds(m0, mix_rows), :] = acc
                if overlap and ch == 0:
                    r0 = j * SSM_CHUNK
                    if kt + 1 < SSM_KT:
                        bu_rows(kt + 1, r0)
                    if kt > 0:
                        y_rows(kt - 1, r0)
                return carry

            for j in range(SCAN_BLOCKS):
                scan_block(j, 0)

        if not overlap:
            chunk_loop(y_rows, kt)
    if overlap:
        chunk_loop(y_rows, SSM_KT - 1)

    def out_chunk(c, carry):
        r0 = pl.multiple_of(c * ROW_CHUNK, ROW_CHUNK)
        z = _gelu(ya[pl.ds(r0, ROW_CHUNK), :])
        o_a = z * _sigmoid(_dot(z.astype(BF16), wglu_ref[...]) + bglu_ref[...])
        ycat_ref[pl.ds(r0, ROW_CHUNK), 0:SSM_WIDTH] = o_a.astype(BF16)
        y_b = gu[pl.ds(r0, ROW_CHUNK), :] * sacc[pl.ds(r0, ROW_CHUNK), :]
        ycat_ref[pl.ds(r0, ROW_CHUNK), SSM_WIDTH:SSM_WIDTH + GMLP_WIDTH] = y_b.astype(BF16)

        ridx = lax.broadcasted_iota(jnp.int32, (ROW_CHUNK, 1), 0) + r0
        pos = base_pos + (ridx >> shift)
        for g, win in enumerate(POOL_WINDOWS):
            cl = g * POOL_GROUP_CH
            reach = (win - 1) * nb
            if reach < ROW_CHUNK:
                off = pl.multiple_of(past_rows - reach + r0, SUBLANES)
                ext = xp[pl.ds(off, ROW_CHUNK + reach), cl:cl + POOL_GROUP_CH]
                cur = ext[reach:, :]
                wsum, span = ext, 1
                while span < win:
                    n = wsum.shape[0] - span * nb
                    wsum = wsum[span * nb:, :] + wsum[:n, :]
                    span *= 2
            else:
                cur = xp[pl.ds(past_rows + r0, ROW_CHUNK), cl:cl + POOL_GROUP_CH]
                wsum = cur
                for j in range(1, win):
                    off = pl.multiple_of(past_rows - j * nb + r0, SUBLANES)
                    wsum = wsum + xp[pl.ds(off, ROW_CHUNK), cl:cl + POOL_GROUP_CH]
            cnt = jnp.minimum(pos + 1, win).astype(F32)
            d = wsum / cnt - cur
            y_c = _dot(d.astype(BF16), poolw_ref[g]) * pscale_ref[:, cl:cl + POOL_GROUP_CH]
            col = SSM_WIDTH + GMLP_WIDTH + cl
            ycat_ref[pl.ds(r0, ROW_CHUNK), col:col + POOL_GROUP_CH] = y_c.astype(BF16)
        return carry

    lax.fori_loop(0, n_chunks, out_chunk, 0)

    pool_out_ref[...] = xp[rows:rows + past_rows, :]
    if n_tiles > 1:
        xp[0:past_rows, :] = xp[rows:rows + past_rows, :]


def _branches_call(x, h0re, h0im, ppast, p, layer, bs, *, nb, npos, start_pos, emit_vn):
    seq_major_in = x.ndim == 3
    n_rows = x.shape[0] * x.shape[1] if seq_major_in else x.shape[0]
    rows = nb * npos
    n_tiles = n_rows // rows
    past_rows = POOL_PAST * nb
    kern = functools.partial(_branches_kernel, nb=nb, npos=npos, n_tiles=n_tiles, start_pos=start_pos,
                             seq_major_in=seq_major_in, emit_vn=emit_vn)
    row_tile = lambda w: pl.BlockSpec((rows, w), lambda i: (i, 0))
    x_spec = pl.BlockSpec((nb, npos, D_MODEL), lambda i: (0, i, 0)) if seq_major_in else row_tile(D_MODEL)
    n_hbuf = SSM_KT if n_tiles > 1 else 1
    fixed = lambda r, w: pl.BlockSpec((r, w), lambda i: (0, 0))
    consts = (p['w_in_br'], p['bb'], p['a_re'], p['a_im'], p['cc'], p['d_skip'],
              p['w_glu'], p['b_glu'], p['gln_g'], p['gln_b'], p['w_s'], bs, p['pool_w'], p['pool_scale'])
    return pl.pallas_call(
        kern,
        grid=(n_tiles,),
        in_specs=[x_spec, _whole(), _whole(), _whole()] + [_layer_const(c, layer) for c in consts],
        out_specs=[row_tile(BRANCH_COLS - GMLP_WIDTH), fixed(nb, SSM_COLS), fixed(nb, SSM_COLS),
                   fixed(past_rows, POOL_WIDTH)] + ([row_tile(GMLP_WIDTH)] if emit_vn else []),
        out_shape=[jax.ShapeDtypeStruct((n_rows, BRANCH_COLS - GMLP_WIDTH), BF16),
                   jax.ShapeDtypeStruct((nb, SSM_COLS), F32),
                   jax.ShapeDtypeStruct((nb, SSM_COLS), F32),
                   jax.ShapeDtypeStruct((past_rows, POOL_WIDTH), F32)]
                  + ([jax.ShapeDtypeStruct((n_rows, GMLP_WIDTH), F32)] if emit_vn else []),
        scratch_shapes=[pltpu.VMEM((rows, 2 * KT_COLS), F32)] * n_hbuf
                       + [pltpu.VMEM((rows, SSM_WIDTH), F32),
                        pltpu.VMEM((rows, SSM_WIDTH), F32),
                        pltpu.VMEM((rows, GMLP_WIDTH), F32),
                        pltpu.VMEM((rows, GMLP_WIDTH), BF16),
                        pltpu.VMEM((rows, GMLP_WIDTH), F32),
                        pltpu.VMEM((rows + past_rows, POOL_WIDTH), F32),
                        pltpu.VMEM((GMLP_HEADS, rows, rows), BF16)],
        compiler_params=pltpu.CompilerParams(dimension_semantics=("arbitrary",),
                                             vmem_limit_bytes=VMEM_LIMIT_BYTES),
        name="branches",
    )(x, h0re, h0im, ppast, *consts)


def _merge_ffn_kernel(x_ref, ycat_ref, cpast_ref,
                      wg_ref, bg_ref, wbr_ref, wo_ref, g1_ref, b1_ref,
                      wup_ref, cw_ref, cb_ref, wdn_ref, g2_ref, b2_ref,
                      x2_ref, cout_ref, tailbuf, act, *, nb, rows, seq_major_in, seq_major_out):
    tail = (FFN_CONV - 1) * nb
    assert rows >= tail

    @pl.when(pl.program_id(0) == 0)
    def _init():
        tailbuf[...] = cpast_ref[...]

    x = _load_rows_t_major(x_ref, 0, rows, nb, seq_major_in)
    xb = x.astype(BF16)
    yc = ycat_ref[...]
    bounds = (0, SSM_WIDTH, SSM_WIDTH + GMLP_WIDTH, BRANCH_COLS - GMLP_WIDTH)
    merged = None
    for i in range(N_BRANCH):
        gate = _sigmoid(_dot(xb, wg_ref[:, i * D_MODEL:(i + 1) * D_MODEL])
                        + bg_ref[:, i * D_MODEL:(i + 1) * D_MODEL])
        br = _dot(yc[:, bounds[i]:bounds[i + 1]], wbr_ref[bounds[i]:bounds[i + 1], :])
        merged = gate * br if merged is None else merged + gate * br
    mix = _dot(merged.astype(BF16), wo_ref[...])
    x1 = _layer_norm(ALPHA * x + mix, g1_ref[...], b1_ref[...])

    x1b = x1.astype(BF16)
    for cc in range(D_FF // FF_COL_CHUNK):
        conv = []
        for part in range(2):
            lo = part * D_FF + cc * FF_COL_CHUNK
            cols = slice(lo, lo + FF_COL_CHUNK)
            h = _dot(x1b, wup_ref[:, cols])
            he = jnp.concatenate([tailbuf[:, cols], h], axis=0)
            tailbuf[:, cols] = h[rows - tail:rows, :]
            y = cb_ref[:, cols] + he[0:rows, :] * cw_ref[0:1, cols]
            y = y + he[nb:nb + rows, :] * cw_ref[1:2, cols]
            y = y + h * cw_ref[2:3, cols]
            conv.append(y)
        a = _gelu(conv[1]) * conv[0]
        act[:, cc * FF_COL_CHUNK:(cc + 1) * FF_COL_CHUNK] = a.astype(BF16)
    ff = _dot(act[...], wdn_ref[...])
    x2 = _layer_norm(ALPHA * x1 + ff, g2_ref[...], b2_ref[...])
    if seq_major_out:
        x2_ref[...] = jnp.swapaxes(x2.reshape(rows // nb, nb, D_MODEL), 0, 1)
    else:
        x2_ref[...] = x2
    cout_ref[...] = tailbuf[...]


def _merge_ffn_call(x, ycat, cpast, p, layer, *, nb, rows, seq_major_out):
    seq_major_in = x.ndim == 3
    n_rows = ycat.shape[0]
    tail = (FFN_CONV - 1) * nb
    kern = functools.partial(_merge_ffn_kernel, nb=nb, rows=rows,
                             seq_major_in=seq_major_in, seq_major_out=seq_major_out)
    consts = (p['w_in_gate'], p['b_gate'], p['w_br'], p['w_o'], p['ln1_g'], p['ln1_b'],
              p['w_up'], p['conv_w'], p['conv_b'], p['w_down'], p['ln2_g'], p['ln2_b'])
    t_major_spec = pl.BlockSpec((rows, D_MODEL), lambda i: (i, 0))
    seq_major_spec = pl.BlockSpec((nb, rows // nb, D_MODEL), lambda i: (0, i, 0))
    return pl.pallas_call(
        kern,
        grid=(n_rows // rows,),
        in_specs=[seq_major_spec if seq_major_in else t_major_spec,
                  pl.BlockSpec((rows, BRANCH_COLS - GMLP_WIDTH), lambda i: (i, 0)), _whole()]
                 + [_layer_const(c, layer) for c in consts],
        out_specs=[seq_major_spec if seq_major_out else t_major_spec,
                   pl.BlockSpec((tail, F2), lambda i: (0, 0))],
        out_shape=[jax.ShapeDtypeStruct((nb, n_rows // nb, D_MODEL) if seq_major_out else (n_rows, D_MODEL), F32),
                   jax.ShapeDtypeStruct((tail, F2), F32)],
        scratch_shapes=[pltpu.VMEM((tail, F2), F32),
                        pltpu.VMEM((rows, D_FF), BF16)],
        compiler_params=pltpu.CompilerParams(dimension_semantics=("arbitrary",),
                                             vmem_limit_bytes=VMEM_LIMIT_BYTES),
        name="merge_convffn",
    )(x, ycat, cpast, *consts)


def _ssm_discretise(a_re, a_im, log_step, b_re, b_im):
    dt = jnp.exp(log_step)[..., None]
    mag = jnp.exp(a_re * dt)
    ab_re = mag * jnp.cos(a_im * dt)
    ab_im = mag * jnp.sin(a_im * dt)
    den = jnp.square(a_re) + jnp.square(a_im)
    nr = ab_re - 1.0
    k_re = (nr * a_re + ab_im * a_im) / den
    k_im = (ab_im * a_re - nr * a_im) / den
    bb_re = k_re[..., None] * b_re - k_im[..., None] * b_im
    bb_im = k_re[..., None] * b_im + k_im[..., None] * b_re
    return ab_re, ab_im, bb_re, bb_im


def _block_diag_in(m):
    gpt = SSM_GROUPS // SSM_KT
    m = jnp.transpose(m, (0, 1, 3, 2)).reshape(-1, SSM_KT, gpt, SSM_GROUP_CH, SSM_STATE)
    eye = jnp.eye(gpt, dtype=m.dtype)
    return jnp.einsum('lkgcn,gh->lkgchn', m, eye).reshape(-1, SSM_KT, gpt * SSM_GROUP_CH, gpt * SSM_STATE)


def _block_diag_out(m):
    gpt = SSM_GROUPS // SSM_KT
    m = jnp.transpose(m, (0, 1, 3, 2)).reshape(-1, SSM_KT, gpt, SSM_STATE, SSM_GROUP_CH)
    eye = jnp.eye(gpt, dtype=m.dtype)
    return jnp.einsum('lkgnc,gh->lkgnhc', m, eye).reshape(-1, SSM_KT, gpt * SSM_STATE, gpt * SSM_GROUP_CH)


def _gmlp_bias_rows(b_s, cl, nb):
    b = jnp.transpose(b_s[:, :, :cl], (0, 2, 1))
    return jnp.repeat(jnp.repeat(b, GMLP_HEAD_DIM, axis=2), nb, axis=1)


def _t_major(a):
    return jnp.transpose(a, (1, 0, 2)).reshape(a.shape[0] * a.shape[1], a.shape[2])


def _seq_major(a, n_seq):
    return jnp.transpose(a.reshape(a.shape[0] // n_seq, n_seq, a.shape[1]), (1, 0, 2))


def _group_layer(x, st_re, st_im, st_pool, st_conv, p, layer, bs, *, nb, npos, start_pos, ffn_rows,
                 seq_major_out=False, emit_vn=False):
    ycat, h_re, h_im, pool_new, *vn = _branches_call(x, st_re, st_im, st_pool, p, layer, bs,
                                                      nb=nb, npos=npos, start_pos=start_pos, emit_vn=emit_vn)
    x2, conv_new = _merge_ffn_call(x, ycat, st_conv, p, layer, nb=nb, rows=ffn_rows,
                                   seq_major_out=seq_major_out)
    return (x2, h_re, h_im, pool_new, conv_new, *vn)


def kernel(x_prompt, x_sample, state_ssm_re, state_ssm_im, state_pool, state_ffn_conv, w_in, b_gate, ssm_a_re, ssm_a_im, ssm_log_step, ssm_b_re, ssm_b_im, ssm_c_re, ssm_c_im, ssm_d, ssm_w_glu, ssm_b_glu, gmlp_ln_g, gmlp_ln_b, gmlp_w_s, gmlp_b_s, pool_w, pool_scale, w_br_ssm, w_br_gmlp, w_br_pool, w_o, ln1_g, ln1_b, ffn_w_up, ffn_conv_w, ffn_conv_b, ffn_w_down, ln2_g, ln2_b):
    n_p, t_p, _ = x_prompt.shape
    n_s, t_s, _ = x_sample.shape
    hp = x_prompt
    hs = _t_major(x_sample)
    npos_p = TILE_ROWS // n_p
    npos_s = TILE_ROWS // n_s
    assert npos_p == GMLP_CHUNK and npos_s == t_s and t_p % npos_p == 0

    row = lambda v: v.reshape(DEPTH, 1, -1)
    ab_re, ab_im, bb_re, bb_im = _ssm_discretise(ssm_a_re, ssm_a_im, ssm_log_step, ssm_b_re, ssm_b_im)
    p = dict(
        w_in_br=w_in[:, :, :BRANCH_COLS].astype(BF16),
        w_in_gate=w_in[:, :, BRANCH_COLS:].astype(BF16),
        b_gate=row(b_gate),
        bb=jnp.concatenate([_block_diag_in(bb_re), _block_diag_in(bb_im)], axis=3).astype(BF16),
        a_re=row(ab_re), a_im=row(ab_im),
        cc=jnp.concatenate([_block_diag_out(ssm_c_re), -_block_diag_out(ssm_c_im)], axis=2).astype(BF16),
        d_skip=row(ssm_d),
        w_glu=ssm_w_glu.astype(BF16), b_glu=row(ssm_b_glu),
        gln_g=row(gmlp_ln_g), gln_b=row(gmlp_ln_b),
        w_s=gmlp_w_s, pool_w=pool_w.astype(BF16), pool_scale=row(pool_scale),
        w_br=jnp.concatenate([w_br_ssm, w_br_gmlp, w_br_pool], axis=1).astype(BF16),
        w_o=w_o.astype(BF16), ln1_g=row(ln1_g), ln1_b=row(ln1_b),
        w_up=ffn_w_up.astype(BF16), conv_w=ffn_conv_w, conv_b=row(ffn_conv_b),
        w_down=ffn_w_down.astype(BF16), ln2_g=row(ln2_g), ln2_b=row(ln2_b),
    )
    bs_p = _gmlp_bias_rows(gmlp_b_s, npos_p, n_p)
    bs_s = _gmlp_bias_rows(gmlp_b_s, npos_s, n_s)
    zeros_p = (jnp.zeros((n_p, SSM_COLS), F32), jnp.zeros((n_p, SSM_COLS), F32),
               jnp.zeros((POOL_PAST * n_p, POOL_WIDTH), F32), jnp.zeros(((FFN_CONV - 1) * n_p, F2), F32))

    outs = [[] for _ in range(9)]
    for l in range(DEPTH):
        hp, a_re_n, a_im_n, pool_n, conv_n = _group_layer(
            hp, *zeros_p, p, l, bs_p, nb=n_p, npos=npos_p, start_pos=0, ffn_rows=2 * ROW_CHUNK,
            seq_major_out=(l == DEPTH - 1))
        hs, b_re_n, b_im_n, pool_m, conv_m, v_n = _group_layer(
            hs, state_ssm_re[l].reshape(n_s, SSM_COLS), state_ssm_im[l].reshape(n_s, SSM_COLS),
            _t_major(state_pool[l]), _t_major(state_ffn_conv[l]),
            p, l, bs_s, nb=n_s, npos=npos_s, start_pos=PAST_LEN, ffn_rows=ROW_CHUNK, emit_vn=True)

        vals = (a_re_n.reshape(n_p, SSM_GROUPS, SSM_STATE), a_im_n.reshape(n_p, SSM_GROUPS, SSM_STATE),
                _seq_major(pool_n, n_p), _seq_major(conv_n, n_p),
                b_re_n.reshape(n_s, SSM_GROUPS, SSM_STATE), b_im_n.reshape(n_s, SSM_GROUPS, SSM_STATE),
                _seq_major(v_n, n_s), _seq_major(pool_m, n_s), _seq_major(conv_m, n_s))
        for o, v in zip(outs, vals):
            o.append(v)

    return (hp, _seq_major(hs, n_s)) + tuple(jnp.stack(o) for o in outs)
```

```python
import functools
import math

import jax
import jax.numpy as jnp
from jax import lax
from jax.experimental import pallas as pl
from jax.experimental.pallas import tpu as pltpu

F32 = jnp.float32
BF16 = jnp.bfloat16

D_MODEL = 1024
DEPTH = 2
PAST_LEN = 16384
SSM_GROUP_CH = 16
SSM_WIDTH = 512
SSM_GROUPS = 32
SSM_STATE = 64
SSM_COLS = SSM_GROUPS * SSM_STATE
GMLP_HEADS = 4
GMLP_CHUNK = 128
GMLP_WIDTH = 256
GMLP_HEAD_DIM = 64
POOL_WINDOWS = (2, 4, 8, 16)
POOL_WIDTH = 512
POOL_GROUP_CH = 128
POOL_PAST = 15
N_BRANCH = 3
BRANCH_COLS = SSM_WIDTH + 2 * GMLP_WIDTH + POOL_WIDTH
D_FF = 2816
F2 = 2 * D_FF
FFN_CONV = 3
LN_EPS = 1e-5
ALPHA = (2 * DEPTH) ** 0.25

SUBLANES = 8
MXU_DIM = 256
VMEM_LIMIT_BYTES = 60 * 1024 * 1024

TILE_ROWS = 1024
ROW_CHUNK = 256
PROJ_CHUNK = 512
SSM_CHUNK = 512
SCAN_PASSES = 2
SCAN_BLOCKS = 2
SSM_KT = SSM_WIDTH // MXU_DIM
KT_COLS = SSM_COLS // SSM_KT
FF_COL_CHUNK = MXU_DIM
HEADS_PER_PASS = GMLP_HEADS // (SSM_KT * SCAN_PASSES)
assert HEADS_PER_PASS * SSM_KT * SCAN_PASSES == GMLP_HEADS


def _gelu(x):
    c = math.sqrt(2.0 / math.pi)
    return 0.5 * x * (1.0 + jnp.tanh(c * (x + 0.044715 * (x * x * x))))


def _sigmoid(x):
    return 1.0 / (1.0 + jnp.exp(-x))


def _layer_norm(x, g, b):
    mu = jnp.mean(x, axis=-1, keepdims=True)
    xc = x - mu
    var = jnp.mean(xc * xc, axis=-1, keepdims=True)
    return xc * lax.rsqrt(var + LN_EPS) * g + b


def _dot(a, b):
    return jnp.dot(a, b, preferred_element_type=F32)


ROWS = "rows"
SEQ = "seq"
LANES = "lanes"


def _load_rows_t_major(x_ref, r0, n, nb, layout):
    if layout == ROWS:
        return x_ref[pl.ds(r0, n), :]
    if layout == LANES:
        p0 = r0 // nb
        return jnp.concatenate([x_ref[:, (p0 + k) * D_MODEL:(p0 + k + 1) * D_MODEL] for k in range(n // nb)],
                               axis=0)
    p0 = r0 // nb if isinstance(r0, int) else pl.multiple_of(r0 >> int(math.log2(nb)), SUBLANES)
    blk = x_ref[:, pl.ds(p0, n // nb), :]
    return jnp.swapaxes(blk, 0, 1).reshape(n, blk.shape[-1])


def _store_rows_t_major(o_ref, val, nb, layout):
    n = val.shape[0]
    if layout == ROWS:
        o_ref[...] = val
    elif layout == LANES:
        _pack_positions(o_ref, val, nb, n // nb, D_MODEL, 0)
    else:
        o_ref[...] = jnp.swapaxes(val.reshape(n // nb, nb, D_MODEL), 0, 1)


def _unpack_positions(dst, src_ref, nb, n_pos, width):
    for k in range(n_pos):
        dst[k * nb:(k + 1) * nb, :] = src_ref[:, k * width:(k + 1) * width]


def _pack_positions(dst_ref, src, nb, n_pos, width, row0):
    for k in range(n_pos):
        dst_ref[:, k * width:(k + 1) * width] = src[row0 + k * nb:row0 + (k + 1) * nb, :]


def _aligned(row):
    return row if isinstance(row, int) else pl.multiple_of(row, SUBLANES)


def _whole():
    return pl.BlockSpec(memory_space=pltpu.VMEM)


def _layer_const(a, layer):
    nd = a.ndim - 1
    return pl.BlockSpec((None,) + a.shape[1:], lambda i: (layer,) + (0,) * nd,
                        pipeline_mode=pl.Buffered(1))


def _expand_mix(ws_ref, wk, rows, shift, nb):
    cl = GMLP_CHUNK
    e = (lax.broadcasted_iota(jnp.int32, (rows, cl), 0) >> shift
         == lax.broadcasted_iota(jnp.int32, (rows, cl), 1)).astype(BF16)
    tril = (lax.broadcasted_iota(jnp.int32, (cl, cl), 0)
            >= lax.broadcasted_iota(jnp.int32, (cl, cl), 1))
    for h in range(GMLP_HEADS):
        w = jnp.where(tril, ws_ref[h], 0.0).astype(BF16)
        rows_w = _dot(e, w).astype(BF16)
        for c0 in range(0, rows, MXU_DIM):
            et = (lax.broadcasted_iota(jnp.int32, (cl, MXU_DIM), 0)
                  == (lax.broadcasted_iota(jnp.int32, (cl, MXU_DIM), 1) + c0) >> shift).astype(BF16)
            full = _dot(rows_w, et)
            same_seq = ((lax.broadcasted_iota(jnp.int32, (rows, MXU_DIM), 0) & (nb - 1))
                        == ((lax.broadcasted_iota(jnp.int32, (rows, MXU_DIM), 1) + c0) & (nb - 1)))
            wk[h, :, c0:c0 + MXU_DIM] = jnp.where(same_seq, full, 0.0).astype(BF16)


def _branches_kernel(x_ref, h0re_ref, h0im_ref, ppast_ref,
                     w_in_ref, bb_ref, are_ref, aim_ref, cc_ref, dsk_ref,
                     wglu_ref, bglu_ref, lng_ref, lnb_ref, ws_ref, bs_ref,
                     poolw_ref, pscale_ref,
                     ycat_ref, hcre, hcim, pool_out_ref,
                     *scratch, nb, npos, n_tiles, start_pos, x_layout, emit_vn):
    vn_out_ref = scratch[0] if emit_vn else None
    *hbufs, us, ya, gu, vnb, sacc, xp, wk = scratch[1:] if emit_vn else scratch
    rows = nb * npos
    n_chunks = rows // ROW_CHUNK
    past_rows = POOL_PAST * nb
    tile = pl.program_id(0)
    shift = int(math.log2(nb))
    slabs_per_block = rows // SUBLANES // SCAN_BLOCKS
    steps = min(npos, slabs_per_block)
    groups_per_block = slabs_per_block // steps
    assert groups_per_block == 1 or steps == npos
    mix_rows = rows // SCAN_BLOCKS

    @pl.when(tile == 0)
    def _init():
        hcre[...] = h0re_ref[...]
        hcim[...] = h0im_ref[...]
        _unpack_positions(xp, ppast_ref, nb, POOL_PAST, POOL_WIDTH)
        _expand_mix(ws_ref, wk, rows, shift, nb)

    base_pos = start_pos + tile * npos

    def proj_chunk(r0):
        xb = _load_rows_t_major(x_ref, r0, PROJ_CHUNK, nb, x_layout).astype(BF16)
        pa = _dot(xb, w_in_ref[...])
        us[pl.ds(r0, PROJ_CHUNK), :] = pa[:, 0:SSM_WIDTH]
        gu[pl.ds(r0, PROJ_CHUNK), :] = _gelu(pa[:, SSM_WIDTH:SSM_WIDTH + GMLP_WIDTH])
        v = _gelu(pa[:, SSM_WIDTH + GMLP_WIDTH:SSM_WIDTH + 2 * GMLP_WIDTH])
        v = _layer_norm(v, lng_ref[...], lnb_ref[...])
        if emit_vn:
            for k in range(PROJ_CHUNK // nb):
                t = r0 // nb + k
                vn_out_ref[:, t * GMLP_WIDTH:(t + 1) * GMLP_WIDTH] = v[k * nb:(k + 1) * nb, :]
        vnb[pl.ds(r0, PROJ_CHUNK), :] = v.astype(BF16)
        sacc[pl.ds(r0, PROJ_CHUNK), :] = bs_ref[pl.ds(r0, PROJ_CHUNK), :]
        xp[pl.ds(past_rows + r0, PROJ_CHUNK), :] = pa[:, SSM_WIDTH + 2 * GMLP_WIDTH:BRANCH_COLS]

    for c in range(rows // PROJ_CHUNK):
        proj_chunk(c * PROJ_CHUNK)

    def pool_rows(r0):
        ridx = lax.broadcasted_iota(jnp.int32, (ROW_CHUNK, 1), 0) + r0
        pos = base_pos + (ridx >> shift)
        for g, win in enumerate(POOL_WINDOWS):
            cl = g * POOL_GROUP_CH
            reach = (win - 1) * nb
            if reach < ROW_CHUNK:
                off = _aligned(past_rows - reach + r0)
                ext = xp[pl.ds(off, ROW_CHUNK + reach), cl:cl + POOL_GROUP_CH]
                cur = ext[reach:, :]
                wsum, span = ext, 1
                while span < win:
                    n = wsum.shape[0] - span * nb
                    wsum = wsum[span * nb:, :] + wsum[:n, :]
                    span *= 2
            else:
                cur = xp[pl.ds(past_rows + r0, ROW_CHUNK), cl:cl + POOL_GROUP_CH]
                wsum = cur
                for j in range(1, win):
                    off = _aligned(past_rows - j * nb + r0)
                    wsum = wsum + xp[pl.ds(off, ROW_CHUNK), cl:cl + POOL_GROUP_CH]
            cnt = jnp.minimum(pos + 1, win).astype(F32)
            d = wsum / cnt - cur
            y_c = _dot(d.astype(BF16), poolw_ref[g]) * pscale_ref[:, cl:cl + POOL_GROUP_CH]
            col = SSM_WIDTH + GMLP_WIDTH + cl
            ycat_ref[pl.ds(r0, ROW_CHUNK), col:col + POOL_GROUP_CH] = y_c.astype(BF16)

    n_hbuf = len(hbufs)
    overlap = n_hbuf > 1

    def bu_rows(kt, r0):
        ub = us[pl.ds(r0, SSM_CHUNK), kt * MXU_DIM:(kt + 1) * MXU_DIM].astype(BF16)
        hbufs[kt % n_hbuf][pl.ds(r0, SSM_CHUNK), :] = _dot(ub, bb_ref[kt])

    def y_rows(kt, r0):
        hb = hbufs[kt % n_hbuf][pl.ds(r0, SSM_CHUNK), :].astype(BF16)
        u = us[pl.ds(r0, SSM_CHUNK), kt * MXU_DIM:(kt + 1) * MXU_DIM]
        y = _dot(hb, cc_ref[kt]) + dsk_ref[:, kt * MXU_DIM:(kt + 1) * MXU_DIM] * u
        ya[pl.ds(r0, SSM_CHUNK), kt * MXU_DIM:(kt + 1) * MXU_DIM] = y

    def chunk_loop(fn, kt):
        def body(c, carry):
            fn(kt, pl.multiple_of(c * SSM_CHUNK, SSM_CHUNK))
            return carry
        lax.fori_loop(0, rows // SSM_CHUNK, body, 0)

    if overlap:
        assert rows // SSM_CHUNK == SCAN_BLOCKS
        chunk_loop(bu_rows, 0)
    for kt in range(SSM_KT):
        c_lo = kt * KT_COLS
        hbuf = hbufs[kt % n_hbuf]
        if not overlap:
            chunk_loop(bu_rows, kt)

        width = KT_COLS // SCAN_PASSES
        for ch in range(SCAN_PASSES):
            lo = ch * width
            if overlap:
                heads_kt = GMLP_HEADS // SSM_KT
                mix_heads = range(kt * heads_kt, (kt + 1) * heads_kt) if ch == SCAN_PASSES - 1 else ()
            else:
                first = (kt * SCAN_PASSES + ch) * HEADS_PER_PASS
                mix_heads = range(first, first + HEADS_PER_PASS)
            a_r = jnp.broadcast_to(are_ref[:, c_lo + lo:c_lo + lo + width], (SUBLANES, width))
            a_i = jnp.broadcast_to(aim_ref[:, c_lo + lo:c_lo + lo + width], (SUBLANES, width))

            def scan_block(j, carry, kt=kt, hbuf=hbuf, ch=ch, lo=lo, a_r=a_r, a_i=a_i, c_lo=c_lo, mix_heads=mix_heads):
                for q in range(groups_per_block):
                    if groups_per_block == 1 and steps < npos:
                        t0, b0 = j * steps, 0
                    else:
                        t0, b0 = 0, (j * groups_per_block + q) * SUBLANES
                    h_r = hcre[pl.ds(b0, SUBLANES), c_lo + lo:c_lo + lo + width]
                    h_i = hcim[pl.ds(b0, SUBLANES), c_lo + lo:c_lo + lo + width]
                    for t in range(steps):
                        row = (t0 + t) * nb + b0
                        b_r = hbuf[pl.ds(row, SUBLANES), lo:lo + width]
                        b_i = hbuf[pl.ds(row, SUBLANES), KT_COLS + lo:KT_COLS + lo + width]
                        h_r, h_i = (a_r * h_r - a_i * h_i + b_r,
                                    a_r * h_i + a_i * h_r + b_i)
                        hbuf[pl.ds(row, SUBLANES), lo:lo + width] = h_r
                        hbuf[pl.ds(row, SUBLANES), KT_COLS + lo:KT_COLS + lo + width] = h_i
                    hcre[pl.ds(b0, SUBLANES), c_lo + lo:c_lo + lo + width] = h_r
                    hcim[pl.ds(b0, SUBLANES), c_lo + lo:c_lo + lo + width] = h_i
                if mix_heads:
                    m0 = j * mix_rows
                    head = lax.broadcasted_iota(jnp.int32, (1, GMLP_WIDTH), 1) // GMLP_HEAD_DIM
                    acc = sacc[pl.ds(m0, mix_rows), :]
                    for mh in mix_heads:
                        mixed = _dot(wk[mh, pl.ds(m0, mix_rows), :], vnb[...])
                        acc = acc + jnp.where(head == mh, mixed, 0.0)
                    sacc[pl.ds(m0, mix_rows), :] = acc
                if overlap and ch == 0:
                    r0 = j * SSM_CHUNK
                    if kt + 1 < SSM_KT:
                        bu_rows(kt + 1, r0)
                    if kt > 0:
                        y_rows(kt - 1, r0)
                return carry

            for j in range(SCAN_BLOCKS):
                scan_block(j, 0)

        if not overlap:
            chunk_loop(y_rows, kt)
    if overlap:
        chunk_loop(y_rows, SSM_KT - 1)

    def out_chunk(c, carry):
        r0 = pl.multiple_of(c * ROW_CHUNK, ROW_CHUNK)
        z = _gelu(ya[pl.ds(r0, ROW_CHUNK), :])
        o_a = z * _sigmoid(_dot(z.astype(BF16), wglu_ref[...]) + bglu_ref[...])
        ycat_ref[pl.ds(r0, ROW_CHUNK), 0:SSM_WIDTH] = o_a.astype(BF16)
        y_b = gu[pl.ds(r0, ROW_CHUNK), :] * sacc[pl.ds(r0, ROW_CHUNK), :]
        ycat_ref[pl.ds(r0, ROW_CHUNK), SSM_WIDTH:SSM_WIDTH + GMLP_WIDTH] = y_b.astype(BF16)
        pool_rows(r0)
        return carry

    lax.fori_loop(0, n_chunks, out_chunk, 0)

    _pack_positions(pool_out_ref, xp, nb, POOL_PAST, POOL_WIDTH, rows)
    if n_tiles > 1:
        xp[0:past_rows, :] = xp[rows:rows + past_rows, :]


def _x_spec(layout, nb, rows):
    if layout == ROWS:
        return pl.BlockSpec((rows, D_MODEL), lambda i: (i, 0))
    if layout == LANES:
        return pl.BlockSpec((nb, rows // nb * D_MODEL), lambda i: (0, i))
    return pl.BlockSpec((nb, rows // nb, D_MODEL), lambda i: (0, i, 0))


def _branches_call(x, h0re, h0im, ppast, p, layer, bs, *, nb, npos, n_rows, start_pos, x_layout, emit_vn):
    rows = nb * npos
    n_tiles = n_rows // rows
    assert not emit_vn or n_tiles == 1
    past_rows = POOL_PAST * nb
    kern = functools.partial(_branches_kernel, nb=nb, npos=npos, n_tiles=n_tiles, start_pos=start_pos,
                             x_layout=x_layout, emit_vn=emit_vn)
    row_tile = lambda w: pl.BlockSpec((rows, w), lambda i: (i, 0))
    x_spec = _x_spec(x_layout, nb, rows)
    n_hbuf = SSM_KT if n_tiles > 1 else 1
    fixed = lambda r, w: pl.BlockSpec((r, w), lambda i: (0, 0))
    consts = (p['w_in_br'], p['bb'], p['a_re'], p['a_im'], p['cc'], p['d_skip'],
              p['w_glu'], p['b_glu'], p['gln_g'], p['gln_b'], p['w_s'], bs, p['pool_w'], p['pool_scale'])
    return pl.pallas_call(
        kern,
        grid=(n_tiles,),
        in_specs=[x_spec, _whole(), _whole(), _whole()] + [_layer_const(c, layer) for c in consts],
        out_specs=[row_tile(BRANCH_COLS - GMLP_WIDTH), fixed(nb, SSM_COLS), fixed(nb, SSM_COLS),
                   fixed(nb, POOL_PAST * POOL_WIDTH)] + ([fixed(nb, npos * GMLP_WIDTH)] if emit_vn else []),
        out_shape=[jax.ShapeDtypeStruct((n_rows, BRANCH_COLS - GMLP_WIDTH), BF16),
                   jax.ShapeDtypeStruct((nb, SSM_COLS), F32),
                   jax.ShapeDtypeStruct((nb, SSM_COLS), F32),
                   jax.ShapeDtypeStruct((nb, POOL_PAST * POOL_WIDTH), F32)]
                  + ([jax.ShapeDtypeStruct((nb, npos * GMLP_WIDTH), F32)] if emit_vn else []),
        scratch_shapes=[pltpu.VMEM((rows, 2 * KT_COLS), F32)] * n_hbuf
                       + [pltpu.VMEM((rows, SSM_WIDTH), F32),
                        pltpu.VMEM((rows, SSM_WIDTH), F32),
                        pltpu.VMEM((rows, GMLP_WIDTH), F32),
                        pltpu.VMEM((rows, GMLP_WIDTH), BF16),
                        pltpu.VMEM((rows, GMLP_WIDTH), F32),
                        pltpu.VMEM((rows + past_rows, POOL_WIDTH), F32),
                        pltpu.VMEM((GMLP_HEADS, rows, rows), BF16)],
        compiler_params=pltpu.CompilerParams(dimension_semantics=("arbitrary",),
                                             vmem_limit_bytes=VMEM_LIMIT_BYTES),
        name="branches",
    )(x, h0re, h0im, ppast, *consts)


def _merge_ffn_kernel(x_ref, ycat_ref, cpast_ref,
                      wg_ref, bg_ref, wbr_ref, wo_ref, g1_ref, b1_ref,
                      wup_ref, cw_ref, cb_ref, wdn_ref, g2_ref, b2_ref,
                      x2_ref, cout_ref, tailbuf, act, *, nb, rows, x_layout, out_layout):
    tail = (FFN_CONV - 1) * nb
    assert rows >= tail

    @pl.when(pl.program_id(0) == 0)
    def _init():
        _unpack_positions(tailbuf, cpast_ref, nb, FFN_CONV - 1, F2)

    x = _load_rows_t_major(x_ref, 0, rows, nb, x_layout)
    xb = x.astype(BF16)
    yc = ycat_ref[...]
    bounds = (0, SSM_WIDTH, SSM_WIDTH + GMLP_WIDTH, BRANCH_COLS - GMLP_WIDTH)
    merged = None
    for i in range(N_BRANCH):
        gate = _sigmoid(_dot(xb, wg_ref[:, i * D_MODEL:(i + 1) * D_MODEL])
                        + bg_ref[:, i * D_MODEL:(i + 1) * D_MODEL])
        br = _dot(yc[:, bounds[i]:bounds[i + 1]], wbr_ref[bounds[i]:bounds[i + 1], :])
        merged = gate * br if merged is None else merged + gate * br
    mix = _dot(merged.astype(BF16), wo_ref[...])
    x1 = _layer_norm(ALPHA * x + mix, g1_ref[...], b1_ref[...])

    x1b = x1.astype(BF16)
    for cc in range(D_FF // FF_COL_CHUNK):
        conv = []
        for part in range(2):
            lo = part * D_FF + cc * FF_COL_CHUNK
            cols = slice(lo, lo + FF_COL_CHUNK)
            h = _dot(x1b, wup_ref[:, cols])
            he = jnp.concatenate([tailbuf[:, cols], h], axis=0)
            tailbuf[:, cols] = h[rows - tail:rows, :]
            y = cb_ref[:, cols] + he[0:rows, :] * cw_ref[0:1, cols]
            y = y + he[nb:nb + rows, :] * cw_ref[1:2, cols]
            y = y + h * cw_ref[2:3, cols]
            conv.append(y)
        a = _gelu(conv[1]) * conv[0]
        act[:, cc * FF_COL_CHUNK:(cc + 1) * FF_COL_CHUNK] = a.astype(BF16)
    ff = _dot(act[...], wdn_ref[...])
    x2 = _layer_norm(ALPHA * x1 + ff, g2_ref[...], b2_ref[...])
    _store_rows_t_major(x2_ref, x2, nb, out_layout)

    @pl.when(pl.program_id(0) == pl.num_programs(0) - 1)
    def _emit_tail():
        _pack_positions(cout_ref, tailbuf, nb, FFN_CONV - 1, F2, 0)


def _merge_ffn_call(x, ycat, cpast, p, layer, *, nb, rows, x_layout, out_layout):
    n_rows = ycat.shape[0]
    tail = (FFN_CONV - 1) * nb
    kern = functools.partial(_merge_ffn_kernel, nb=nb, rows=rows, x_layout=x_layout, out_layout=out_layout)
    consts = (p['w_in_gate'], p['b_gate'], p['w_br'], p['w_o'], p['ln1_g'], p['ln1_b'],
              p['w_up'], p['conv_w'], p['conv_b'], p['w_down'], p['ln2_g'], p['ln2_b'])
    out_shapes = {ROWS: (n_rows, D_MODEL), SEQ: (nb, n_rows // nb, D_MODEL), LANES: (nb, n_rows // nb * D_MODEL)}
    return pl.pallas_call(
        kern,
        grid=(n_rows // rows,),
        in_specs=[_x_spec(x_layout, nb, rows),
                  pl.BlockSpec((rows, BRANCH_COLS - GMLP_WIDTH), lambda i: (i, 0)), _whole()]
                 + [_layer_const(c, layer) for c in consts],
        out_specs=[_x_spec(out_layout, nb, rows),
                   pl.BlockSpec((nb, (FFN_CONV - 1) * F2), lambda i: (0, 0))],
        out_shape=[jax.ShapeDtypeStruct(out_shapes[out_layout], F32),
                   jax.ShapeDtypeStruct((nb, (FFN_CONV - 1) * F2), F32)],
        scratch_shapes=[pltpu.VMEM((tail, F2), F32),
                        pltpu.VMEM((rows, D_FF), BF16)],
        compiler_params=pltpu.CompilerParams(dimension_semantics=("arbitrary",),
                                             vmem_limit_bytes=VMEM_LIMIT_BYTES),
        name="merge_convffn",
    )(x, ycat, cpast, *consts)


def _ssm_discretise(a_re, a_im, log_step, b_re, b_im):
    dt = jnp.exp(log_step)[..., None]
    mag = jnp.exp(a_re * dt)
    ab_re = mag * jnp.cos(a_im * dt)
    ab_im = mag * jnp.sin(a_im * dt)
    den = jnp.square(a_re) + jnp.square(a_im)
    nr = ab_re - 1.0
    k_re = (nr * a_re + ab_im * a_im) / den
    k_im = (ab_im * a_re - nr * a_im) / den
    bb_re = k_re[..., None] * b_re - k_im[..., None] * b_im
    bb_im = k_re[..., None] * b_im + k_im[..., None] * b_re
    return ab_re, ab_im, bb_re, bb_im


def _block_diag_in(m):
    gpt = SSM_GROUPS // SSM_KT
    m = jnp.transpose(m, (0, 1, 3, 2)).reshape(-1, SSM_KT, gpt, SSM_GROUP_CH, SSM_STATE)
    eye = jnp.eye(gpt, dtype=m.dtype)
    return jnp.einsum('lkgcn,gh->lkgchn', m, eye).reshape(-1, SSM_KT, gpt * SSM_GROUP_CH, gpt * SSM_STATE)


def _block_diag_out(m):
    gpt = SSM_GROUPS // SSM_KT
    m = jnp.transpose(m, (0, 1, 3, 2)).reshape(-1, SSM_KT, gpt, SSM_STATE, SSM_GROUP_CH)
    eye = jnp.eye(gpt, dtype=m.dtype)
    return jnp.einsum('lkgnc,gh->lkgnhc', m, eye).reshape(-1, SSM_KT, gpt * SSM_STATE, gpt * SSM_GROUP_CH)


def _gmlp_bias_rows(b_s, cl, nb):
    b = jnp.transpose(b_s[:, :, :cl], (0, 2, 1))
    return jnp.repeat(jnp.repeat(b, GMLP_HEAD_DIM, axis=2), nb, axis=1)


def _group_layer(x, st_re, st_im, st_pool, st_conv, p, layer, bs, *, nb, npos, n_rows, start_pos, ffn_rows,
                 x_layout, out_layout, emit_vn=False):
    ycat, h_re, h_im, pool_new, *vn = _branches_call(x, st_re, st_im, st_pool, p, layer, bs, nb=nb, npos=npos,
                                                      n_rows=n_rows, start_pos=start_pos, x_layout=x_layout,
                                                      emit_vn=emit_vn)
    x2, conv_new = _merge_ffn_call(x, ycat, st_conv, p, layer, nb=nb, rows=ffn_rows,
                                   x_layout=x_layout, out_layout=out_layout)
    return (x2, h_re, h_im, pool_new, conv_new, *vn)


def kernel(x_prompt, x_sample, state_ssm_re, state_ssm_im, state_pool, state_ffn_conv, w_in, b_gate, ssm_a_re, ssm_a_im, ssm_log_step, ssm_b_re, ssm_b_im, ssm_c_re, ssm_c_im, ssm_d, ssm_w_glu, ssm_b_glu, gmlp_ln_g, gmlp_ln_b, gmlp_w_s, gmlp_b_s, pool_w, pool_scale, w_br_ssm, w_br_gmlp, w_br_pool, w_o, ln1_g, ln1_b, ffn_w_up, ffn_conv_w, ffn_conv_b, ffn_w_down, ln2_g, ln2_b):
    n_p, t_p, _ = x_prompt.shape
    n_s, t_s, _ = x_sample.shape
    npos_p = TILE_ROWS // n_p
    npos_s = TILE_ROWS // n_s
    assert npos_p == GMLP_CHUNK and npos_s == t_s and t_p % npos_p == 0
    hp = x_prompt
    hs = x_sample.reshape(n_s, t_s * D_MODEL)
    pool_cols, conv_cols = POOL_PAST * POOL_WIDTH, (FFN_CONV - 1) * F2

    row = lambda v: v.reshape(DEPTH, 1, -1)
    ab_re, ab_im, bb_re, bb_im = _ssm_discretise(ssm_a_re, ssm_a_im, ssm_log_step, ssm_b_re, ssm_b_im)
    p = dict(
        w_in_br=w_in[:, :, :BRANCH_COLS].astype(BF16),
        w_in_gate=w_in[:, :, BRANCH_COLS:].astype(BF16),
        b_gate=row(b_gate),
        bb=jnp.concatenate([_block_diag_in(bb_re), _block_diag_in(bb_im)], axis=3).astype(BF16),
        a_re=row(ab_re), a_im=row(ab_im),
        cc=jnp.concatenate([_block_diag_out(ssm_c_re), -_block_diag_out(ssm_c_im)], axis=2).astype(BF16),
        d_skip=row(ssm_d),
        w_glu=ssm_w_glu.astype(BF16), b_glu=row(ssm_b_glu),
        gln_g=row(gmlp_ln_g), gln_b=row(gmlp_ln_b),
        w_s=gmlp_w_s, pool_w=pool_w.astype(BF16), pool_scale=row(pool_scale),
        w_br=jnp.concatenate([w_br_ssm, w_br_gmlp, w_br_pool], axis=1).astype(BF16),
        w_o=w_o.astype(BF16), ln1_g=row(ln1_g), ln1_b=row(ln1_b),
        w_up=ffn_w_up.astype(BF16), conv_w=ffn_conv_w, conv_b=row(ffn_conv_b),
        w_down=ffn_w_down.astype(BF16), ln2_g=row(ln2_g), ln2_b=row(ln2_b),
    )
    bs_p = _gmlp_bias_rows(gmlp_b_s, npos_p, n_p)
    bs_s = _gmlp_bias_rows(gmlp_b_s, npos_s, n_s)
    zeros_p = (jnp.zeros((n_p, SSM_COLS), F32), jnp.zeros((n_p, SSM_COLS), F32),
               jnp.zeros((n_p, pool_cols), F32), jnp.zeros((n_p, conv_cols), F32))

    outs = [[] for _ in range(9)]
    for l in range(DEPTH):
        hp, a_re_n, a_im_n, pool_n, conv_n = _group_layer(
            hp, *zeros_p, p, l, bs_p, nb=n_p, npos=npos_p, n_rows=n_p * t_p, start_pos=0, ffn_rows=2 * ROW_CHUNK,
            x_layout=SEQ if l == 0 else ROWS, out_layout=SEQ if l == DEPTH - 1 else ROWS)
        hs, b_re_n, b_im_n, pool_m, conv_m, v_n = _group_layer(
            hs, state_ssm_re[l].reshape(n_s, SSM_COLS), state_ssm_im[l].reshape(n_s, SSM_COLS),
            state_pool[l].reshape(n_s, pool_cols), state_ffn_conv[l].reshape(n_s, conv_cols),
            p, l, bs_s, nb=n_s, npos=npos_s, n_rows=n_s * t_s, start_pos=PAST_LEN, ffn_rows=ROW_CHUNK,
            x_layout=LANES, out_layout=LANES, emit_vn=True)

        vals = (a_re_n.reshape(n_p, SSM_GROUPS, SSM_STATE), a_im_n.reshape(n_p, SSM_GROUPS, SSM_STATE),
                pool_n.reshape(n_p, POOL_PAST, POOL_WIDTH), conv_n.reshape(n_p, FFN_CONV - 1, F2),
                b_re_n.reshape(n_s, SSM_GROUPS, SSM_STATE), b_im_n.reshape(n_s, SSM_GROUPS, SSM_STATE),
                v_n.reshape(n_s, t_s, GMLP_WIDTH), pool_m.reshape(n_s, POOL_PAST, POOL_WIDTH),
                conv_m.reshape(n_s, FFN_CONV - 1, F2))
        for o, v in zip(outs, vals):
            o.append(v)

    return (hp, hs.reshape(n_s, t_s, D_MODEL)) + tuple(jnp.stack(o) for o in outs)
```

```python
import functools
import math

import jax
import jax.numpy as jnp
from jax import lax
from jax.experimental import pallas as pl
from jax.experimental.pallas import tpu as pltpu

F32 = jnp.float32
BF16 = jnp.bfloat16

D_MODEL = 1024
DEPTH = 2
PAST_LEN = 16384
SSM_GROUP_CH = 16
SSM_WIDTH = 512
SSM_GROUPS = 32
SSM_STATE = 64
SSM_COLS = SSM_GROUPS * SSM_STATE
GMLP_HEADS = 4
GMLP_CHUNK = 128
GMLP_WIDTH = 256
GMLP_HEAD_DIM = 64
POOL_WINDOWS = (2, 4, 8, 16)
POOL_WIDTH = 512
POOL_GROUP_CH = 128
POOL_PAST = 15
N_BRANCH = 3
BRANCH_COLS = SSM_WIDTH + 2 * GMLP_WIDTH + POOL_WIDTH
D_FF = 2816
F2 = 2 * D_FF
FFN_CONV = 3
LN_EPS = 1e-5
ALPHA = (2 * DEPTH) ** 0.25

SUBLANES = 8
MXU_DIM = 256
VMEM_LIMIT_BYTES = 60 * 1024 * 1024

TILE_ROWS = 1024
ROW_CHUNK = 256
SSM_CHUNK = 512
SCAN_PASSES = 2
SCAN_BLOCKS = 2
SSM_KT = SSM_WIDTH // MXU_DIM
KT_COLS = SSM_COLS // SSM_KT
FF_COL_CHUNK = MXU_DIM
HEADS_PER_PASS = GMLP_HEADS // (SSM_KT * SCAN_PASSES)
assert HEADS_PER_PASS * SSM_KT * SCAN_PASSES == GMLP_HEADS


def _gelu(x):
    c = math.sqrt(2.0 / math.pi)
    return 0.5 * x * (1.0 + jnp.tanh(c * (x + 0.044715 * (x * x * x))))


def _sigmoid(x):
    return 1.0 / (1.0 + jnp.exp(-x))


def _layer_norm(x, g, b):
    mu = jnp.mean(x, axis=-1, keepdims=True)
    xc = x - mu
    var = jnp.mean(xc * xc, axis=-1, keepdims=True)
    return xc * lax.rsqrt(var + LN_EPS) * g + b


def _dot(a, b):
    return jnp.dot(a, b, preferred_element_type=F32)


def _load_rows_t_major(x_ref, r0, n, nb, seq_major):
    if not seq_major:
        return x_ref[pl.ds(r0, n), :]
    p0 = r0 // nb if isinstance(r0, int) else pl.multiple_of(r0 >> int(math.log2(nb)), SUBLANES)
    blk = x_ref[:, pl.ds(p0, n // nb), :]
    return jnp.swapaxes(blk, 0, 1).reshape(n, blk.shape[-1])


def _whole():
    return pl.BlockSpec(memory_space=pltpu.VMEM)


def _layer_const(a, layer):
    nd = a.ndim - 1
    return pl.BlockSpec((None,) + a.shape[1:], lambda i: (layer,) + (0,) * nd,
                        pipeline_mode=pl.Buffered(1))


def _expand_mix(ws_ref, wk, rows, shift, nb):
    cl = GMLP_CHUNK
    e = (lax.broadcasted_iota(jnp.int32, (rows, cl), 0) >> shift
         == lax.broadcasted_iota(jnp.int32, (rows, cl), 1)).astype(BF16)
    tril = (lax.broadcasted_iota(jnp.int32, (cl, cl), 0)
            >= lax.broadcasted_iota(jnp.int32, (cl, cl), 1))
    for h in range(GMLP_HEADS):
        w = jnp.where(tril, ws_ref[h], 0.0).astype(BF16)
        rows_w = _dot(e, w).astype(BF16)
        for c0 in range(0, rows, MXU_DIM):
            et = (lax.broadcasted_iota(jnp.int32, (cl, MXU_DIM), 0)
                  == (lax.broadcasted_iota(jnp.int32, (cl, MXU_DIM), 1) + c0) >> shift).astype(BF16)
            full = _dot(rows_w, et)
            same_seq = ((lax.broadcasted_iota(jnp.int32, (rows, MXU_DIM), 0) & (nb - 1))
                        == ((lax.broadcasted_iota(jnp.int32, (rows, MXU_DIM), 1) + c0) & (nb - 1)))
            wk[h, :, c0:c0 + MXU_DIM] = jnp.where(same_seq, full, 0.0).astype(BF16)


def _expand_ssm(bbc_ref, ccc_ref, bb, cc):
    n_in, n_st = KT_COLS // SSM_STATE * SSM_GROUP_CH, KT_COLS
    iota = lambda shape, d: lax.broadcasted_iota(jnp.int32, shape, d)
    log_n, log_c = int(math.log2(SSM_STATE)), int(math.log2(SSM_GROUP_CH))
    spread = ((iota((SSM_STATE, n_st), 1) & (SSM_STATE - 1)) == iota((SSM_STATE, n_st), 0)).astype(BF16)
    same_in = (iota((n_in, n_st), 0) >> log_c) == (iota((n_in, n_st), 1) >> log_n)
    gather = ((iota((n_st, SSM_STATE), 0) & (SSM_STATE - 1)) == iota((n_st, SSM_STATE), 1)).astype(BF16)
    same_out = (iota((n_st, n_in), 0) >> log_n) == (iota((n_st, n_in), 1) >> log_c)
    for kt in range(SSM_KT):
        for part in range(2):
            full = _dot(bbc_ref[kt, part], spread)
            bb[kt, :, part * n_st:(part + 1) * n_st] = jnp.where(same_in, full, 0.0).astype(BF16)
            full = _dot(gather, ccc_ref[kt, part])
            cc[kt, part * n_st:(part + 1) * n_st, :] = jnp.where(same_out, full, 0.0).astype(BF16)


def _branches_kernel(x_ref, h0re_ref, h0im_ref, ppast_ref,
                     w_in_ref, bbc_ref, are_ref, aim_ref, ccc_ref, dsk_ref,
                     wglu_ref, bglu_ref, lng_ref, lnb_ref, ws_ref, bs_ref,
                     poolw_ref, pscale_ref,
                     ycat_ref, hcre, hcim, pool_out_ref,
                     *scratch, nb, npos, n_tiles, start_pos, seq_major_in, emit_vn):
    vn_out_ref = scratch[0] if emit_vn else None
    *hbufs, us, ya, gu, vnb, sacc, xp, wk, bb_ref, cc_ref = scratch[1:] if emit_vn else scratch
    rows = nb * npos
    n_chunks = rows // ROW_CHUNK
    past_rows = POOL_PAST * nb
    tile = pl.program_id(0)
    shift = int(math.log2(nb))
    slabs_per_block = rows // SUBLANES // SCAN_BLOCKS
    steps = min(npos, slabs_per_block)
    groups_per_block = slabs_per_block // steps
    assert groups_per_block == 1 or steps == npos
    mix_rows = rows // SCAN_BLOCKS

    @pl.when(tile == 0)
    def _init():
        hcre[...] = h0re_ref[...]
        hcim[...] = h0im_ref[...]
        xp[0:past_rows, :] = ppast_ref[...]
        _expand_mix(ws_ref, wk, rows, shift, nb)
        _expand_ssm(bbc_ref, ccc_ref, bb_ref, cc_ref)

    base_pos = start_pos + tile * npos

    def proj_chunk(c, carry):
        r0 = pl.multiple_of(c * ROW_CHUNK, ROW_CHUNK)
        xb = _load_rows_t_major(x_ref, r0, ROW_CHUNK, nb, seq_major_in).astype(BF16)
        pa = _dot(xb, w_in_ref[...])
        us[pl.ds(r0, ROW_CHUNK), :] = pa[:, 0:SSM_WIDTH]
        gu[pl.ds(r0, ROW_CHUNK), :] = _gelu(pa[:, SSM_WIDTH:SSM_WIDTH + GMLP_WIDTH])
        v = _gelu(pa[:, SSM_WIDTH + GMLP_WIDTH:SSM_WIDTH + 2 * GMLP_WIDTH])
        v = _layer_norm(v, lng_ref[...], lnb_ref[...])
        if emit_vn:
            vn_out_ref[pl.ds(r0, ROW_CHUNK), :] = v
        vnb[pl.ds(r0, ROW_CHUNK), :] = v.astype(BF16)
        sacc[pl.ds(r0, ROW_CHUNK), :] = bs_ref[pl.ds(r0, ROW_CHUNK), :]
        xp[pl.ds(past_rows + r0, ROW_CHUNK), :] = pa[:, SSM_WIDTH + 2 * GMLP_WIDTH:BRANCH_COLS]
        return carry

    lax.fori_loop(0, n_chunks, proj_chunk, 0)

    n_hbuf = len(hbufs)
    overlap = n_hbuf > 1

    def bu_rows(kt, r0):
        ub = us[pl.ds(r0, SSM_CHUNK), kt * MXU_DIM:(kt + 1) * MXU_DIM].astype(BF16)
        hbufs[kt % n_hbuf][pl.ds(r0, SSM_CHUNK), :] = _dot(ub, bb_ref[kt])

    def y_rows(kt, r0):
        hb = hbufs[kt % n_hbuf][pl.ds(r0, SSM_CHUNK), :].astype(BF16)
        u = us[pl.ds(r0, SSM_CHUNK), kt * MXU_DIM:(kt + 1) * MXU_DIM]
        y = _dot(hb, cc_ref[kt]) + dsk_ref[:, kt * MXU_DIM:(kt + 1) * MXU_DIM] * u
        ya[pl.ds(r0, SSM_CHUNK), kt * MXU_DIM:(kt + 1) * MXU_DIM] = y

    def chunk_loop(fn, kt):
        def body(c, carry):
            fn(kt, pl.multiple_of(c * SSM_CHUNK, SSM_CHUNK))
            return carry
        lax.fori_loop(0, rows // SSM_CHUNK, body, 0)

    if overlap:
        assert rows // SSM_CHUNK == SCAN_BLOCKS
        chunk_loop(bu_rows, 0)
    for kt in range(SSM_KT):
        c_lo = kt * KT_COLS
        hbuf = hbufs[kt % n_hbuf]
        if not overlap:
            chunk_loop(bu_rows, kt)

        width = KT_COLS // SCAN_PASSES
        for ch in range(SCAN_PASSES):
            lo = ch * width
            if overlap:
                heads_kt = GMLP_HEADS // SSM_KT
                mix_heads = range(kt * heads_kt, (kt + 1) * heads_kt) if ch == SCAN_PASSES - 1 else ()
            else:
                first = (kt * SCAN_PASSES + ch) * HEADS_PER_PASS
                mix_heads = range(first, first + HEADS_PER_PASS)
            a_r = jnp.broadcast_to(are_ref[:, c_lo + lo:c_lo + lo + width], (SUBLANES, width))
            a_i = jnp.broadcast_to(aim_ref[:, c_lo + lo:c_lo + lo + width], (SUBLANES, width))

            def scan_block(j, carry, kt=kt, hbuf=hbuf, ch=ch, lo=lo, a_r=a_r, a_i=a_i, c_lo=c_lo, mix_heads=mix_heads):
                for q in range(groups_per_block):
                    if groups_per_block == 1 and steps < npos:
                        t0, b0 = j * steps, 0
                    else:
                        t0, b0 = 0, (j * groups_per_block + q) * SUBLANES
                    h_r = hcre[pl.ds(b0, SUBLANES), c_lo + lo:c_lo + lo + width]
                    h_i = hcim[pl.ds(b0, SUBLANES), c_lo + lo:c_lo + lo + width]
                    for t in range(steps):
                        row = (t0 + t) * nb + b0
                        b_r = hbuf[pl.ds(row, SUBLANES), lo:lo + width]
                        b_i = hbuf[pl.ds(row, SUBLANES), KT_COLS + lo:KT_COLS + lo + width]
                        h_r, h_i = (a_r * h_r - a_i * h_i + b_r,
                                    a_r * h_i + a_i * h_r + b_i)
                        hbuf[pl.ds(row, SUBLANES), lo:lo + width] = h_r
                        hbuf[pl.ds(row, SUBLANES), KT_COLS + lo:KT_COLS + lo + width] = h_i
                    hcre[pl.ds(b0, SUBLANES), c_lo + lo:c_lo + lo + width] = h_r
                    hcim[pl.ds(b0, SUBLANES), c_lo + lo:c_lo + lo + width] = h_i
                if mix_heads:
                    m0 = j * mix_rows
                    head = lax.broadcasted_iota(jnp.int32, (1, GMLP_WIDTH), 1) // GMLP_HEAD_DIM
                    acc = sacc[pl.ds(m0, mix_rows), :]
                    for mh in mix_heads:
                        mixed = _dot(wk[mh, pl.ds(m0, mix_rows), :], vnb[...])
                        acc = acc + jnp.where(head == mh, mixed, 0.0)
                    sacc[pl.ds(m0, mix_rows), :] = acc
                if overlap and ch == 0:
                    r0 = j * SSM_CHUNK
                    if kt + 1 < SSM_KT:
                        bu_rows(kt + 1, r0)
                    if kt > 0:
                        y_rows(kt - 1, r0)
                return carry

            for j in range(SCAN_BLOCKS):
                scan_block(j, 0)

        if not overlap:
            chunk_loop(y_rows, kt)
    if overlap:
        chunk_loop(y_rows, SSM_KT - 1)

    def out_chunk(c, carry):
        r0 = pl.multiple_of(c * ROW_CHUNK, ROW_CHUNK)
        z = _gelu(ya[pl.ds(r0, ROW_CHUNK), :])
        o_a = z * _sigmoid(_dot(z.astype(BF16), wglu_ref[...]) + bglu_ref[...])
        ycat_ref[pl.ds(r0, ROW_CHUNK), 0:SSM_WIDTH] = o_a.astype(BF16)
        y_b = gu[pl.ds(r0, ROW_CHUNK), :] * sacc[pl.ds(r0, ROW_CHUNK), :]
        ycat_ref[pl.ds(r0, ROW_CHUNK), SSM_WIDTH:SSM_WIDTH + GMLP_WIDTH] = y_b.astype(BF16)

        ridx = lax.broadcasted_iota(jnp.int32, (ROW_CHUNK, 1), 0) + r0
        pos = base_pos + (ridx >> shift)
        for g, win in enumerate(POOL_WINDOWS):
            cl = g * POOL_GROUP_CH
            reach = (win - 1) * nb
            if reach < ROW_CHUNK:
                off = pl.multiple_of(past_rows - reach + r0, SUBLANES)
                ext = xp[pl.ds(off, ROW_CHUNK + reach), cl:cl + POOL_GROUP_CH]
                cur = ext[reach:, :]
                wsum, span = ext, 1
                while span < win:
                    n = wsum.shape[0] - span * nb
                    wsum = wsum[span * nb:, :] + wsum[:n, :]
                    span *= 2
            else:
                cur = xp[pl.ds(past_rows + r0, ROW_CHUNK), cl:cl + POOL_GROUP_CH]
                wsum = cur
                for j in range(1, win):
                    off = pl.multiple_of(past_rows - j * nb + r0, SUBLANES)
                    wsum = wsum + xp[pl.ds(off, ROW_CHUNK), cl:cl + POOL_GROUP_CH]
            cnt = jnp.minimum(pos + 1, win).astype(F32)
            d = wsum / cnt - cur
            y_c = _dot(d.astype(BF16), poolw_ref[g]) * pscale_ref[:, cl:cl + POOL_GROUP_CH]
            col = SSM_WIDTH + GMLP_WIDTH + cl
            ycat_ref[pl.ds(r0, ROW_CHUNK), col:col + POOL_GROUP_CH] = y_c.astype(BF16)
        return carry

    lax.fori_loop(0, n_chunks, out_chunk, 0)

    pool_out_ref[...] = xp[rows:rows + past_rows, :]
    if n_tiles > 1:
        xp[0:past_rows, :] = xp[rows:rows + past_rows, :]


def _branches_call(x, h0re, h0im, ppast, p, layer, bs, *, nb, npos, start_pos, emit_vn):
    seq_major_in = x.ndim == 3
    n_rows = x.shape[0] * x.shape[1] if seq_major_in else x.shape[0]
    rows = nb * npos
    n_tiles = n_rows // rows
    past_rows = POOL_PAST * nb
    kern = functools.partial(_branches_kernel, nb=nb, npos=npos, n_tiles=n_tiles, start_pos=start_pos,
                             seq_major_in=seq_major_in, emit_vn=emit_vn)
    row_tile = lambda w: pl.BlockSpec((rows, w), lambda i: (i, 0))
    x_spec = pl.BlockSpec((nb, npos, D_MODEL), lambda i: (0, i, 0)) if seq_major_in else row_tile(D_MODEL)
    n_hbuf = SSM_KT if n_tiles > 1 else 1
    fixed = lambda r, w: pl.BlockSpec((r, w), lambda i: (0, 0))
    consts = (p['w_in_br'], p['bbc'], p['a_re'], p['a_im'], p['ccc'], p['d_skip'],
              p['w_glu'], p['b_glu'], p['gln_g'], p['gln_b'], p['w_s'], bs, p['pool_w'], p['pool_scale'])
    return pl.pallas_call(
        kern,
        grid=(n_tiles,),
        in_specs=[x_spec, _whole(), _whole(), _whole()] + [_layer_const(c, layer) for c in consts],
        out_specs=[row_tile(BRANCH_COLS - GMLP_WIDTH), fixed(nb, SSM_COLS), fixed(nb, SSM_COLS),
                   fixed(past_rows, POOL_WIDTH)] + ([row_tile(GMLP_WIDTH)] if emit_vn else []),
        out_shape=[jax.ShapeDtypeStruct((n_rows, BRANCH_COLS - GMLP_WIDTH), BF16),
                   jax.ShapeDtypeStruct((nb, SSM_COLS), F32),
                   jax.ShapeDtypeStruct((nb, SSM_COLS), F32),
                   jax.ShapeDtypeStruct((past_rows, POOL_WIDTH), F32)]
                  + ([jax.ShapeDtypeStruct((n_rows, GMLP_WIDTH), F32)] if emit_vn else []),
        scratch_shapes=[pltpu.VMEM((rows, 2 * KT_COLS), F32)] * n_hbuf
                       + [pltpu.VMEM((rows, SSM_WIDTH), F32),
                        pltpu.VMEM((rows, SSM_WIDTH), F32),
                        pltpu.VMEM((rows, GMLP_WIDTH), F32),
                        pltpu.VMEM((rows, GMLP_WIDTH), BF16),
                        pltpu.VMEM((rows, GMLP_WIDTH), F32),
                        pltpu.VMEM((rows + past_rows, POOL_WIDTH), F32),
                        pltpu.VMEM((GMLP_HEADS, rows, rows), BF16),
                        pltpu.VMEM((SSM_KT, MXU_DIM, 2 * KT_COLS), BF16),
                        pltpu.VMEM((SSM_KT, 2 * KT_COLS, MXU_DIM), BF16)],
        compiler_params=pltpu.CompilerParams(dimension_semantics=("arbitrary",),
                                             vmem_limit_bytes=VMEM_LIMIT_BYTES),
        name="branches",
    )(x, h0re, h0im, ppast, *consts)


def _merge_ffn_kernel(x_ref, ycat_ref, cpast_ref,
                      wg_ref, bg_ref, wbr_ref, wo_ref, g1_ref, b1_ref,
                      wup_ref, cw_ref, cb_ref, wdn_ref, g2_ref, b2_ref,
                      x2_ref, cout_ref, tailbuf, act, *, nb, rows, seq_major_in, seq_major_out):
    tail = (FFN_CONV - 1) * nb
    assert rows >= tail

    @pl.when(pl.program_id(0) == 0)
    def _init():
        tailbuf[...] = cpast_ref[...]

    x = _load_rows_t_major(x_ref, 0, rows, nb, seq_major_in)
    xb = x.astype(BF16)
    yc = ycat_ref[...]
    bounds = (0, SSM_WIDTH, SSM_WIDTH + GMLP_WIDTH, BRANCH_COLS - GMLP_WIDTH)
    merged = None
    for i in range(N_BRANCH):
        gate = _sigmoid(_dot(xb, wg_ref[:, i * D_MODEL:(i + 1) * D_MODEL])
                        + bg_ref[:, i * D_MODEL:(i + 1) * D_MODEL])
        br = _dot(yc[:, bounds[i]:bounds[i + 1]], wbr_ref[bounds[i]:bounds[i + 1], :])
        merged = gate * br if merged is None else merged + gate * br
    mix = _dot(merged.astype(BF16), wo_ref[...])
    x1 = _layer_norm(ALPHA * x + mix, g1_ref[...], b1_ref[...])

    x1b = x1.astype(BF16)
    for cc in range(D_FF // FF_COL_CHUNK):
        conv = []
        for part in range(2):
            lo = part * D_FF + cc * FF_COL_CHUNK
            cols = slice(lo, lo + FF_COL_CHUNK)
            h = _dot(x1b, wup_ref[:, cols])
            he = jnp.concatenate([tailbuf[:, cols], h], axis=0)
            tailbuf[:, cols] = h[rows - tail:rows, :]
            y = cb_ref[:, cols] + he[0:rows, :] * cw_ref[0:1, cols]
            y = y + he[nb:nb + rows, :] * cw_ref[1:2, cols]
            y = y + h * cw_ref[2:3, cols]
            conv.append(y)
        a = _gelu(conv[1]) * conv[0]
        act[:, cc * FF_COL_CHUNK:(cc + 1) * FF_COL_CHUNK] = a.astype(BF16)
    ff = _dot(act[...], wdn_ref[...])
    x2 = _layer_norm(ALPHA * x1 + ff, g2_ref[...], b2_ref[...])
    if seq_major_out:
        x2_ref[...] = jnp.swapaxes(x2.reshape(rows // nb, nb, D_MODEL), 0, 1)
    else:
        x2_ref[...] = x2
    cout_ref[...] = tailbuf[...]


def _merge_ffn_call(x, ycat, cpast, p, layer, *, nb, rows, seq_major_out):
    seq_major_in = x.ndim == 3
    n_rows = ycat.shape[0]
    tail = (FFN_CONV - 1) * nb
    kern = functools.partial(_merge_ffn_kernel, nb=nb, rows=rows,
                             seq_major_in=seq_major_in, seq_major_out=seq_major_out)
    consts = (p['w_in_gate'], p['b_gate'], p['w_br'], p['w_o'], p['ln1_g'], p['ln1_b'],
              p['w_up'], p['conv_w'], p['conv_b'], p['w_down'], p['ln2_g'], p['ln2_b'])
    t_major_spec = pl.BlockSpec((rows, D_MODEL), lambda i: (i, 0))
    seq_major_spec = pl.BlockSpec((nb, rows // nb, D_MODEL), lambda i: (0, i, 0))
    return pl.pallas_call(
        kern,
        grid=(n_rows // rows,),
        in_specs=[seq_major_spec if seq_major_in else t_major_spec,
                  pl.BlockSpec((rows, BRANCH_COLS - GMLP_WIDTH), lambda i: (i, 0)), _whole()]
                 + [_layer_const(c, layer) for c in consts],
        out_specs=[seq_major_spec if seq_major_out else t_major_spec,
                   pl.BlockSpec((tail, F2), lambda i: (0, 0))],
        out_shape=[jax.ShapeDtypeStruct((nb, n_rows // nb, D_MODEL) if seq_major_out else (n_rows, D_MODEL), F32),
                   jax.ShapeDtypeStruct((tail, F2), F32)],
        scratch_shapes=[pltpu.VMEM((tail, F2), F32),
                        pltpu.VMEM((rows, D_FF), BF16)],
        compiler_params=pltpu.CompilerParams(dimension_semantics=("arbitrary",),
                                             vmem_limit_bytes=VMEM_LIMIT_BYTES),
        name="merge_convffn",
    )(x, ycat, cpast, *consts)


def _ssm_discretise(a_re, a_im, log_step, b_re, b_im):
    dt = jnp.exp(log_step)[..., None]
    mag = jnp.exp(a_re * dt)
    ab_re = mag * jnp.cos(a_im * dt)
    ab_im = mag * jnp.sin(a_im * dt)
    den = jnp.square(a_re) + jnp.square(a_im)
    nr = ab_re - 1.0
    k_re = (nr * a_re + ab_im * a_im) / den
    k_im = (ab_im * a_re - nr * a_im) / den
    bb_re = k_re[..., None] * b_re - k_im[..., None] * b_im
    bb_im = k_re[..., None] * b_im + k_im[..., None] * b_re
    return ab_re, ab_im, bb_re, bb_im


def _compact_in(m):
    gpt = SSM_GROUPS // SSM_KT
    return jnp.transpose(m, (0, 1, 3, 2)).reshape(-1, SSM_KT, gpt * SSM_GROUP_CH, SSM_STATE)


def _compact_out(m):
    gpt = SSM_GROUPS // SSM_KT
    m = m.reshape(-1, SSM_KT, gpt, SSM_GROUP_CH, SSM_STATE)
    return jnp.transpose(m, (0, 1, 4, 2, 3)).reshape(-1, SSM_KT, SSM_STATE, gpt * SSM_GROUP_CH)


def _gmlp_bias_rows(b_s, cl, nb):
    b = jnp.transpose(b_s[:, :, :cl], (0, 2, 1))
    return jnp.repeat(jnp.repeat(b, GMLP_HEAD_DIM, axis=2), nb, axis=1)


def _t_major(a):
    return jnp.transpose(a, (1, 0, 2)).reshape(a.shape[0] * a.shape[1], a.shape[2])


def _seq_major(a, n_seq):
    return jnp.transpose(a.reshape(a.shape[0] // n_seq, n_seq, a.shape[1]), (1, 0, 2))


def _group_layer(x, st_re, st_im, st_pool, st_conv, p, layer, bs, *, nb, npos, start_pos, ffn_rows,
                 seq_major_out=False, emit_vn=False):
    ycat, h_re, h_im, pool_new, *vn = _branches_call(x, st_re, st_im, st_pool, p, layer, bs,
                                                      nb=nb, npos=npos, start_pos=start_pos, emit_vn=emit_vn)
    x2, conv_new = _merge_ffn_call(x, ycat, st_conv, p, layer, nb=nb, rows=ffn_rows,
                                   seq_major_out=seq_major_out)
    return (x2, h_re, h_im, pool_new, conv_new, *vn)


def kernel(x_prompt, x_sample, state_ssm_re, state_ssm_im, state_pool, state_ffn_conv, w_in, b_gate, ssm_a_re, ssm_a_im, ssm_log_step, ssm_b_re, ssm_b_im, ssm_c_re, ssm_c_im, ssm_d, ssm_w_glu, ssm_b_glu, gmlp_ln_g, gmlp_ln_b, gmlp_w_s, gmlp_b_s, pool_w, pool_scale, w_br_ssm, w_br_gmlp, w_br_pool, w_o, ln1_g, ln1_b, ffn_w_up, ffn_conv_w, ffn_conv_b, ffn_w_down, ln2_g, ln2_b):
    n_p, t_p, _ = x_prompt.shape
    n_s, t_s, _ = x_sample.shape
    hp = x_prompt
    hs = _t_major(x_sample)
    npos_p = TILE_ROWS // n_p
    npos_s = TILE_ROWS // n_s
    assert npos_p == GMLP_CHUNK and npos_s == t_s and t_p % npos_p == 0

    row = lambda v: v.reshape(DEPTH, 1, -1)
    ab_re, ab_im, bb_re, bb_im = _ssm_discretise(ssm_a_re, ssm_a_im, ssm_log_step, ssm_b_re, ssm_b_im)
    p = dict(
        w_in_br=w_in[:, :, :BRANCH_COLS].astype(BF16),
        w_in_gate=w_in[:, :, BRANCH_COLS:].astype(BF16),
        b_gate=row(b_gate),
        bbc=jnp.stack([_compact_in(bb_re), _compact_in(bb_im)], axis=2).astype(BF16),
        a_re=row(ab_re), a_im=row(ab_im),
        ccc=jnp.stack([_compact_out(ssm_c_re), -_compact_out(ssm_c_im)], axis=2).astype(BF16),
        d_skip=row(ssm_d),
        w_glu=ssm_w_glu.astype(BF16), b_glu=row(ssm_b_glu),
        gln_g=row(gmlp_ln_g), gln_b=row(gmlp_ln_b),
        w_s=gmlp_w_s, pool_w=pool_w.astype(BF16), pool_scale=row(pool_scale),
        w_br=jnp.concatenate([w_br_ssm, w_br_gmlp, w_br_pool], axis=1).astype(BF16),
        w_o=w_o.astype(BF16), ln1_g=row(ln1_g), ln1_b=row(ln1_b),
        w_up=ffn_w_up.astype(BF16), conv_w=ffn_conv_w, conv_b=row(ffn_conv_b),
        w_down=ffn_w_down.astype(BF16), ln2_g=row(ln2_g), ln2_b=row(ln2_b),
    )
    bs_p = _gmlp_bias_rows(gmlp_b_s, npos_p, n_p)
    bs_s = _gmlp_bias_rows(gmlp_b_s, npos_s, n_s)
    zeros_p = (jnp.zeros((n_p, SSM_COLS), F32), jnp.zeros((n_p, SSM_COLS), F32),
               jnp.zeros((POOL_PAST * n_p, POOL_WIDTH), F32), jnp.zeros(((FFN_CONV - 1) * n_p, F2), F32))

    pool_t = jnp.transpose(state_pool, (0, 2, 1, 3)).reshape(DEPTH, POOL_PAST * n_s, POOL_WIDTH)
    conv_t = jnp.transpose(state_ffn_conv, (0, 2, 1, 3)).reshape(DEPTH, (FFN_CONV - 1) * n_s, F2)

    outs = [[] for _ in range(9)]
    for l in range(DEPTH):
        hp, a_re_n, a_im_n, pool_n, conv_n = _group_layer(
            hp, *zeros_p, p, l, bs_p, nb=n_p, npos=npos_p, start_pos=0, ffn_rows=2 * ROW_CHUNK,
            seq_major_out=(l == DEPTH - 1))
        hs, b_re_n, b_im_n, pool_m, conv_m, v_n = _group_layer(
            hs, state_ssm_re[l].reshape(n_s, SSM_COLS), state_ssm_im[l].reshape(n_s, SSM_COLS),
            pool_t[l], conv_t[l],
            p, l, bs_s, nb=n_s, npos=npos_s, start_pos=PAST_LEN, ffn_rows=ROW_CHUNK, emit_vn=True)
        for o, v in zip(outs, (a_re_n, a_im_n, pool_n, conv_n, b_re_n, b_im_n, v_n, pool_m, conv_m)):
            o.append(v)

    p_re, p_im, p_pool, p_conv, s_re, s_im, s_v, s_pool, s_conv = (jnp.stack(o) for o in outs)
    ssm = lambda a, n_seq: a.reshape(DEPTH, n_seq, SSM_GROUPS, SSM_STATE)
    per_seq = lambda a, n_seq: jnp.transpose(a.reshape(DEPTH, -1, n_seq, a.shape[-1]), (0, 2, 1, 3))
    return (hp, _seq_major(hs, n_s),
            ssm(p_re, n_p), ssm(p_im, n_p), per_seq(p_pool, n_p), per_seq(p_conv, n_p),
            ssm(s_re, n_s), ssm(s_im, n_s), per_seq(s_v, n_s), per_seq(s_pool, n_s), per_seq(s_conv, n_s))
```

```python
import functools
import math

import jax
import jax.numpy as jnp
from jax import lax
from jax.experimental import pallas as pl
from jax.experimental.pallas import tpu as pltpu

F32 = jnp.float32
BF16 = jnp.bfloat16

D_MODEL = 1024
DEPTH = 2
PAST_LEN = 16384
SSM_GROUP_CH = 16
SSM_WIDTH = 512
SSM_GROUPS = 32
SSM_STATE = 64
SSM_COLS = SSM_GROUPS * SSM_STATE
GMLP_HEADS = 4
GMLP_CHUNK = 128
GMLP_WIDTH = 256
GMLP_HEAD_DIM = 64
POOL_WINDOWS = (2, 4, 8, 16)
POOL_WIDTH = 512
POOL_GROUP_CH = 128
POOL_PAST = 15
N_BRANCH = 3
BRANCH_COLS = SSM_WIDTH + 2 * GMLP_WIDTH + POOL_WIDTH
D_FF = 2816
F2 = 2 * D_FF
FFN_CONV = 3
LN_EPS = 1e-5
ALPHA = (2 * DEPTH) ** 0.25

SUBLANES = 8
MXU_DIM = 256
VMEM_LIMIT_BYTES = 60 * 1024 * 1024

TILE_ROWS = 1024
ROW_CHUNK = 256
SSM_CHUNK = 512
SCAN_PASSES = 2
SCAN_BLOCKS = 2
SSM_KT = SSM_WIDTH // MXU_DIM
KT_COLS = SSM_COLS // SSM_KT
FF_COL_CHUNK = MXU_DIM
HEADS_PER_PASS = GMLP_HEADS // (SSM_KT * SCAN_PASSES)
assert HEADS_PER_PASS * SSM_KT * SCAN_PASSES == GMLP_HEADS


def _gelu(x):
    c = math.sqrt(2.0 / math.pi)
    return 0.5 * x * (1.0 + jnp.tanh(c * (x + 0.044715 * (x * x * x))))


def _sigmoid(x):
    return 1.0 / (1.0 + jnp.exp(-x))


def _layer_norm(x, g, b):
    mu = jnp.mean(x, axis=-1, keepdims=True)
    xc = x - mu
    var = jnp.mean(xc * xc, axis=-1, keepdims=True)
    return xc * lax.rsqrt(var + LN_EPS) * g + b


def _dot(a, b):
    return jnp.dot(a, b, preferred_element_type=F32)


def _load_rows_t_major(x_ref, r0, n, nb, seq_major):
    if not seq_major:
        return x_ref[pl.ds(r0, n), :]
    p0 = r0 // nb if isinstance(r0, int) else pl.multiple_of(r0 >> int(math.log2(nb)), SUBLANES)
    blk = x_ref[:, pl.ds(p0, n // nb), :]
    return jnp.swapaxes(blk, 0, 1).reshape(n, blk.shape[-1])


def _layer_const(a, layer):
    nd = a.ndim - 1
    return pl.BlockSpec((None,) + a.shape[1:], lambda i: (layer,) + (0,) * nd,
                        pipeline_mode=pl.Buffered(1))


def _expand_mix(ws_ref, wk, rows, shift, nb):
    cl = GMLP_CHUNK
    e = (lax.broadcasted_iota(jnp.int32, (rows, cl), 0) >> shift
         == lax.broadcasted_iota(jnp.int32, (rows, cl), 1)).astype(BF16)
    tril = (lax.broadcasted_iota(jnp.int32, (cl, cl), 0)
            >= lax.broadcasted_iota(jnp.int32, (cl, cl), 1))
    for h in range(GMLP_HEADS):
        w = jnp.where(tril, ws_ref[h], 0.0).astype(BF16)
        rows_w = _dot(e, w).astype(BF16)
        for c0 in range(0, rows, MXU_DIM):
            et = (lax.broadcasted_iota(jnp.int32, (cl, MXU_DIM), 0)
                  == (lax.broadcasted_iota(jnp.int32, (cl, MXU_DIM), 1) + c0) >> shift).astype(BF16)
            full = _dot(rows_w, et)
            same_seq = ((lax.broadcasted_iota(jnp.int32, (rows, MXU_DIM), 0) & (nb - 1))
                        == ((lax.broadcasted_iota(jnp.int32, (rows, MXU_DIM), 1) + c0) & (nb - 1)))
            wk[h, :, c0:c0 + MXU_DIM] = jnp.where(same_seq, full, 0.0).astype(BF16)


def _expand_ssm(bbc_ref, ccc_ref, bb, cc):
    n_in, n_st = KT_COLS // SSM_STATE * SSM_GROUP_CH, KT_COLS
    iota = lambda shape, d: lax.broadcasted_iota(jnp.int32, shape, d)
    log_n, log_c = int(math.log2(SSM_STATE)), int(math.log2(SSM_GROUP_CH))
    spread = ((iota((SSM_STATE, n_st), 1) & (SSM_STATE - 1)) == iota((SSM_STATE, n_st), 0)).astype(BF16)
    same_in = (iota((n_in, n_st), 0) >> log_c) == (iota((n_in, n_st), 1) >> log_n)
    gather = ((iota((n_st, SSM_STATE), 0) & (SSM_STATE - 1)) == iota((n_st, SSM_STATE), 1)).astype(BF16)
    same_out = (iota((n_st, n_in), 0) >> log_n) == (iota((n_st, n_in), 1) >> log_c)
    for kt in range(SSM_KT):
        for part in range(2):
            full = _dot(bbc_ref[kt, part], spread)
            bb[kt, :, part * n_st:(part + 1) * n_st] = jnp.where(same_in, full, 0.0).astype(BF16)
            full = _dot(gather, ccc_ref[kt, part])
            cc[kt, part * n_st:(part + 1) * n_st, :] = jnp.where(same_out, full, 0.0).astype(BF16)


def _branches_kernel(x_ref, h0re_ref, h0im_ref, ppast_ref,
                     w_in_ref, bbc_ref, are_ref, aim_ref, ccc_ref, dsk_ref,
                     wglu_ref, bglu_ref, lng_ref, lnb_ref, ws_ref, bs_ref,
                     poolw_ref, pscale_ref,
                     ycat_ref, hcre, hcim, pool_out_ref,
                     *scratch, nb, npos, n_tiles, start_pos, seq_major_in, emit_vn):
    vn_out_ref = scratch[0] if emit_vn else None
    *hbufs, us, ya, gu, vnb, sacc, xp, wk, bb_ref, cc_ref = scratch[1:] if emit_vn else scratch
    rows = nb * npos
    n_chunks = rows // ROW_CHUNK
    past_rows = POOL_PAST * nb
    tile = pl.program_id(0)
    shift = int(math.log2(nb))
    slabs_per_block = rows // SUBLANES // SCAN_BLOCKS
    steps = min(npos, slabs_per_block)
    groups_per_block = slabs_per_block // steps
    assert groups_per_block == 1 or steps == npos
    mix_rows = rows // SCAN_BLOCKS

    @pl.when(tile == 0)
    def _init():
        hcre[...] = h0re_ref[...]
        hcim[...] = h0im_ref[...]
        xp[0:past_rows, :] = ppast_ref[...]
        _expand_mix(ws_ref, wk, rows, shift, nb)
        _expand_ssm(bbc_ref, ccc_ref, bb_ref, cc_ref)

    base_pos = start_pos + tile * npos

    def proj_chunk(c, carry):
        r0 = pl.multiple_of(c * ROW_CHUNK, ROW_CHUNK)
        xb = _load_rows_t_major(x_ref, r0, ROW_CHUNK, nb, seq_major_in).astype(BF16)
        pa = _dot(xb, w_in_ref[...])
        us[pl.ds(r0, ROW_CHUNK), :] = pa[:, 0:SSM_WIDTH]
        gu[pl.ds(r0, ROW_CHUNK), :] = _gelu(pa[:, SSM_WIDTH:SSM_WIDTH + GMLP_WIDTH])
        v = _gelu(pa[:, SSM_WIDTH + GMLP_WIDTH:SSM_WIDTH + 2 * GMLP_WIDTH])
        v = _layer_norm(v, lng_ref[...], lnb_ref[...])
        if emit_vn:
            vn_out_ref[pl.ds(r0, ROW_CHUNK), :] = v
        vnb[pl.ds(r0, ROW_CHUNK), :] = v.astype(BF16)
        sacc[pl.ds(r0, ROW_CHUNK), :] = bs_ref[pl.ds(r0, ROW_CHUNK), :]
        xp[pl.ds(past_rows + r0, ROW_CHUNK), :] = pa[:, SSM_WIDTH + 2 * GMLP_WIDTH:BRANCH_COLS]
        return carry

    lax.fori_loop(0, n_chunks, proj_chunk, 0)

    n_hbuf = len(hbufs)
    overlap = n_hbuf > 1

    def bu_rows(kt, r0):
        ub = us[pl.ds(r0, SSM_CHUNK), kt * MXU_DIM:(kt + 1) * MXU_DIM].astype(BF16)
        hbufs[kt % n_hbuf][pl.ds(r0, SSM_CHUNK), :] = _dot(ub, bb_ref[kt])

    def y_rows(kt, r0):
        hb = hbufs[kt % n_hbuf][pl.ds(r0, SSM_CHUNK), :].astype(BF16)
        u = us[pl.ds(r0, SSM_CHUNK), kt * MXU_DIM:(kt + 1) * MXU_DIM]
        y = _dot(hb, cc_ref[kt]) + dsk_ref[:, kt * MXU_DIM:(kt + 1) * MXU_DIM] * u
        ya[pl.ds(r0, SSM_CHUNK), kt * MXU_DIM:(kt + 1) * MXU_DIM] = y

    def chunk_loop(fn, kt):
        def body(c, carry):
            fn(kt, pl.multiple_of(c * SSM_CHUNK, SSM_CHUNK))
            return carry
        lax.fori_loop(0, rows // SSM_CHUNK, body, 0)

    if overlap:
        assert rows // SSM_CHUNK == SCAN_BLOCKS
        chunk_loop(bu_rows, 0)
    for kt in range(SSM_KT):
        c_lo = kt * KT_COLS
        hbuf = hbufs[kt % n_hbuf]
        if not overlap:
            chunk_loop(bu_rows, kt)

        width = KT_COLS // SCAN_PASSES
        for ch in range(SCAN_PASSES):
            lo = ch * width
            if overlap:
                heads_kt = GMLP_HEADS // SSM_KT
                mix_heads = range(kt * heads_kt, (kt + 1) * heads_kt) if ch == SCAN_PASSES - 1 else ()
            else:
                first = (kt * SCAN_PASSES + ch) * HEADS_PER_PASS
                mix_heads = range(first, first + HEADS_PER_PASS)
            a_r = jnp.broadcast_to(are_ref[:, c_lo + lo:c_lo + lo + width], (SUBLANES, width))
            a_i = jnp.broadcast_to(aim_ref[:, c_lo + lo:c_lo + lo + width], (SUBLANES, width))

            def scan_block(j, carry, kt=kt, hbuf=hbuf, ch=ch, lo=lo, a_r=a_r, a_i=a_i, c_lo=c_lo, mix_heads=mix_heads):
                for q in range(groups_per_block):
                    if groups_per_block == 1 and steps < npos:
                        t0, b0 = j * steps, 0
                    else:
                        t0, b0 = 0, (j * groups_per_block + q) * SUBLANES
                    h_r = hcre[pl.ds(b0, SUBLANES), c_lo + lo:c_lo + lo + width]
                    h_i = hcim[pl.ds(b0, SUBLANES), c_lo + lo:c_lo + lo + width]
                    for t in range(steps):
                        row = (t0 + t) * nb + b0
                        b_r = hbuf[pl.ds(row, SUBLANES), lo:lo + width]
                        b_i = hbuf[pl.ds(row, SUBLANES), KT_COLS + lo:KT_COLS + lo + width]
                        h_r, h_i = (a_r * h_r - a_i * h_i + b_r,
                                    a_r * h_i + a_i * h_r + b_i)
                        hbuf[pl.ds(row, SUBLANES), lo:lo + width] = h_r
                        hbuf[pl.ds(row, SUBLANES), KT_COLS + lo:KT_COLS + lo + width] = h_i
                    hcre[pl.ds(b0, SUBLANES), c_lo + lo:c_lo + lo + width] = h_r
                    hcim[pl.ds(b0, SUBLANES), c_lo + lo:c_lo + lo + width] = h_i
                if mix_heads:
                    m0 = j * mix_rows
                    head = lax.broadcasted_iota(jnp.int32, (1, GMLP_WIDTH), 1) // GMLP_HEAD_DIM
                    acc = sacc[pl.ds(m0, mix_rows), :]
                    for mh in mix_heads:
                        mixed = _dot(wk[mh, pl.ds(m0, mix_rows), :], vnb[...])
                        acc = acc + jnp.where(head == mh, mixed, 0.0)
                    sacc[pl.ds(m0, mix_rows), :] = acc
                if overlap and ch == 0:
                    r0 = j * SSM_CHUNK
                    if kt + 1 < SSM_KT:
                        bu_rows(kt + 1, r0)
                    if kt > 0:
                        y_rows(kt - 1, r0)
                return carry

            for j in range(SCAN_BLOCKS):
                scan_block(j, 0)

        if not overlap:
            chunk_loop(y_rows, kt)
    if overlap:
        chunk_loop(y_rows, SSM_KT - 1)

    def out_chunk(c, carry):
        r0 = pl.multiple_of(c * ROW_CHUNK, ROW_CHUNK)
        z = _gelu(ya[pl.ds(r0, ROW_CHUNK), :])
        o_a = z * _sigmoid(_dot(z.astype(BF16), wglu_ref[...]) + bglu_ref[...])
        ycat_ref[pl.ds(r0, ROW_CHUNK), 0:SSM_WIDTH] = o_a.astype(BF16)
        y_b = gu[pl.ds(r0, ROW_CHUNK), :] * sacc[pl.ds(r0, ROW_CHUNK), :]
        ycat_ref[pl.ds(r0, ROW_CHUNK), SSM_WIDTH:SSM_WIDTH + GMLP_WIDTH] = y_b.astype(BF16)

        ridx = lax.broadcasted_iota(jnp.int32, (ROW_CHUNK, 1), 0) + r0
        pos = base_pos + (ridx >> shift)
        for g, win in enumerate(POOL_WINDOWS):
            cl = g * POOL_GROUP_CH
            reach = (win - 1) * nb
            if reach < ROW_CHUNK:
                off = pl.multiple_of(past_rows - reach + r0, SUBLANES)
                ext = xp[pl.ds(off, ROW_CHUNK + reach), cl:cl + POOL_GROUP_CH]
                cur = ext[reach:, :]
                wsum, span = ext, 1
                while span < win:
                    n = wsum.shape[0] - span * nb
                    wsum = wsum[span * nb:, :] + wsum[:n, :]
                    span *= 2
            else:
                cur = xp[pl.ds(past_rows + r0, ROW_CHUNK), cl:cl + POOL_GROUP_CH]
                wsum = cur
                for j in range(1, win):
                    off = pl.multiple_of(past_rows - j * nb + r0, SUBLANES)
                    wsum = wsum + xp[pl.ds(off, ROW_CHUNK), cl:cl + POOL_GROUP_CH]
            cnt = jnp.minimum(pos + 1, win).astype(F32)
            d = wsum / cnt - cur
            y_c = _dot(d.astype(BF16), poolw_ref[g]) * pscale_ref[:, cl:cl + POOL_GROUP_CH]
            col = SSM_WIDTH + GMLP_WIDTH + cl
            ycat_ref[pl.ds(r0, ROW_CHUNK), col:col + POOL_GROUP_CH] = y_c.astype(BF16)
        return carry

    lax.fori_loop(0, n_chunks, out_chunk, 0)

    pool_out_ref[...] = xp[rows:rows + past_rows, :]
    if n_tiles > 1:
        xp[0:past_rows, :] = xp[rows:rows + past_rows, :]


def _branches_call(x, h0re, h0im, ppast, state_layer, p, layer, bs, *, nb, npos, start_pos, emit_vn):
    seq_major_in = x.ndim == 3
    n_rows = x.shape[0] * x.shape[1] if seq_major_in else x.shape[0]
    rows = nb * npos
    n_tiles = n_rows // rows
    past_rows = POOL_PAST * nb
    kern = functools.partial(_branches_kernel, nb=nb, npos=npos, n_tiles=n_tiles, start_pos=start_pos,
                             seq_major_in=seq_major_in, emit_vn=emit_vn)
    row_tile = lambda w: pl.BlockSpec((rows, w), lambda i: (i, 0))
    x_spec = pl.BlockSpec((nb, npos, D_MODEL), lambda i: (0, i, 0)) if seq_major_in else row_tile(D_MODEL)
    n_hbuf = SSM_KT if n_tiles > 1 else 1
    fixed = lambda r, w: pl.BlockSpec((r, w), lambda i: (0, 0))
    consts = (p['w_in_br'], p['bbc'], p['a_re'], p['a_im'], p['ccc'], p['d_skip'],
              p['w_glu'], p['b_glu'], p['gln_g'], p['gln_b'], p['w_s'], bs, p['pool_w'], p['pool_scale'])
    return pl.pallas_call(
        kern,
        grid=(n_tiles,),
        in_specs=[x_spec] + [_layer_const(s, state_layer) for s in (h0re, h0im, ppast)]
                 + [_layer_const(c, layer) for c in consts],
        out_specs=[row_tile(BRANCH_COLS - GMLP_WIDTH), fixed(nb, SSM_COLS), fixed(nb, SSM_COLS),
                   fixed(past_rows, POOL_WIDTH)] + ([row_tile(GMLP_WIDTH)] if emit_vn else []),
        out_shape=[jax.ShapeDtypeStruct((n_rows, BRANCH_COLS - GMLP_WIDTH), BF16),
                   jax.ShapeDtypeStruct((nb, SSM_COLS), F32),
                   jax.ShapeDtypeStruct((nb, SSM_COLS), F32),
                   jax.ShapeDtypeStruct((past_rows, POOL_WIDTH), F32)]
                  + ([jax.ShapeDtypeStruct((n_rows, GMLP_WIDTH), F32)] if emit_vn else []),
        scratch_shapes=[pltpu.VMEM((rows, 2 * KT_COLS), F32)] * n_hbuf
                       + [pltpu.VMEM((rows, SSM_WIDTH), F32),
                        pltpu.VMEM((rows, SSM_WIDTH), F32),
                        pltpu.VMEM((rows, GMLP_WIDTH), F32),
                        pltpu.VMEM((rows, GMLP_WIDTH), BF16),
                        pltpu.VMEM((rows, GMLP_WIDTH), F32),
                        pltpu.VMEM((rows + past_rows, POOL_WIDTH), F32),
                        pltpu.VMEM((GMLP_HEADS, rows, rows), BF16),
                        pltpu.VMEM((SSM_KT, MXU_DIM, 2 * KT_COLS), BF16),
                        pltpu.VMEM((SSM_KT, 2 * KT_COLS, MXU_DIM), BF16)],
        compiler_params=pltpu.CompilerParams(dimension_semantics=("arbitrary",),
                                             vmem_limit_bytes=VMEM_LIMIT_BYTES),
        name="branches",
    )(x, h0re, h0im, ppast, *consts)


def _merge_ffn_kernel(x_ref, ycat_ref, cpast_ref,
                      wg_ref, bg_ref, wbr_ref, wo_ref, g1_ref, b1_ref,
                      wup_ref, cw_ref, cb_ref, wdn_ref, g2_ref, b2_ref,
                      x2_ref, cout_ref, tailbuf, act, *, nb, rows, seq_major_in, seq_major_out):
    tail = (FFN_CONV - 1) * nb
    assert rows >= tail

    @pl.when(pl.program_id(0) == 0)
    def _init():
        tailbuf[...] = cpast_ref[...]

    x = _load_rows_t_major(x_ref, 0, rows, nb, seq_major_in)
    xb = x.astype(BF16)
    yc = ycat_ref[...]
    bounds = (0, SSM_WIDTH, SSM_WIDTH + GMLP_WIDTH, BRANCH_COLS - GMLP_WIDTH)
    merged = None
    for i in range(N_BRANCH):
        gate = _sigmoid(_dot(xb, wg_ref[:, i * D_MODEL:(i + 1) * D_MODEL])
                        + bg_ref[:, i * D_MODEL:(i + 1) * D_MODEL])
        br = _dot(yc[:, bounds[i]:bounds[i + 1]], wbr_ref[bounds[i]:bounds[i + 1], :])
        merged = gate * br if merged is None else merged + gate * br
    mix = _dot(merged.astype(BF16), wo_ref[...])
    x1 = _layer_norm(ALPHA * x + mix, g1_ref[...], b1_ref[...])

    x1b = x1.astype(BF16)
    for cc in range(D_FF // FF_COL_CHUNK):
        conv = []
        for part in range(2):
            lo = part * D_FF + cc * FF_COL_CHUNK
            cols = slice(lo, lo + FF_COL_CHUNK)
            h = _dot(x1b, wup_ref[:, cols])
            he = jnp.concatenate([tailbuf[:, cols], h], axis=0)
            tailbuf[:, cols] = h[rows - tail:rows, :]
            y = cb_ref[:, cols] + he[0:rows, :] * cw_ref[0:1, cols]
            y = y + he[nb:nb + rows, :] * cw_ref[1:2, cols]
            y = y + h * cw_ref[2:3, cols]
            conv.append(y)
        a = _gelu(conv[1]) * conv[0]
        act[:, cc * FF_COL_CHUNK:(cc + 1) * FF_COL_CHUNK] = a.astype(BF16)
    ff = _dot(act[...], wdn_ref[...])
    x2 = _layer_norm(ALPHA * x1 + ff, g2_ref[...], b2_ref[...])
    if seq_major_out:
        x2_ref[...] = jnp.swapaxes(x2.reshape(rows // nb, nb, D_MODEL), 0, 1)
    else:
        x2_ref[...] = x2
    cout_ref[...] = tailbuf[...]


def _merge_ffn_call(x, ycat, cpast, state_layer, p, layer, *, nb, rows, seq_major_out):
    seq_major_in = x.ndim == 3
    n_rows = ycat.shape[0]
    tail = (FFN_CONV - 1) * nb
    kern = functools.partial(_merge_ffn_kernel, nb=nb, rows=rows,
                             seq_major_in=seq_major_in, seq_major_out=seq_major_out)
    consts = (p['w_in_gate'], p['b_gate'], p['w_br'], p['w_o'], p['ln1_g'], p['ln1_b'],
              p['w_up'], p['conv_w'], p['conv_b'], p['w_down'], p['ln2_g'], p['ln2_b'])
    t_major_spec = pl.BlockSpec((rows, D_MODEL), lambda i: (i, 0))
    seq_major_spec = pl.BlockSpec((nb, rows // nb, D_MODEL), lambda i: (0, i, 0))
    return pl.pallas_call(
        kern,
        grid=(n_rows // rows,),
        in_specs=[seq_major_spec if seq_major_in else t_major_spec,
                  pl.BlockSpec((rows, BRANCH_COLS - GMLP_WIDTH), lambda i: (i, 0)),
                  _layer_const(cpast, state_layer)]
                 + [_layer_const(c, layer) for c in consts],
        out_specs=[seq_major_spec if seq_major_out else t_major_spec,
                   pl.BlockSpec((tail, F2), lambda i: (0, 0))],
        out_shape=[jax.ShapeDtypeStruct((nb, n_rows // nb, D_MODEL) if seq_major_out else (n_rows, D_MODEL), F32),
                   jax.ShapeDtypeStruct((tail, F2), F32)],
        scratch_shapes=[pltpu.VMEM((tail, F2), F32),
                        pltpu.VMEM((rows, D_FF), BF16)],
        compiler_params=pltpu.CompilerParams(dimension_semantics=("arbitrary",),
                                             vmem_limit_bytes=VMEM_LIMIT_BYTES),
        name="merge_convffn",
    )(x, ycat, cpast, *consts)


def _ssm_discretise(a_re, a_im, log_step, b_re, b_im):
    dt = jnp.exp(log_step)[..., None]
    mag = jnp.exp(a_re * dt)
    ab_re = mag * jnp.cos(a_im * dt)
    ab_im = mag * jnp.sin(a_im * dt)
    den = jnp.square(a_re) + jnp.square(a_im)
    nr = ab_re - 1.0
    k_re = (nr * a_re + ab_im * a_im) / den
    k_im = (ab_im * a_re - nr * a_im) / den
    bb_re = k_re[..., None] * b_re - k_im[..., None] * b_im
    bb_im = k_re[..., None] * b_im + k_im[..., None] * b_re
    return ab_re, ab_im, bb_re, bb_im


def _compact_in(m):
    gpt = SSM_GROUPS // SSM_KT
    return jnp.transpose(m, (0, 1, 3, 2)).reshape(-1, SSM_KT, gpt * SSM_GROUP_CH, SSM_STATE)


def _compact_out(m):
    gpt = SSM_GROUPS // SSM_KT
    m = m.reshape(-1, SSM_KT, gpt, SSM_GROUP_CH, SSM_STATE)
    return jnp.transpose(m, (0, 1, 4, 2, 3)).reshape(-1, SSM_KT, SSM_STATE, gpt * SSM_GROUP_CH)


def _gmlp_bias_rows(b_s, cl, nb):
    b = jnp.transpose(b_s[:, :, :cl], (0, 2, 1))
    return jnp.repeat(jnp.repeat(b, GMLP_HEAD_DIM, axis=2), nb, axis=1)


def _t_major(a):
    return jnp.transpose(a, (1, 0, 2)).reshape(a.shape[0] * a.shape[1], a.shape[2])


def _seq_major(a, n_seq):
    return jnp.transpose(a.reshape(a.shape[0] // n_seq, n_seq, a.shape[1]), (1, 0, 2))


def _group_layer(x, st_re, st_im, st_pool, st_conv, state_layer, p, layer, bs, *, nb, npos, start_pos, ffn_rows,
                 seq_major_out=False, emit_vn=False):
    ycat, h_re, h_im, pool_new, *vn = _branches_call(x, st_re, st_im, st_pool, state_layer, p, layer, bs,
                                                      nb=nb, npos=npos, start_pos=start_pos, emit_vn=emit_vn)
    x2, conv_new = _merge_ffn_call(x, ycat, st_conv, state_layer, p, layer, nb=nb, rows=ffn_rows,
                                   seq_major_out=seq_major_out)
    return (x2, h_re, h_im, pool_new, conv_new, *vn)


def kernel(x_prompt, x_sample, state_ssm_re, state_ssm_im, state_pool, state_ffn_conv, w_in, b_gate, ssm_a_re, ssm_a_im, ssm_log_step, ssm_b_re, ssm_b_im, ssm_c_re, ssm_c_im, ssm_d, ssm_w_glu, ssm_b_glu, gmlp_ln_g, gmlp_ln_b, gmlp_w_s, gmlp_b_s, pool_w, pool_scale, w_br_ssm, w_br_gmlp, w_br_pool, w_o, ln1_g, ln1_b, ffn_w_up, ffn_conv_w, ffn_conv_b, ffn_w_down, ln2_g, ln2_b):
    n_p, t_p, _ = x_prompt.shape
    n_s, t_s, _ = x_sample.shape
    hp = x_prompt
    hs = _t_major(x_sample)
    npos_p = TILE_ROWS // n_p
    npos_s = TILE_ROWS // n_s
    assert npos_p == GMLP_CHUNK and npos_s == t_s and t_p % npos_p == 0

    row = lambda v: v.reshape(DEPTH, 1, -1)
    ab_re, ab_im, bb_re, bb_im = _ssm_discretise(ssm_a_re, ssm_a_im, ssm_log_step, ssm_b_re, ssm_b_im)
    p = dict(
        w_in_br=w_in[:, :, :BRANCH_COLS].astype(BF16),
        w_in_gate=w_in[:, :, BRANCH_COLS:].astype(BF16),
        b_gate=row(b_gate),
        bbc=jnp.stack([_compact_in(bb_re), _compact_in(bb_im)], axis=2).astype(BF16),
        a_re=row(ab_re), a_im=row(ab_im),
        ccc=jnp.stack([_compact_out(ssm_c_re), -_compact_out(ssm_c_im)], axis=2).astype(BF16),
        d_skip=row(ssm_d),
        w_glu=ssm_w_glu.astype(BF16), b_glu=row(ssm_b_glu),
        gln_g=row(gmlp_ln_g), gln_b=row(gmlp_ln_b),
        w_s=gmlp_w_s, pool_w=pool_w.astype(BF16), pool_scale=row(pool_scale),
        w_br=jnp.concatenate([w_br_ssm, w_br_gmlp, w_br_pool], axis=1).astype(BF16),
        w_o=w_o.astype(BF16), ln1_g=row(ln1_g), ln1_b=row(ln1_b),
        w_up=ffn_w_up.astype(BF16), conv_w=ffn_conv_w, conv_b=row(ffn_conv_b),
        w_down=ffn_w_down.astype(BF16), ln2_g=row(ln2_g), ln2_b=row(ln2_b),
    )
    bs_p = _gmlp_bias_rows(gmlp_b_s, npos_p, n_p)
    bs_s = _gmlp_bias_rows(gmlp_b_s, npos_s, n_s)
    zeros_p = (jnp.zeros((1, n_p, SSM_COLS), F32), jnp.zeros((1, n_p, SSM_COLS), F32),
               jnp.zeros((1, POOL_PAST * n_p, POOL_WIDTH), F32), jnp.zeros((1, (FFN_CONV - 1) * n_p, F2), F32))
    ssm_re_s = state_ssm_re.reshape(DEPTH, n_s, SSM_COLS)
    ssm_im_s = state_ssm_im.reshape(DEPTH, n_s, SSM_COLS)

    pool_t = jnp.transpose(state_pool, (0, 2, 1, 3)).reshape(DEPTH, POOL_PAST * n_s, POOL_WIDTH)
    conv_t = jnp.transpose(state_ffn_conv, (0, 2, 1, 3)).reshape(DEPTH, (FFN_CONV - 1) * n_s, F2)

    outs = [[] for _ in range(9)]
    for l in range(DEPTH):
        hp, a_re_n, a_im_n, pool_n, conv_n = _group_layer(
            hp, *zeros_p, 0, p, l, bs_p, nb=n_p, npos=npos_p, start_pos=0, ffn_rows=2 * ROW_CHUNK,
            seq_major_out=(l == DEPTH - 1))
        hs, b_re_n, b_im_n, pool_m, conv_m, v_n = _group_layer(
            hs, ssm_re_s, ssm_im_s, pool_t, conv_t, l, p, l, bs_s, nb=n_s, npos=npos_s, start_pos=PAST_LEN, ffn_rows=ROW_CHUNK, emit_vn=True)
        for o, v in zip(outs, (a_re_n, a_im_n, pool_n, conv_n, b_re_n, b_im_n, v_n, pool_m, conv_m)):
            o.append(v)

    p_re, p_im, p_pool, p_conv, s_re, s_im, s_v, s_pool, s_conv = (jnp.stack(o) for o in outs)
    ssm = lambda a, n_seq: a.reshape(DEPTH, n_seq, SSM_GROUPS, SSM_STATE)
    per_seq = lambda a, n_seq: jnp.transpose(a.reshape(DEPTH, -1, n_seq, a.shape[-1]), (0, 2, 1, 3))
    return (hp, _seq_major(hs, n_s),
            ssm(p_re, n_p), ssm(p_im, n_p), per_seq(p_pool, n_p), per_seq(p_conv, n_p),
            ssm(s_re, n_s), ssm(s_im, n_s), per_seq(s_v, n_s), per_seq(s_pool, n_s), per_seq(s_conv, n_s))
```

```python
import functools
import math

import jax
import jax.numpy as jnp
from jax import lax
from jax.experimental import pallas as pl
from jax.experimental.pallas import tpu as pltpu

F32 = jnp.float32
BF16 = jnp.bfloat16

D_MODEL = 1024
DEPTH = 2
PAST_LEN = 16384
SSM_GROUP_CH = 16
SSM_WIDTH = 512
SSM_GROUPS = 32
SSM_STATE = 64
SSM_COLS = SSM_GROUPS * SSM_STATE
GMLP_HEADS = 4
GMLP_CHUNK = 128
GMLP_WIDTH = 256
GMLP_HEAD_DIM = 64
POOL_WINDOWS = (2, 4, 8, 16)
POOL_WIDTH = 512
POOL_GROUP_CH = 128
POOL_PAST = 15
N_BRANCH = 3
BRANCH_COLS = SSM_WIDTH + 2 * GMLP_WIDTH + POOL_WIDTH
D_FF = 2816
F2 = 2 * D_FF
FFN_CONV = 3
LN_EPS = 1e-5
ALPHA = (2 * DEPTH) ** 0.25

SUBLANES = 8
MXU_DIM = 256
VMEM_LIMIT_BYTES = 60 * 1024 * 1024

TILE_ROWS = 1024
ROW_CHUNK = 256
PROJ_CHUNK = 512
SSM_CHUNK = 512
SCAN_PASSES = 2
SCAN_BLOCKS = 2
SSM_KT = SSM_WIDTH // MXU_DIM
KT_COLS = SSM_COLS // SSM_KT
FF_COL_CHUNK = MXU_DIM
HEADS_PER_PASS = GMLP_HEADS // (SSM_KT * SCAN_PASSES)
assert HEADS_PER_PASS * SSM_KT * SCAN_PASSES == GMLP_HEADS


def _gelu(x):
    c = math.sqrt(2.0 / math.pi)
    return 0.5 * x * (1.0 + jnp.tanh(c * (x + 0.044715 * (x * x * x))))


def _sigmoid(x):
    return 1.0 / (1.0 + jnp.exp(-x))


def _layer_norm(x, g, b):
    mu = jnp.mean(x, axis=-1, keepdims=True)
    xc = x - mu
    var = jnp.mean(xc * xc, axis=-1, keepdims=True)
    return xc * lax.rsqrt(var + LN_EPS) * g + b


def _dot(a, b):
    return jnp.dot(a, b, preferred_element_type=F32)


def _load_rows_t_major(x_ref, r0, n, nb, seq_major):
    if not seq_major:
        return x_ref[pl.ds(r0, n), :]
    p0 = r0 // nb if isinstance(r0, int) else pl.multiple_of(r0 >> int(math.log2(nb)), SUBLANES)
    blk = x_ref[:, pl.ds(p0, n // nb), :]
    return jnp.swapaxes(blk, 0, 1).reshape(n, blk.shape[-1])


def _layer_const(a, layer):
    nd = a.ndim - 1
    return pl.BlockSpec((None,) + a.shape[1:], lambda i: (layer,) + (0,) * nd,
                        pipeline_mode=pl.Buffered(1))


def _expand_mix(ws_ref, wk, rows, shift, nb):
    cl = GMLP_CHUNK
    e = (lax.broadcasted_iota(jnp.int32, (rows, cl), 0) >> shift
         == lax.broadcasted_iota(jnp.int32, (rows, cl), 1)).astype(BF16)
    tril = (lax.broadcasted_iota(jnp.int32, (cl, cl), 0)
            >= lax.broadcasted_iota(jnp.int32, (cl, cl), 1))
    for h in range(GMLP_HEADS):
        w = jnp.where(tril, ws_ref[h], 0.0).astype(BF16)
        rows_w = _dot(e, w).astype(BF16)
        for c0 in range(0, rows, MXU_DIM):
            et = (lax.broadcasted_iota(jnp.int32, (cl, MXU_DIM), 0)
                  == (lax.broadcasted_iota(jnp.int32, (cl, MXU_DIM), 1) + c0) >> shift).astype(BF16)
            full = _dot(rows_w, et)
            same_seq = ((lax.broadcasted_iota(jnp.int32, (rows, MXU_DIM), 0) & (nb - 1))
                        == ((lax.broadcasted_iota(jnp.int32, (rows, MXU_DIM), 1) + c0) & (nb - 1)))
            wk[h, :, c0:c0 + MXU_DIM] = jnp.where(same_seq, full, 0.0).astype(BF16)


def _expand_ssm(bbc_ref, ccc_ref, bb, cc):
    n_in, n_st = KT_COLS // SSM_STATE * SSM_GROUP_CH, KT_COLS
    iota = lambda shape, d: lax.broadcasted_iota(jnp.int32, shape, d)
    log_n, log_c = int(math.log2(SSM_STATE)), int(math.log2(SSM_GROUP_CH))
    spread = ((iota((SSM_STATE, n_st), 1) & (SSM_STATE - 1)) == iota((SSM_STATE, n_st), 0)).astype(BF16)
    same_in = (iota((n_in, n_st), 0) >> log_c) == (iota((n_in, n_st), 1) >> log_n)
    gather = ((iota((n_st, SSM_STATE), 0) & (SSM_STATE - 1)) == iota((n_st, SSM_STATE), 1)).astype(BF16)
    same_out = (iota((n_st, n_in), 0) >> log_n) == (iota((n_st, n_in), 1) >> log_c)
    for kt in range(SSM_KT):
        for part in range(2):
            full = _dot(bbc_ref[kt, part], spread)
            bb[kt, :, part * n_st:(part + 1) * n_st] = jnp.where(same_in, full, 0.0).astype(BF16)
            full = _dot(gather, ccc_ref[kt, part])
            cc[kt, part * n_st:(part + 1) * n_st, :] = jnp.where(same_out, full, 0.0).astype(BF16)


def _branches_kernel(x_ref, h0re_ref, h0im_ref, ppast_ref,
                     w_in_ref, bbc_ref, are_ref, aim_ref, ccc_ref, dsk_ref,
                     wglu_ref, bglu_ref, lng_ref, lnb_ref, ws_ref, bs_ref,
                     poolw_ref, pscale_ref,
                     ycat_ref, hcre, hcim, pool_out_ref,
                     *scratch, nb, npos, n_tiles, start_pos, seq_major_in, emit_vn):
    vn_out_ref = scratch[0] if emit_vn else None
    *hbufs, us, ya, gu, vnb, sacc, xp, wk, bb_ref, cc_ref = scratch[1:] if emit_vn else scratch
    rows = nb * npos
    n_chunks = rows // ROW_CHUNK
    past_rows = POOL_PAST * nb
    tile = pl.program_id(0)
    shift = int(math.log2(nb))
    slabs_per_block = rows // SUBLANES // SCAN_BLOCKS
    steps = min(npos, slabs_per_block)
    groups_per_block = slabs_per_block // steps
    assert groups_per_block == 1 or steps == npos
    mix_rows = rows // SCAN_BLOCKS

    @pl.when(tile == 0)
    def _init():
        hcre[...] = h0re_ref[...]
        hcim[...] = h0im_ref[...]
        xp[0:past_rows, :] = ppast_ref[...]
        _expand_mix(ws_ref, wk, rows, shift, nb)
        _expand_ssm(bbc_ref, ccc_ref, bb_ref, cc_ref)

    base_pos = start_pos + tile * npos

    def proj_chunk(c, carry):
        r0 = pl.multiple_of(c * PROJ_CHUNK, PROJ_CHUNK)
        xb = _load_rows_t_major(x_ref, r0, PROJ_CHUNK, nb, seq_major_in).astype(BF16)
        pa = _dot(xb, w_in_ref[...])
        us[pl.ds(r0, PROJ_CHUNK), :] = pa[:, 0:SSM_WIDTH]
        gu[pl.ds(r0, PROJ_CHUNK), :] = _gelu(pa[:, SSM_WIDTH:SSM_WIDTH + GMLP_WIDTH])
        v = _gelu(pa[:, SSM_WIDTH + GMLP_WIDTH:SSM_WIDTH + 2 * GMLP_WIDTH])
        v = _layer_norm(v, lng_ref[...], lnb_ref[...])
        if emit_vn:
            vn_out_ref[pl.ds(r0, PROJ_CHUNK), :] = v
        vnb[pl.ds(r0, PROJ_CHUNK), :] = v.astype(BF16)
        sacc[pl.ds(r0, PROJ_CHUNK), :] = bs_ref[pl.ds(r0, PROJ_CHUNK), :]
        xp[pl.ds(past_rows + r0, PROJ_CHUNK), :] = pa[:, SSM_WIDTH + 2 * GMLP_WIDTH:BRANCH_COLS]
        return carry

    lax.fori_loop(0, rows // PROJ_CHUNK, proj_chunk, 0)

    n_hbuf = len(hbufs)
    overlap = n_hbuf > 1

    def bu_rows(kt, r0):
        ub = us[pl.ds(r0, SSM_CHUNK), kt * MXU_DIM:(kt + 1) * MXU_DIM].astype(BF16)
        hbufs[kt % n_hbuf][pl.ds(r0, SSM_CHUNK), :] = _dot(ub, bb_ref[kt])

    def y_rows(kt, r0):
        hb = hbufs[kt % n_hbuf][pl.ds(r0, SSM_CHUNK), :].astype(BF16)
        u = us[pl.ds(r0, SSM_CHUNK), kt * MXU_DIM:(kt + 1) * MXU_DIM]
        y = _dot(hb, cc_ref[kt]) + dsk_ref[:, kt * MXU_DIM:(kt + 1) * MXU_DIM] * u
        ya[pl.ds(r0, SSM_CHUNK), kt * MXU_DIM:(kt + 1) * MXU_DIM] = y

    def chunk_loop(fn, kt):
        def body(c, carry):
            fn(kt, pl.multiple_of(c * SSM_CHUNK, SSM_CHUNK))
            return carry
        lax.fori_loop(0, rows // SSM_CHUNK, body, 0)

    if overlap:
        assert rows // SSM_CHUNK == SCAN_BLOCKS
        chunk_loop(bu_rows, 0)
    for kt in range(SSM_KT):
        c_lo = kt * KT_COLS
        hbuf = hbufs[kt % n_hbuf]
        if not overlap:
            chunk_loop(bu_rows, kt)

        width = KT_COLS // SCAN_PASSES
        for ch in range(SCAN_PASSES):
            lo = ch * width
            if overlap:
                heads_kt = GMLP_HEADS // SSM_KT
                mix_heads = range(kt * heads_kt, (kt + 1) * heads_kt) if ch == SCAN_PASSES - 1 else ()
            else:
                first = (kt * SCAN_PASSES + ch) * HEADS_PER_PASS
                mix_heads = range(first, first + HEADS_PER_PASS)
            a_r = jnp.broadcast_to(are_ref[:, c_lo + lo:c_lo + lo + width], (SUBLANES, width))
            a_i = jnp.broadcast_to(aim_ref[:, c_lo + lo:c_lo + lo + width], (SUBLANES, width))

            def scan_block(j, carry, kt=kt, hbuf=hbuf, ch=ch, lo=lo, a_r=a_r, a_i=a_i, c_lo=c_lo, mix_heads=mix_heads):
                for q in range(groups_per_block):
                    if groups_per_block == 1 and steps < npos:
                        t0, b0 = j * steps, 0
                    else:
                        t0, b0 = 0, (j * groups_per_block + q) * SUBLANES
                    h_r = hcre[pl.ds(b0, SUBLANES), c_lo + lo:c_lo + lo + width]
                    h_i = hcim[pl.ds(b0, SUBLANES), c_lo + lo:c_lo + lo + width]
                    for t in range(steps):
                        row = (t0 + t) * nb + b0
                        b_r = hbuf[pl.ds(row, SUBLANES), lo:lo + width]
                        b_i = hbuf[pl.ds(row, SUBLANES), KT_COLS + lo:KT_COLS + lo + width]
                        h_r, h_i = (a_r * h_r - a_i * h_i + b_r,
                                    a_r * h_i + a_i * h_r + b_i)
                        hbuf[pl.ds(row, SUBLANES), lo:lo + width] = h_r
                        hbuf[pl.ds(row, SUBLANES), KT_COLS + lo:KT_COLS + lo + width] = h_i
                    hcre[pl.ds(b0, SUBLANES), c_lo + lo:c_lo + lo + width] = h_r
                    hcim[pl.ds(b0, SUBLANES), c_lo + lo:c_lo + lo + width] = h_i
                if mix_heads:
                    m0 = j * mix_rows
                    head = lax.broadcasted_iota(jnp.int32, (1, GMLP_WIDTH), 1) // GMLP_HEAD_DIM
                    acc = sacc[pl.ds(m0, mix_rows), :]
                    for mh in mix_heads:
                        k_hi = (j + 1) * mix_rows
                        mixed = _dot(wk[mh, pl.ds(m0, mix_rows), 0:k_hi], vnb[0:k_hi, :])
                        acc = acc + jnp.where(head == mh, mixed, 0.0)
                    sacc[pl.ds(m0, mix_rows), :] = acc
                if overlap and ch == 0:
                    r0 = j * SSM_CHUNK
                    if kt + 1 < SSM_KT:
                        bu_rows(kt + 1, r0)
                    if kt > 0:
                        y_rows(kt - 1, r0)
                return carry

            for j in range(SCAN_BLOCKS):
                scan_block(j, 0)

        if not overlap:
            chunk_loop(y_rows, kt)
    if overlap:
        chunk_loop(y_rows, SSM_KT - 1)

    def out_chunk(c, carry):
        r0 = pl.multiple_of(c * ROW_CHUNK, ROW_CHUNK)
        z = _gelu(ya[pl.ds(r0, ROW_CHUNK), :])
        o_a = z * _sigmoid(_dot(z.astype(BF16), wglu_ref[...]) + bglu_ref[...])
        ycat_ref[pl.ds(r0, ROW_CHUNK), 0:SSM_WIDTH] = o_a.astype(BF16)
        y_b = gu[pl.ds(r0, ROW_CHUNK), :] * sacc[pl.ds(r0, ROW_CHUNK), :]
        ycat_ref[pl.ds(r0, ROW_CHUNK), SSM_WIDTH:SSM_WIDTH + GMLP_WIDTH] = y_b.astype(BF16)

        ridx = lax.broadcasted_iota(jnp.int32, (ROW_CHUNK, 1), 0) + r0
        pos = base_pos + (ridx >> shift)
        for g, win in enumerate(POOL_WINDOWS):
            cl = g * POOL_GROUP_CH
            reach = (win - 1) * nb
            if reach < ROW_CHUNK:
                off = pl.multiple_of(past_rows - reach + r0, SUBLANES)
                ext = xp[pl.ds(off, ROW_CHUNK + reach), cl:cl + POOL_GROUP_CH]
                cur = ext[reach:, :]
                wsum, span = ext, 1
                while span < win:
                    n = wsum.shape[0] - span * nb
                    wsum = wsum[span * nb:, :] + wsum[:n, :]
                    span *= 2
            else:
                cur = xp[pl.ds(past_rows + r0, ROW_CHUNK), cl:cl + POOL_GROUP_CH]
                wsum = cur
                for j in range(1, win):
                    off = pl.multiple_of(past_rows - j * nb + r0, SUBLANES)
                    wsum = wsum + xp[pl.ds(off, ROW_CHUNK), cl:cl + POOL_GROUP_CH]
            cnt = jnp.minimum(pos + 1, win).astype(F32)
            d = wsum / cnt - cur
            y_c = _dot(d.astype(BF16), poolw_ref[g]) * pscale_ref[:, cl:cl + POOL_GROUP_CH]
            col = SSM_WIDTH + GMLP_WIDTH + cl
            ycat_ref[pl.ds(r0, ROW_CHUNK), col:col + POOL_GROUP_CH] = y_c.astype(BF16)
        return carry

    lax.fori_loop(0, n_chunks, out_chunk, 0)

    pool_out_ref[...] = xp[rows:rows + past_rows, :]
    if n_tiles > 1:
        xp[0:past_rows, :] = xp[rows:rows + past_rows, :]


def _branches_call(x, h0re, h0im, ppast, state_layer, p, layer, bs, *, nb, npos, start_pos, emit_vn):
    seq_major_in = x.ndim == 3
    n_rows = x.shape[0] * x.shape[1] if seq_major_in else x.shape[0]
    rows = nb * npos
    n_tiles = n_rows // rows
    past_rows = POOL_PAST * nb
    kern = functools.partial(_branches_kernel, nb=nb, npos=npos, n_tiles=n_tiles, start_pos=start_pos,
                             seq_major_in=seq_major_in, emit_vn=emit_vn)
    row_tile = lambda w: pl.BlockSpec((rows, w), lambda i: (i, 0))
    x_spec = pl.BlockSpec((nb, npos, D_MODEL), lambda i: (0, i, 0)) if seq_major_in else row_tile(D_MODEL)
    n_hbuf = SSM_KT if n_tiles > 1 else 1
    fixed = lambda r, w: pl.BlockSpec((r, w), lambda i: (0, 0))
    consts = (p['w_in_br'], p['bbc'], p['a_re'], p['a_im'], p['ccc'], p['d_skip'],
              p['w_glu'], p['b_glu'], p['gln_g'], p['gln_b'], p['w_s'], bs, p['pool_w'], p['pool_scale'])
    return pl.pallas_call(
        kern,
        grid=(n_tiles,),
        in_specs=[x_spec] + [_layer_const(s, state_layer) for s in (h0re, h0im, ppast)]
                 + [_layer_const(c, layer) for c in consts],
        out_specs=[row_tile(BRANCH_COLS - GMLP_WIDTH), fixed(nb, SSM_COLS), fixed(nb, SSM_COLS),
                   fixed(past_rows, POOL_WIDTH)] + ([row_tile(GMLP_WIDTH)] if emit_vn else []),
        out_shape=[jax.ShapeDtypeStruct((n_rows, BRANCH_COLS - GMLP_WIDTH), BF16),
                   jax.ShapeDtypeStruct((nb, SSM_COLS), F32),
                   jax.ShapeDtypeStruct((nb, SSM_COLS), F32),
                   jax.ShapeDtypeStruct((past_rows, POOL_WIDTH), F32)]
                  + ([jax.ShapeDtypeStruct((n_rows, GMLP_WIDTH), F32)] if emit_vn else []),
        scratch_shapes=[pltpu.VMEM((rows, 2 * KT_COLS), F32)] * n_hbuf
                       + [pltpu.VMEM((rows, SSM_WIDTH), F32),
                        pltpu.VMEM((rows, SSM_WIDTH), F32),
                        pltpu.VMEM((rows, GMLP_WIDTH), F32),
                        pltpu.VMEM((rows, GMLP_WIDTH), BF16),
                        pltpu.VMEM((rows, GMLP_WIDTH), F32),
                        pltpu.VMEM((rows + past_rows, POOL_WIDTH), F32),
                        pltpu.VMEM((GMLP_HEADS, rows, rows), BF16),
                        pltpu.VMEM((SSM_KT, MXU_DIM, 2 * KT_COLS), BF16),
                        pltpu.VMEM((SSM_KT, 2 * KT_COLS, MXU_DIM), BF16)],
        compiler_params=pltpu.CompilerParams(dimension_semantics=("arbitrary",),
                                             vmem_limit_bytes=VMEM_LIMIT_BYTES),
        name="branches",
    )(x, h0re, h0im, ppast, *consts)


def _merge_ffn_kernel(x_ref, ycat_ref, cpast_ref,
                      wg_ref, bg_ref, wbr_ref, wo_ref, g1_ref, b1_ref,
                      wup_ref, cw_ref, cb_ref, wdn_ref, g2_ref, b2_ref,
                      x2_ref, cout_ref, tailbuf, act, *, nb, rows, seq_major_in, seq_major_out):
    tail = (FFN_CONV - 1) * nb
    assert rows >= tail

    @pl.when(pl.program_id(0) == 0)
    def _init():
        tailbuf[...] = cpast_ref[...]

    x = _load_rows_t_major(x_ref, 0, rows, nb, seq_major_in)
    xb = x.astype(BF16)
    yc = ycat_ref[...]
    bounds = (0, SSM_WIDTH, SSM_WIDTH + GMLP_WIDTH, BRANCH_COLS - GMLP_WIDTH)
    merged = None
    for i in range(N_BRANCH):
        gate = _sigmoid(_dot(xb, wg_ref[:, i * D_MODEL:(i + 1) * D_MODEL])
                        + bg_ref[:, i * D_MODEL:(i + 1) * D_MODEL])
        br = _dot(yc[:, bounds[i]:bounds[i + 1]], wbr_ref[bounds[i]:bounds[i + 1], :])
        merged = gate * br if merged is None else merged + gate * br
    mix = _dot(merged.astype(BF16), wo_ref[...])
    x1 = _layer_norm(ALPHA * x + mix, g1_ref[...], b1_ref[...])

    x1b = x1.astype(BF16)
    for cc in range(D_FF // FF_COL_CHUNK):
        conv = []
        for part in range(2):
            lo = part * D_FF + cc * FF_COL_CHUNK
            cols = slice(lo, lo + FF_COL_CHUNK)
            h = _dot(x1b, wup_ref[:, cols])
            he = jnp.concatenate([tailbuf[:, cols], h], axis=0)
            tailbuf[:, cols] = h[rows - tail:rows, :]
            y = cb_ref[:, cols] + he[0:rows, :] * cw_ref[0:1, cols]
            y = y + he[nb:nb + rows, :] * cw_ref[1:2, cols]
            y = y + h * cw_ref[2:3, cols]
            conv.append(y)
        a = _gelu(conv[1]) * conv[0]
        act[:, cc * FF_COL_CHUNK:(cc + 1) * FF_COL_CHUNK] = a.astype(BF16)
    ff = _dot(act[...], wdn_ref[...])
    x2 = _layer_norm(ALPHA * x1 + ff, g2_ref[...], b2_ref[...])
    if seq_major_out:
        x2_ref[...] = jnp.swapaxes(x2.reshape(rows // nb, nb, D_MODEL), 0, 1)
    else:
        x2_ref[...] = x2
    cout_ref[...] = tailbuf[...]


def _merge_ffn_call(x, ycat, cpast, state_layer, p, layer, *, nb, rows, seq_major_out):
    seq_major_in = x.ndim == 3
    n_rows = ycat.shape[0]
    tail = (FFN_CONV - 1) * nb
    kern = functools.partial(_merge_ffn_kernel, nb=nb, rows=rows,
                             seq_major_in=seq_major_in, seq_major_out=seq_major_out)
    consts = (p['w_in_gate'], p['b_gate'], p['w_br'], p['w_o'], p['ln1_g'], p['ln1_b'],
              p['w_up'], p['conv_w'], p['conv_b'], p['w_down'], p['ln2_g'], p['ln2_b'])
    t_major_spec = pl.BlockSpec((rows, D_MODEL), lambda i: (i, 0))
    seq_major_spec = pl.BlockSpec((nb, rows // nb, D_MODEL), lambda i: (0, i, 0))
    return pl.pallas_call(
        kern,
        grid=(n_rows // rows,),
        in_specs=[seq_major_spec if seq_major_in else t_major_spec,
                  pl.BlockSpec((rows, BRANCH_COLS - GMLP_WIDTH), lambda i: (i, 0)),
                  _layer_const(cpast, state_layer)]
                 + [_layer_const(c, layer) for c in consts],
        out_specs=[seq_major_spec if seq_major_out else t_major_spec,
                   pl.BlockSpec((tail, F2), lambda i: (0, 0))],
        out_shape=[jax.ShapeDtypeStruct((nb, n_rows // nb, D_MODEL) if seq_major_out else (n_rows, D_MODEL), F32),
                   jax.ShapeDtypeStruct((tail, F2), F32)],
        scratch_shapes=[pltpu.VMEM((tail, F2), F32),
                        pltpu.VMEM((rows, D_FF), BF16)],
        compiler_params=pltpu.CompilerParams(dimension_semantics=("arbitrary",),
                                             vmem_limit_bytes=VMEM_LIMIT_BYTES),
        name="merge_convffn",
    )(x, ycat, cpast, *consts)


def _ssm_discretise(a_re, a_im, log_step, b_re, b_im):
    dt = jnp.exp(log_step)[..., None]
    mag = jnp.exp(a_re * dt)
    ab_re = mag * jnp.cos(a_im * dt)
    ab_im = mag * jnp.sin(a_im * dt)
    den = jnp.square(a_re) + jnp.square(a_im)
    nr = ab_re - 1.0
    k_re = (nr * a_re + ab_im * a_im) / den
    k_im = (ab_im * a_re - nr * a_im) / den
    bb_re = k_re[..., None] * b_re - k_im[..., None] * b_im
    bb_im = k_re[..., None] * b_im + k_im[..., None] * b_re
    return ab_re, ab_im, bb_re, bb_im


def _compact_in(m):
    gpt = SSM_GROUPS // SSM_KT
    return jnp.transpose(m, (0, 1, 3, 2)).reshape(-1, SSM_KT, gpt * SSM_GROUP_CH, SSM_STATE)


def _compact_out(m):
    gpt = SSM_GROUPS // SSM_KT
    m = m.reshape(-1, SSM_KT, gpt, SSM_GROUP_CH, SSM_STATE)
    return jnp.transpose(m, (0, 1, 4, 2, 3)).reshape(-1, SSM_KT, SSM_STATE, gpt * SSM_GROUP_CH)


def _gmlp_bias_rows(b_s, cl, nb):
    b = jnp.transpose(b_s[:, :, :cl], (0, 2, 1))
    return jnp.repeat(jnp.repeat(b, GMLP_HEAD_DIM, axis=2), nb, axis=1)


def _t_major(a):
    return jnp.transpose(a, (1, 0, 2)).reshape(a.shape[0] * a.shape[1], a.shape[2])


def _seq_major(a, n_seq):
    return jnp.transpose(a.reshape(a.shape[0] // n_seq, n_seq, a.shape[1]), (1, 0, 2))


def _group_layer(x, st_re, st_im, st_pool, st_conv, state_layer, p, layer, bs, *, nb, npos, start_pos, ffn_rows,
                 seq_major_out=False, emit_vn=False):
    ycat, h_re, h_im, pool_new, *vn = _branches_call(x, st_re, st_im, st_pool, state_layer, p, layer, bs,
                                                      nb=nb, npos=npos, start_pos=start_pos, emit_vn=emit_vn)
    x2, conv_new = _merge_ffn_call(x, ycat, st_conv, state_layer, p, layer, nb=nb, rows=ffn_rows,
                                   seq_major_out=seq_major_out)
    return (x2, h_re, h_im, pool_new, conv_new, *vn)


def kernel(x_prompt, x_sample, state_ssm_re, state_ssm_im, state_pool, state_ffn_conv, w_in, b_gate, ssm_a_re, ssm_a_im, ssm_log_step, ssm_b_re, ssm_b_im, ssm_c_re, ssm_c_im, ssm_d, ssm_w_glu, ssm_b_glu, gmlp_ln_g, gmlp_ln_b, gmlp_w_s, gmlp_b_s, pool_w, pool_scale, w_br_ssm, w_br_gmlp, w_br_pool, w_o, ln1_g, ln1_b, ffn_w_up, ffn_conv_w, ffn_conv_b, ffn_w_down, ln2_g, ln2_b):
    n_p, t_p, _ = x_prompt.shape
    n_s, t_s, _ = x_sample.shape
    hp = x_prompt
    hs = _t_major(x_sample)
    npos_p = TILE_ROWS // n_p
    npos_s = TILE_ROWS // n_s
    assert npos_p == GMLP_CHUNK and npos_s == t_s and t_p % npos_p == 0

    row = lambda v: v.reshape(DEPTH, 1, -1)
    ab_re, ab_im, bb_re, bb_im = _ssm_discretise(ssm_a_re, ssm_a_im, ssm_log_step, ssm_b_re, ssm_b_im)
    p = dict(
        w_in_br=w_in[:, :, :BRANCH_COLS].astype(BF16),
        w_in_gate=w_in[:, :, BRANCH_COLS:].astype(BF16),
        b_gate=row(b_gate),
        bbc=jnp.stack([_compact_in(bb_re), _compact_in(bb_im)], axis=2).astype(BF16),
        a_re=row(ab_re), a_im=row(ab_im),
        ccc=jnp.stack([_compact_out(ssm_c_re), -_compact_out(ssm_c_im)], axis=2).astype(BF16),
        d_skip=row(ssm_d),
        w_glu=ssm_w_glu.astype(BF16), b_glu=row(ssm_b_glu),
        gln_g=row(gmlp_ln_g), gln_b=row(gmlp_ln_b),
        w_s=gmlp_w_s, pool_w=pool_w.astype(BF16), pool_scale=row(pool_scale),
        w_br=jnp.concatenate([w_br_ssm, w_br_gmlp, w_br_pool], axis=1).astype(BF16),
        w_o=w_o.astype(BF16), ln1_g=row(ln1_g), ln1_b=row(ln1_b),
        w_up=ffn_w_up.astype(BF16), conv_w=ffn_conv_w, conv_b=row(ffn_conv_b),
        w_down=ffn_w_down.astype(BF16), ln2_g=row(ln2_g), ln2_b=row(ln2_b),
    )
    bs_p = _gmlp_bias_rows(gmlp_b_s, npos_p, n_p)
    bs_s = _gmlp_bias_rows(gmlp_b_s, npos_s, n_s)
    zeros_p = (jnp.zeros((1, n_p, SSM_COLS), F32), jnp.zeros((1, n_p, SSM_COLS), F32),
               jnp.zeros((1, POOL_PAST * n_p, POOL_WIDTH), F32), jnp.zeros((1, (FFN_CONV - 1) * n_p, F2), F32))
    ssm_re_s = state_ssm_re.reshape(DEPTH, n_s, SSM_COLS)
    ssm_im_s = state_ssm_im.reshape(DEPTH, n_s, SSM_COLS)

    pool_t = jnp.transpose(state_pool, (0, 2, 1, 3)).reshape(DEPTH, POOL_PAST * n_s, POOL_WIDTH)
    conv_t = jnp.transpose(state_ffn_conv, (0, 2, 1, 3)).reshape(DEPTH, (FFN_CONV - 1) * n_s, F2)

    outs = [[] for _ in range(9)]
    for l in range(DEPTH):
        hp, a_re_n, a_im_n, pool_n, conv_n = _group_layer(
            hp, *zeros_p, 0, p, l, bs_p, nb=n_p, npos=npos_p, start_pos=0, ffn_rows=2 * ROW_CHUNK,
            seq_major_out=(l == DEPTH - 1))
        hs, b_re_n, b_im_n, pool_m, conv_m, v_n = _group_layer(
            hs, ssm_re_s, ssm_im_s, pool_t, conv_t, l, p, l, bs_s, nb=n_s, npos=npos_s, start_pos=PAST_LEN, ffn_rows=ROW_CHUNK, emit_vn=True)
        for o, v in zip(outs, (a_re_n, a_im_n, pool_n, conv_n, b_re_n, b_im_n, v_n, pool_m, conv_m)):
            o.append(v)

    p_re, p_im, p_pool, p_conv, s_re, s_im, s_v, s_pool, s_conv = (jnp.stack(o) for o in outs)
    ssm = lambda a, n_seq: a.reshape(DEPTH, n_seq, SSM_GROUPS, SSM_STATE)
    per_seq = lambda a, n_seq: jnp.transpose(a.reshape(DEPTH, -1, n_seq, a.shape[-1]), (0, 2, 1, 3))
    return (hp, _seq_major(hs, n_s),
            ssm(p_re, n_p), ssm(p_im, n_p), per_seq(p_pool, n_p), per_seq(p_conv, n_p),
            ssm(s_re, n_s), ssm(s_im, n_s), per_seq(s_v, n_s), per_seq(s_pool, n_s), per_seq(s_conv, n_s))
```

```python
import functools
import math

import jax
import jax.numpy as jnp
from jax import lax
from jax.experimental import pallas as pl
from jax.experimental.pallas import tpu as pltpu

F32 = jnp.float32
BF16 = jnp.bfloat16

D_MODEL = 1024
DEPTH = 2
PAST_LEN = 16384
SSM_GROUP_CH = 16
SSM_WIDTH = 512
SSM_GROUPS = 32
SSM_STATE = 64
SSM_COLS = SSM_GROUPS * SSM_STATE
GMLP_HEADS = 4
GMLP_CHUNK = 128
GMLP_WIDTH = 256
GMLP_HEAD_DIM = 64
POOL_WINDOWS = (2, 4, 8, 16)
POOL_WIDTH = 512
POOL_GROUP_CH = 128
POOL_PAST = 15
N_BRANCH = 3
BRANCH_COLS = SSM_WIDTH + 2 * GMLP_WIDTH + POOL_WIDTH
D_FF = 2816
F2 = 2 * D_FF
FFN_CONV = 3
LN_EPS = 1e-5
ALPHA = (2 * DEPTH) ** 0.25

SUBLANES = 8
MXU_DIM = 256
VMEM_LIMIT_BYTES = 60 * 1024 * 1024

TILE_ROWS = 1024
ROW_CHUNK = 256
PROJ_CHUNK = 512
SSM_CHUNK = 512
SCAN_PASSES = 2
SCAN_BLOCKS = 2
SSM_KT = SSM_WIDTH // MXU_DIM
KT_COLS = SSM_COLS // SSM_KT
FF_COL_CHUNK = MXU_DIM
HEADS_PER_PASS = GMLP_HEADS // (SSM_KT * SCAN_PASSES)
assert HEADS_PER_PASS * SSM_KT * SCAN_PASSES == GMLP_HEADS


def _gelu(x):
    c = math.sqrt(2.0 / math.pi)
    return 0.5 * x * (1.0 + jnp.tanh(c * (x + 0.044715 * (x * x * x))))


def _sigmoid(x):
    return 1.0 / (1.0 + jnp.exp(-x))


def _layer_norm(x, g, b):
    mu = jnp.mean(x, axis=-1, keepdims=True)
    xc = x - mu
    var = jnp.mean(xc * xc, axis=-1, keepdims=True)
    return xc * lax.rsqrt(var + LN_EPS) * g + b


def _dot(a, b):
    return jnp.dot(a, b, preferred_element_type=F32)


def _load_rows_t_major(x_ref, r0, n, nb, seq_major):
    if not seq_major:
        return x_ref[pl.ds(r0, n), :]
    p0 = r0 // nb if isinstance(r0, int) else pl.multiple_of(r0 >> int(math.log2(nb)), SUBLANES)
    blk = x_ref[:, pl.ds(p0, n // nb), :]
    return jnp.swapaxes(blk, 0, 1).reshape(n, blk.shape[-1])


def _w_in_cols(w_in, layer, block):
    return pl.BlockSpec((None, w_in.shape[1], BRANCH_COLS), lambda i: (layer, 0, block),
                        pipeline_mode=pl.Buffered(1))


def _layer_const(a, layer):
    nd = a.ndim - 1
    return pl.BlockSpec((None,) + a.shape[1:], lambda i: (layer,) + (0,) * nd,
                        pipeline_mode=pl.Buffered(1))


def _expand_mix(ws_ref, wk, rows, shift, nb):
    cl = GMLP_CHUNK
    e = (lax.broadcasted_iota(jnp.int32, (rows, cl), 0) >> shift
         == lax.broadcasted_iota(jnp.int32, (rows, cl), 1)).astype(BF16)
    tril = (lax.broadcasted_iota(jnp.int32, (cl, cl), 0)
            >= lax.broadcasted_iota(jnp.int32, (cl, cl), 1))
    for h in range(GMLP_HEADS):
        w = jnp.where(tril, ws_ref[h], 0.0).astype(BF16)
        rows_w = _dot(e, w).astype(BF16)
        for c0 in range(0, rows, MXU_DIM):
            et = (lax.broadcasted_iota(jnp.int32, (cl, MXU_DIM), 0)
                  == (lax.broadcasted_iota(jnp.int32, (cl, MXU_DIM), 1) + c0) >> shift).astype(BF16)
            full = _dot(rows_w, et)
            same_seq = ((lax.broadcasted_iota(jnp.int32, (rows, MXU_DIM), 0) & (nb - 1))
                        == ((lax.broadcasted_iota(jnp.int32, (rows, MXU_DIM), 1) + c0) & (nb - 1)))
            wk[h, :, c0:c0 + MXU_DIM] = jnp.where(same_seq, full, 0.0).astype(BF16)


def _expand_ssm(bbc_ref, ccc_ref, bb, cc):
    n_in, n_st = KT_COLS // SSM_STATE * SSM_GROUP_CH, KT_COLS
    iota = lambda shape, d: lax.broadcasted_iota(jnp.int32, shape, d)
    log_n, log_c = int(math.log2(SSM_STATE)), int(math.log2(SSM_GROUP_CH))
    spread = ((iota((SSM_STATE, n_st), 1) & (SSM_STATE - 1)) == iota((SSM_STATE, n_st), 0)).astype(BF16)
    same_in = (iota((n_in, n_st), 0) >> log_c) == (iota((n_in, n_st), 1) >> log_n)
    gather = ((iota((n_st, SSM_STATE), 0) & (SSM_STATE - 1)) == iota((n_st, SSM_STATE), 1)).astype(BF16)
    same_out = (iota((n_st, n_in), 0) >> log_n) == (iota((n_st, n_in), 1) >> log_c)
    for kt in range(SSM_KT):
        for part in range(2):
            full = _dot(bbc_ref[kt, part], spread)
            bb[kt, :, part * n_st:(part + 1) * n_st] = jnp.where(same_in, full, 0.0).astype(BF16)
            full = _dot(gather, ccc_ref[kt, part])
            cc[kt, part * n_st:(part + 1) * n_st, :] = jnp.where(same_out, full, 0.0).astype(BF16)


def _branches_kernel(x_ref, h0re_ref, h0im_ref, ppast_ref,
                     w_in_ref, bbc_ref, are_ref, aim_ref, ccc_ref, dsk_ref,
                     wglu_ref, bglu_ref, lng_ref, lnb_ref, ws_ref, bs_ref,
                     poolw_ref, pscale_ref,
                     ycat_ref, hcre, hcim, pool_out_ref,
                     *scratch, nb, npos, n_tiles, start_pos, seq_major_in, emit_vn):
    vn_out_ref = scratch[0] if emit_vn else None
    *hbufs, us, ya, gu, vnb, sacc, xp, wk, bb_ref, cc_ref = scratch[1:] if emit_vn else scratch
    rows = nb * npos
    n_chunks = rows // ROW_CHUNK
    past_rows = POOL_PAST * nb
    tile = pl.program_id(0)
    shift = int(math.log2(nb))
    slabs_per_block = rows // SUBLANES // SCAN_BLOCKS
    steps = min(npos, slabs_per_block)
    groups_per_block = slabs_per_block // steps
    assert groups_per_block == 1 or steps == npos
    mix_rows = rows // SCAN_BLOCKS

    @pl.when(tile == 0)
    def _init():
        hcre[...] = h0re_ref[...]
        hcim[...] = h0im_ref[...]
        xp[0:past_rows, :] = ppast_ref[...]
        _expand_mix(ws_ref, wk, rows, shift, nb)
        _expand_ssm(bbc_ref, ccc_ref, bb_ref, cc_ref)

    base_pos = start_pos + tile * npos

    def proj_chunk(c, carry):
        r0 = pl.multiple_of(c * PROJ_CHUNK, PROJ_CHUNK)
        xb = _load_rows_t_major(x_ref, r0, PROJ_CHUNK, nb, seq_major_in).astype(BF16)
        pa = _dot(xb, w_in_ref[...])
        us[pl.ds(r0, PROJ_CHUNK), :] = pa[:, 0:SSM_WIDTH]
        gu[pl.ds(r0, PROJ_CHUNK), :] = _gelu(pa[:, SSM_WIDTH:SSM_WIDTH + GMLP_WIDTH])
        v = _gelu(pa[:, SSM_WIDTH + GMLP_WIDTH:SSM_WIDTH + 2 * GMLP_WIDTH])
        v = _layer_norm(v, lng_ref[...], lnb_ref[...])
        if emit_vn:
            vn_out_ref[pl.ds(r0, PROJ_CHUNK), :] = v
        vnb[pl.ds(r0, PROJ_CHUNK), :] = v.astype(BF16)
        sacc[pl.ds(r0, PROJ_CHUNK), :] = bs_ref[pl.ds(r0, PROJ_CHUNK), :]
        xp[pl.ds(past_rows + r0, PROJ_CHUNK), :] = pa[:, SSM_WIDTH + 2 * GMLP_WIDTH:BRANCH_COLS]
        return carry

    lax.fori_loop(0, rows // PROJ_CHUNK, proj_chunk, 0)

    n_hbuf = len(hbufs)
    overlap = n_hbuf > 1

    def bu_rows(kt, r0):
        ub = us[pl.ds(r0, SSM_CHUNK), kt * MXU_DIM:(kt + 1) * MXU_DIM].astype(BF16)
        hbufs[kt % n_hbuf][pl.ds(r0, SSM_CHUNK), :] = _dot(ub, bb_ref[kt])

    def y_rows(kt, r0):
        hb = hbufs[kt % n_hbuf][pl.ds(r0, SSM_CHUNK), :].astype(BF16)
        u = us[pl.ds(r0, SSM_CHUNK), kt * MXU_DIM:(kt + 1) * MXU_DIM]
        y = _dot(hb, cc_ref[kt]) + dsk_ref[:, kt * MXU_DIM:(kt + 1) * MXU_DIM] * u
        ya[pl.ds(r0, SSM_CHUNK), kt * MXU_DIM:(kt + 1) * MXU_DIM] = y

    def chunk_loop(fn, kt):
        def body(c, carry):
            fn(kt, pl.multiple_of(c * SSM_CHUNK, SSM_CHUNK))
            return carry
        lax.fori_loop(0, rows // SSM_CHUNK, body, 0)

    if overlap:
        assert rows // SSM_CHUNK == SCAN_BLOCKS
        chunk_loop(bu_rows, 0)
    for kt in range(SSM_KT):
        c_lo = kt * KT_COLS
        hbuf = hbufs[kt % n_hbuf]
        if not overlap:
            chunk_loop(bu_rows, kt)

        width = KT_COLS // SCAN_PASSES
        for ch in range(SCAN_PASSES):
            lo = ch * width
            if overlap:
                heads_kt = GMLP_HEADS // SSM_KT
                mix_heads = range(kt * heads_kt, (kt + 1) * heads_kt) if ch == SCAN_PASSES - 1 else ()
            else:
                first = (kt * SCAN_PASSES + ch) * HEADS_PER_PASS
                mix_heads = range(first, first + HEADS_PER_PASS)
            a_r = jnp.broadcast_to(are_ref[:, c_lo + lo:c_lo + lo + width], (SUBLANES, width))
            a_i = jnp.broadcast_to(aim_ref[:, c_lo + lo:c_lo + lo + width], (SUBLANES, width))

            def scan_block(j, carry, kt=kt, hbuf=hbuf, ch=ch, lo=lo, a_r=a_r, a_i=a_i, c_lo=c_lo, mix_heads=mix_heads):
                for q in range(groups_per_block):
                    if groups_per_block == 1 and steps < npos:
                        t0, b0 = j * steps, 0
                    else:
                        t0, b0 = 0, (j * groups_per_block + q) * SUBLANES
                    h_r = hcre[pl.ds(b0, SUBLANES), c_lo + lo:c_lo + lo + width]
                    h_i = hcim[pl.ds(b0, SUBLANES), c_lo + lo:c_lo + lo + width]
                    for t in range(steps):
                        row = (t0 + t) * nb + b0
                        b_r = hbuf[pl.ds(row, SUBLANES), lo:lo + width]
                        b_i = hbuf[pl.ds(row, SUBLANES), KT_COLS + lo:KT_COLS + lo + width]
                        h_r, h_i = (a_r * h_r - a_i * h_i + b_r,
                                    a_r * h_i + a_i * h_r + b_i)
                        hbuf[pl.ds(row, SUBLANES), lo:lo + width] = h_r
                        hbuf[pl.ds(row, SUBLANES), KT_COLS + lo:KT_COLS + lo + width] = h_i
                    hcre[pl.ds(b0, SUBLANES), c_lo + lo:c_lo + lo + width] = h_r
                    hcim[pl.ds(b0, SUBLANES), c_lo + lo:c_lo + lo + width] = h_i
                if mix_heads:
                    m0 = j * mix_rows
                    head = lax.broadcasted_iota(jnp.int32, (1, GMLP_WIDTH), 1) // GMLP_HEAD_DIM
                    acc = sacc[pl.ds(m0, mix_rows), :]
                    for mh in mix_heads:
                        k_hi = (j + 1) * mix_rows
                        mixed = _dot(wk[mh, pl.ds(m0, mix_rows), 0:k_hi], vnb[0:k_hi, :])
                        acc = acc + jnp.where(head == mh, mixed, 0.0)
                    sacc[pl.ds(m0, mix_rows), :] = acc
                if overlap and ch == 0:
                    r0 = j * SSM_CHUNK
                    if kt + 1 < SSM_KT:
                        bu_rows(kt + 1, r0)
                    if kt > 0:
                        y_rows(kt - 1, r0)
                return carry

            for j in range(SCAN_BLOCKS):
                scan_block(j, 0)

        if not overlap:
            chunk_loop(y_rows, kt)
    if overlap:
        chunk_loop(y_rows, SSM_KT - 1)

    def out_chunk(c, carry):
        r0 = pl.multiple_of(c * ROW_CHUNK, ROW_CHUNK)
        z = _gelu(ya[pl.ds(r0, ROW_CHUNK), :])
        o_a = z * _sigmoid(_dot(z.astype(BF16), wglu_ref[...]) + bglu_ref[...])
        ycat_ref[pl.ds(r0, ROW_CHUNK), 0:SSM_WIDTH] = o_a.astype(BF16)
        y_b = gu[pl.ds(r0, ROW_CHUNK), :] * sacc[pl.ds(r0, ROW_CHUNK), :]
        ycat_ref[pl.ds(r0, ROW_CHUNK), SSM_WIDTH:SSM_WIDTH + GMLP_WIDTH] = y_b.astype(BF16)

        ridx = lax.broadcasted_iota(jnp.int32, (ROW_CHUNK, 1), 0) + r0
        pos = base_pos + (ridx >> shift)
        for g, win in enumerate(POOL_WINDOWS):
            cl = g * POOL_GROUP_CH
            reach = (win - 1) * nb
            if reach < ROW_CHUNK:
                off = pl.multiple_of(past_rows - reach + r0, SUBLANES)
                ext = xp[pl.ds(off, ROW_CHUNK + reach), cl:cl + POOL_GROUP_CH]
                cur = ext[reach:, :]
                wsum, span = ext, 1
                while span < win:
                    n = wsum.shape[0] - span * nb
                    wsum = wsum[span * nb:, :] + wsum[:n, :]
                    span *= 2
            else:
                cur = xp[pl.ds(past_rows + r0, ROW_CHUNK), cl:cl + POOL_GROUP_CH]
                wsum = cur
                for j in range(1, win):
                    off = pl.multiple_of(past_rows - j * nb + r0, SUBLANES)
                    wsum = wsum + xp[pl.ds(off, ROW_CHUNK), cl:cl + POOL_GROUP_CH]
            cnt = jnp.minimum(pos + 1, win).astype(F32)
            d = wsum / cnt - cur
            y_c = _dot(d.astype(BF16), poolw_ref[g]) * pscale_ref[:, cl:cl + POOL_GROUP_CH]
            col = SSM_WIDTH + GMLP_WIDTH + cl
            ycat_ref[pl.ds(r0, ROW_CHUNK), col:col + POOL_GROUP_CH] = y_c.astype(BF16)
        return carry

    lax.fori_loop(0, n_chunks, out_chunk, 0)

    pool_out_ref[...] = xp[rows:rows + past_rows, :]
    if n_tiles > 1:
        xp[0:past_rows, :] = xp[rows:rows + past_rows, :]


def _branches_call(x, h0re, h0im, ppast, state_layer, p, layer, bs, *, nb, npos, start_pos, emit_vn):
    seq_major_in = x.ndim == 3
    n_rows = x.shape[0] * x.shape[1] if seq_major_in else x.shape[0]
    rows = nb * npos
    n_tiles = n_rows // rows
    past_rows = POOL_PAST * nb
    kern = functools.partial(_branches_kernel, nb=nb, npos=npos, n_tiles=n_tiles, start_pos=start_pos,
                             seq_major_in=seq_major_in, emit_vn=emit_vn)
    row_tile = lambda w: pl.BlockSpec((rows, w), lambda i: (i, 0))
    x_spec = pl.BlockSpec((nb, npos, D_MODEL), lambda i: (0, i, 0)) if seq_major_in else row_tile(D_MODEL)
    n_hbuf = SSM_KT if n_tiles > 1 else 1
    fixed = lambda r, w: pl.BlockSpec((r, w), lambda i: (0, 0))
    consts = (p['bbc'], p['a_re'], p['a_im'], p['ccc'], p['d_skip'],
              p['w_glu'], p['b_glu'], p['gln_g'], p['gln_b'], p['w_s'], bs, p['pool_w'], p['pool_scale'])
    return pl.pallas_call(
        kern,
        grid=(n_tiles,),
        in_specs=[x_spec] + [_layer_const(s, state_layer) for s in (h0re, h0im, ppast)]
                 + [_w_in_cols(p['w_in'], layer, 0)] + [_layer_const(c, layer) for c in consts],
        out_specs=[row_tile(BRANCH_COLS - GMLP_WIDTH), fixed(nb, SSM_COLS), fixed(nb, SSM_COLS),
                   fixed(past_rows, POOL_WIDTH)] + ([row_tile(GMLP_WIDTH)] if emit_vn else []),
        out_shape=[jax.ShapeDtypeStruct((n_rows, BRANCH_COLS - GMLP_WIDTH), BF16),
                   jax.ShapeDtypeStruct((nb, SSM_COLS), F32),
                   jax.ShapeDtypeStruct((nb, SSM_COLS), F32),
                   jax.ShapeDtypeStruct((past_rows, POOL_WIDTH), F32)]
                  + ([jax.ShapeDtypeStruct((n_rows, GMLP_WIDTH), F32)] if emit_vn else []),
        scratch_shapes=[pltpu.VMEM((rows, 2 * KT_COLS), F32)] * n_hbuf
                       + [pltpu.VMEM((rows, SSM_WIDTH), F32),
                        pltpu.VMEM((rows, SSM_WIDTH), F32),
                        pltpu.VMEM((rows, GMLP_WIDTH), F32),
                        pltpu.VMEM((rows, GMLP_WIDTH), BF16),
                        pltpu.VMEM((rows, GMLP_WIDTH), F32),
                        pltpu.VMEM((rows + past_rows, POOL_WIDTH), F32),
                        pltpu.VMEM((GMLP_HEADS, rows, rows), BF16),
                        pltpu.VMEM((SSM_KT, MXU_DIM, 2 * KT_COLS), BF16),
                        pltpu.VMEM((SSM_KT, 2 * KT_COLS, MXU_DIM), BF16)],
        compiler_params=pltpu.CompilerParams(dimension_semantics=("arbitrary",),
                                             vmem_limit_bytes=VMEM_LIMIT_BYTES),
        name="branches",
    )(x, h0re, h0im, ppast, p['w_in'], *consts)


def _merge_ffn_kernel(x_ref, ycat_ref, cpast_ref,
                      wga_ref, wgb_ref, bg_ref, wbr_ref, wo_ref, g1_ref, b1_ref,
                      wup_ref, cw_ref, cb_ref, wdn_ref, g2_ref, b2_ref,
                      x2_ref, cout_ref, tailbuf, act, *, nb, rows, seq_major_in, seq_major_out):
    tail = (FFN_CONV - 1) * nb
    assert rows >= tail

    @pl.when(pl.program_id(0) == 0)
    def _init():
        tailbuf[...] = cpast_ref[...]

    x = _load_rows_t_major(x_ref, 0, rows, nb, seq_major_in)
    xb = x.astype(BF16)
    yc = ycat_ref[...]
    bounds = (0, SSM_WIDTH, SSM_WIDTH + GMLP_WIDTH, BRANCH_COLS - GMLP_WIDTH)
    merged = None

    def gate_logits(c0, c1):
        parts = []
        for ref, base in ((wga_ref, 0), (wgb_ref, BRANCH_COLS)):
            lo, hi = max(c0, base) - base, min(c1, base + BRANCH_COLS) - base
            if lo < hi:
                parts.append(_dot(xb, ref[:, lo:hi]))
        return parts[0] if len(parts) == 1 else jnp.concatenate(parts, axis=1)

    for i in range(N_BRANCH):
        gate = _sigmoid(gate_logits(i * D_MODEL, (i + 1) * D_MODEL) + bg_ref[:, i * D_MODEL:(i + 1) * D_MODEL])
        br = _dot(yc[:, bounds[i]:bounds[i + 1]], wbr_ref[bounds[i]:bounds[i + 1], :])
        merged = gate * br if merged is None else merged + gate * br
    mix = _dot(merged.astype(BF16), wo_ref[...])
    x1 = _layer_norm(ALPHA * x + mix, g1_ref[...], b1_ref[...])

    x1b = x1.astype(BF16)
    for cc in range(D_FF // FF_COL_CHUNK):
        conv = []
        for part in range(2):
            lo = part * D_FF + cc * FF_COL_CHUNK
            cols = slice(lo, lo + FF_COL_CHUNK)
            h = _dot(x1b, wup_ref[:, cols])
            he = jnp.concatenate([tailbuf[:, cols], h], axis=0)
            tailbuf[:, cols] = h[rows - tail:rows, :]
            y = cb_ref[:, cols] + he[0:rows, :] * cw_ref[0:1, cols]
            y = y + he[nb:nb + rows, :] * cw_ref[1:2, cols]
            y = y + h * cw_ref[2:3, cols]
            conv.append(y)
        a = _gelu(conv[1]) * conv[0]
        act[:, cc * FF_COL_CHUNK:(cc + 1) * FF_COL_CHUNK] = a.astype(BF16)
    ff = _dot(act[...], wdn_ref[...])
    x2 = _layer_norm(ALPHA * x1 + ff, g2_ref[...], b2_ref[...])
    if seq_major_out:
        x2_ref[...] = jnp.swapaxes(x2.reshape(rows // nb, nb, D_MODEL), 0, 1)
    else:
        x2_ref[...] = x2
    cout_ref[...] = tailbuf[...]


def _merge_ffn_call(x, ycat, cpast, state_layer, p, layer, *, nb, rows, seq_major_out):
    seq_major_in = x.ndim == 3
    n_rows = ycat.shape[0]
    tail = (FFN_CONV - 1) * nb
    kern = functools.partial(_merge_ffn_kernel, nb=nb, rows=rows,
                             seq_major_in=seq_major_in, seq_major_out=seq_major_out)
    consts = (p['b_gate'], p['w_br'], p['w_o'], p['ln1_g'], p['ln1_b'],
              p['w_up'], p['conv_w'], p['conv_b'], p['w_down'], p['ln2_g'], p['ln2_b'])
    t_major_spec = pl.BlockSpec((rows, D_MODEL), lambda i: (i, 0))
    seq_major_spec = pl.BlockSpec((nb, rows // nb, D_MODEL), lambda i: (0, i, 0))
    return pl.pallas_call(
        kern,
        grid=(n_rows // rows,),
        in_specs=[seq_major_spec if seq_major_in else t_major_spec,
                  pl.BlockSpec((rows, BRANCH_COLS - GMLP_WIDTH), lambda i: (i, 0)),
                  _layer_const(cpast, state_layer),
                  _w_in_cols(p['w_in'], layer, 1), _w_in_cols(p['w_in'], layer, 2)]
                 + [_layer_const(c, layer) for c in consts],
        out_specs=[seq_major_spec if seq_major_out else t_major_spec,
                   pl.BlockSpec((tail, F2), lambda i: (0, 0))],
        out_shape=[jax.ShapeDtypeStruct((nb, n_rows // nb, D_MODEL) if seq_major_out else (n_rows, D_MODEL), F32),
                   jax.ShapeDtypeStruct((tail, F2), F32)],
        scratch_shapes=[pltpu.VMEM((tail, F2), F32),
                        pltpu.VMEM((rows, D_FF), BF16)],
        compiler_params=pltpu.CompilerParams(dimension_semantics=("arbitrary",),
                                             vmem_limit_bytes=VMEM_LIMIT_BYTES),
        name="merge_convffn",
    )(x, ycat, cpast, p['w_in'], p['w_in'], *consts)


def _ssm_discretise(a_re, a_im, log_step, b_re, b_im):
    dt = jnp.exp(log_step)[..., None]
    mag = jnp.exp(a_re * dt)
    ab_re = mag * jnp.cos(a_im * dt)
    ab_im = mag * jnp.sin(a_im * dt)
    den = jnp.square(a_re) + jnp.square(a_im)
    nr = ab_re - 1.0
    k_re = (nr * a_re + ab_im * a_im) / den
    k_im = (ab_im * a_re - nr * a_im) / den
    bb_re = k_re[..., None] * b_re - k_im[..., None] * b_im
    bb_im = k_re[..., None] * b_im + k_im[..., None] * b_re
    return ab_re, ab_im, bb_re, bb_im


def _compact_in(m):
    gpt = SSM_GROUPS // SSM_KT
    return jnp.transpose(m, (0, 1, 3, 2)).reshape(-1, SSM_KT, gpt * SSM_GROUP_CH, SSM_STATE)


def _compact_out(m):
    gpt = SSM_GROUPS // SSM_KT
    m = m.reshape(-1, SSM_KT, gpt, SSM_GROUP_CH, SSM_STATE)
    return jnp.transpose(m, (0, 1, 4, 2, 3)).reshape(-1, SSM_KT, SSM_STATE, gpt * SSM_GROUP_CH)


def _gmlp_bias_rows(b_s, cl, nb):
    b = jnp.transpose(b_s[:, :, :cl], (0, 2, 1))
    return jnp.repeat(jnp.repeat(b, GMLP_HEAD_DIM, axis=2), nb, axis=1)


def _t_major(a):
    return jnp.transpose(a, (1, 0, 2)).reshape(a.shape[0] * a.shape[1], a.shape[2])


def _seq_major(a, n_seq):
    return jnp.transpose(a.reshape(a.shape[0] // n_seq, n_seq, a.shape[1]), (1, 0, 2))


def _group_layer(x, st_re, st_im, st_pool, st_conv, state_layer, p, layer, bs, *, nb, npos, start_pos, ffn_rows,
                 seq_major_out=False, emit_vn=False):
    ycat, h_re, h_im, pool_new, *vn = _branches_call(x, st_re, st_im, st_pool, state_layer, p, layer, bs,
                                                      nb=nb, npos=npos, start_pos=start_pos, emit_vn=emit_vn)
    x2, conv_new = _merge_ffn_call(x, ycat, st_conv, state_layer, p, layer, nb=nb, rows=ffn_rows,
                                   seq_major_out=seq_major_out)
    return (x2, h_re, h_im, pool_new, conv_new, *vn)


def kernel(x_prompt, x_sample, state_ssm_re, state_ssm_im, state_pool, state_ffn_conv, w_in, b_gate, ssm_a_re, ssm_a_im, ssm_log_step, ssm_b_re, ssm_b_im, ssm_c_re, ssm_c_im, ssm_d, ssm_w_glu, ssm_b_glu, gmlp_ln_g, gmlp_ln_b, gmlp_w_s, gmlp_b_s, pool_w, pool_scale, w_br_ssm, w_br_gmlp, w_br_pool, w_o, ln1_g, ln1_b, ffn_w_up, ffn_conv_w, ffn_conv_b, ffn_w_down, ln2_g, ln2_b):
    n_p, t_p, _ = x_prompt.shape
    n_s, t_s, _ = x_sample.shape
    hp = x_prompt
    hs = _t_major(x_sample)
    npos_p = TILE_ROWS // n_p
    npos_s = TILE_ROWS // n_s
    assert npos_p == GMLP_CHUNK and npos_s == t_s and t_p % npos_p == 0

    row = lambda v: v.reshape(DEPTH, 1, -1)
    ab_re, ab_im, bb_re, bb_im = _ssm_discretise(ssm_a_re, ssm_a_im, ssm_log_step, ssm_b_re, ssm_b_im)
    p = dict(
        w_in=w_in.astype(BF16),
        b_gate=row(b_gate),
        bbc=jnp.stack([_compact_in(bb_re), _compact_in(bb_im)], axis=2).astype(BF16),
        a_re=row(ab_re), a_im=row(ab_im),
        ccc=jnp.stack([_compact_out(ssm_c_re), -_compact_out(ssm_c_im)], axis=2).astype(BF16),
        d_skip=row(ssm_d),
        w_glu=ssm_w_glu.astype(BF16), b_glu=row(ssm_b_glu),
        gln_g=row(gmlp_ln_g), gln_b=row(gmlp_ln_b),
        w_s=gmlp_w_s, pool_w=pool_w.astype(BF16), pool_scale=row(pool_scale),
        w_br=jnp.concatenate([w_br_ssm, w_br_gmlp, w_br_pool], axis=1).astype(BF16),
        w_o=w_o.astype(BF16), ln1_g=row(ln1_g), ln1_b=row(ln1_b),
        w_up=ffn_w_up.astype(BF16), conv_w=ffn_conv_w, conv_b=row(ffn_conv_b),
        w_down=ffn_w_down.astype(BF16), ln2_g=row(ln2_g), ln2_b=row(ln2_b),
    )
    bs_p = _gmlp_bias_rows(gmlp_b_s, npos_p, n_p)
    bs_s = _gmlp_bias_rows(gmlp_b_s, npos_s, n_s)
    zeros_p = (jnp.zeros((1, n_p, SSM_COLS), F32), jnp.zeros((1, n_p, SSM_COLS), F32),
               jnp.zeros((1, POOL_PAST * n_p, POOL_WIDTH), F32), jnp.zeros((1, (FFN_CONV - 1) * n_p, F2), F32))
    ssm_re_s = state_ssm_re.reshape(DEPTH, n_s, SSM_COLS)
    ssm_im_s = state_ssm_im.reshape(DEPTH, n_s, SSM_COLS)

    pool_t = jnp.transpose(state_pool, (0, 2, 1, 3)).reshape(DEPTH, POOL_PAST * n_s, POOL_WIDTH)
    conv_t = jnp.transpose(state_ffn_conv, (0, 2, 1, 3)).reshape(DEPTH, (FFN_CONV - 1) * n_s, F2)

    outs = [[] for _ in range(9)]
    for l in range(DEPTH):
        hp, a_re_n, a_im_n, pool_n, conv_n = _group_layer(
            hp, *zeros_p, 0, p, l, bs_p, nb=n_p, npos=npos_p, start_pos=0, ffn_rows=2 * ROW_CHUNK,
            seq_major_out=(l == DEPTH - 1))
        hs, b_re_n, b_im_n, pool_m, conv_m, v_n = _group_layer(
            hs, ssm_re_s, ssm_im_s, pool_t, conv_t, l, p, l, bs_s, nb=n_s, npos=npos_s, start_pos=PAST_LEN, ffn_rows=ROW_CHUNK, emit_vn=True)
        for o, v in zip(outs, (a_re_n, a_im_n, pool_n, conv_n, b_re_n, b_im_n, v_n, pool_m, conv_m)):
            o.append(v)

    p_re, p_im, p_pool, p_conv, s_re, s_im, s_v, s_pool, s_conv = (jnp.stack(o) for o in outs)
    ssm = lambda a, n_seq: a.reshape(DEPTH, n_seq, SSM_GROUPS, SSM_STATE)
    per_seq = lambda a, n_seq: jnp.transpose(a.reshape(DEPTH, -1, n_seq, a.shape[-1]), (0, 2, 1, 3))
    return (hp, _seq_major(hs, n_s),
            ssm(p_re, n_p), ssm(p_im, n_p), per_seq(p_pool, n_p), per_seq(p_conv, n_p),
            ssm(s_re, n_s), ssm(s_im, n_s), per_seq(s_v, n_s), per_seq(s_pool, n_s), per_seq(s_conv, n_s))
```

```python
import functools
import math

import jax
import jax.numpy as jnp
from jax import lax
from jax.experimental import pallas as pl
from jax.experimental.pallas import tpu as pltpu

F32 = jnp.float32
BF16 = jnp.bfloat16

D_MODEL = 1024
DEPTH = 2
PAST_LEN = 16384
SSM_GROUP_CH = 16
SSM_WIDTH = 512
SSM_GROUPS = 32
SSM_STATE = 64
SSM_COLS = SSM_GROUPS * SSM_STATE
GMLP_HEADS = 4
GMLP_CHUNK = 128
GMLP_WIDTH = 256
GMLP_HEAD_DIM = 64
POOL_WINDOWS = (2, 4, 8, 16)
POOL_WIDTH = 512
POOL_GROUP_CH = 128
POOL_PAST = 15
N_BRANCH = 3
BRANCH_COLS = SSM_WIDTH + 2 * GMLP_WIDTH + POOL_WIDTH
D_FF = 2816
F2 = 2 * D_FF
FFN_CONV = 3
LN_EPS = 1e-5
ALPHA = (2 * DEPTH) ** 0.25

SUBLANES = 8
MXU_DIM = 256
VMEM_LIMIT_BYTES = 60 * 1024 * 1024

TILE_ROWS = 1024
ROW_CHUNK = 256
PROJ_CHUNK = 512
SSM_CHUNK = 512
SCAN_PASSES = 1
SCAN_BLOCKS = 2
SSM_KT = SSM_WIDTH // MXU_DIM
KT_COLS = SSM_COLS // SSM_KT
FF_COL_CHUNK = MXU_DIM
HEADS_PER_PASS = GMLP_HEADS // (SSM_KT * SCAN_PASSES)
assert HEADS_PER_PASS * SSM_KT * SCAN_PASSES == GMLP_HEADS


def _gelu(x):
    c = math.sqrt(2.0 / math.pi)
    return 0.5 * x * (1.0 + jnp.tanh(c * (x + 0.044715 * (x * x * x))))


def _sigmoid(x):
    return 1.0 / (1.0 + jnp.exp(-x))


def _layer_norm(x, g, b):
    mu = jnp.mean(x, axis=-1, keepdims=True)
    xc = x - mu
    var = jnp.mean(xc * xc, axis=-1, keepdims=True)
    return xc * lax.rsqrt(var + LN_EPS) * g + b


def _dot(a, b):
    return jnp.dot(a, b, preferred_element_type=F32)


def _load_rows_t_major(x_ref, r0, n, nb, seq_major):
    if not seq_major:
        return x_ref[pl.ds(r0, n), :]
    p0 = r0 // nb if isinstance(r0, int) else pl.multiple_of(r0 >> int(math.log2(nb)), SUBLANES)
    blk = x_ref[:, pl.ds(p0, n // nb), :]
    return jnp.swapaxes(blk, 0, 1).reshape(n, blk.shape[-1])


def _w_in_cols(w_in, layer, block):
    return pl.BlockSpec((None, w_in.shape[1], BRANCH_COLS), lambda i: (layer, 0, block),
                        pipeline_mode=pl.Buffered(1))


def _layer_const(a, layer):
    nd = a.ndim - 1
    return pl.BlockSpec((None,) + a.shape[1:], lambda i: (layer,) + (0,) * nd,
                        pipeline_mode=pl.Buffered(1))


def _expand_mix(ws_ref, wk, rows, shift, nb):
    cl = GMLP_CHUNK
    e = (lax.broadcasted_iota(jnp.int32, (rows, cl), 0) >> shift
         == lax.broadcasted_iota(jnp.int32, (rows, cl), 1)).astype(BF16)
    tril = (lax.broadcasted_iota(jnp.int32, (cl, cl), 0)
            >= lax.broadcasted_iota(jnp.int32, (cl, cl), 1))
    for h in range(GMLP_HEADS):
        w = jnp.where(tril, ws_ref[h], 0.0).astype(BF16)
        rows_w = _dot(e, w).astype(BF16)
        for c0 in range(0, rows, MXU_DIM):
            et = (lax.broadcasted_iota(jnp.int32, (cl, MXU_DIM), 0)
                  == (lax.broadcasted_iota(jnp.int32, (cl, MXU_DIM), 1) + c0) >> shift).astype(BF16)
            full = _dot(rows_w, et)
            same_seq = ((lax.broadcasted_iota(jnp.int32, (rows, MXU_DIM), 0) & (nb - 1))
                        == ((lax.broadcasted_iota(jnp.int32, (rows, MXU_DIM), 1) + c0) & (nb - 1)))
            wk[h, :, c0:c0 + MXU_DIM] = jnp.where(same_seq, full, 0.0).astype(BF16)


def _expand_ssm(bbc_ref, ccc_ref, bb, cc):
    n_in, n_st = KT_COLS // SSM_STATE * SSM_GROUP_CH, KT_COLS
    iota = lambda shape, d: lax.broadcasted_iota(jnp.int32, shape, d)
    log_n, log_c = int(math.log2(SSM_STATE)), int(math.log2(SSM_GROUP_CH))
    spread = ((iota((SSM_STATE, n_st), 1) & (SSM_STATE - 1)) == iota((SSM_STATE, n_st), 0)).astype(BF16)
    same_in = (iota((n_in, n_st), 0) >> log_c) == (iota((n_in, n_st), 1) >> log_n)
    gather = ((iota((n_st, SSM_STATE), 0) & (SSM_STATE - 1)) == iota((n_st, SSM_STATE), 1)).astype(BF16)
    same_out = (iota((n_st, n_in), 0) >> log_n) == (iota((n_st, n_in), 1) >> log_c)
    for kt in range(SSM_KT):
        for part in range(2):
            full = _dot(bbc_ref[kt, part], spread)
            bb[kt, :, part * n_st:(part + 1) * n_st] = jnp.where(same_in, full, 0.0).astype(BF16)
            full = _dot(gather, ccc_ref[kt, part])
            cc[kt, part * n_st:(part + 1) * n_st, :] = jnp.where(same_out, full, 0.0).astype(BF16)


def _branches_kernel(x_ref, h0re_ref, h0im_ref, ppast_ref,
                     w_in_ref, bbc_ref, are_ref, aim_ref, ccc_ref, dsk_ref,
                     wglu_ref, bglu_ref, lng_ref, lnb_ref, ws_ref, bs_ref,
                     poolw_ref, pscale_ref,
                     ycat_ref, hcre, hcim, pool_out_ref,
                     *scratch, nb, npos, n_tiles, start_pos, seq_major_in, emit_vn):
    vn_out_ref = scratch[0] if emit_vn else None
    *hbufs, us, ya, gu, vnb, sacc, xp, wk, bb_ref, cc_ref = scratch[1:] if emit_vn else scratch
    rows = nb * npos
    n_chunks = rows // ROW_CHUNK
    past_rows = POOL_PAST * nb
    tile = pl.program_id(0)
    shift = int(math.log2(nb))
    slabs_per_block = rows // SUBLANES // SCAN_BLOCKS
    steps = min(npos, slabs_per_block)
    groups_per_block = slabs_per_block // steps
    assert groups_per_block == 1 or steps == npos
    mix_rows = rows // SCAN_BLOCKS

    @pl.when(tile == 0)
    def _init():
        hcre[...] = h0re_ref[...]
        hcim[...] = h0im_ref[...]
        xp[0:past_rows, :] = ppast_ref[...]
        _expand_mix(ws_ref, wk, rows, shift, nb)
        _expand_ssm(bbc_ref, ccc_ref, bb_ref, cc_ref)

    base_pos = start_pos + tile * npos

    def proj_chunk(c, carry):
        r0 = pl.multiple_of(c * PROJ_CHUNK, PROJ_CHUNK)
        xb = _load_rows_t_major(x_ref, r0, PROJ_CHUNK, nb, seq_major_in).astype(BF16)
        pa = _dot(xb, w_in_ref[...])
        us[pl.ds(r0, PROJ_CHUNK), :] = pa[:, 0:SSM_WIDTH]
        gu[pl.ds(r0, PROJ_CHUNK), :] = _gelu(pa[:, SSM_WIDTH:SSM_WIDTH + GMLP_WIDTH])
        v = _gelu(pa[:, SSM_WIDTH + GMLP_WIDTH:SSM_WIDTH + 2 * GMLP_WIDTH])
        v = _layer_norm(v, lng_ref[...], lnb_ref[...])
        if emit_vn:
            vn_out_ref[pl.ds(r0, PROJ_CHUNK), :] = v
        vnb[pl.ds(r0, PROJ_CHUNK), :] = v.astype(BF16)
        sacc[pl.ds(r0, PROJ_CHUNK), :] = bs_ref[pl.ds(r0, PROJ_CHUNK), :]
        xp[pl.ds(past_rows + r0, PROJ_CHUNK), :] = pa[:, SSM_WIDTH + 2 * GMLP_WIDTH:BRANCH_COLS]
        return carry

    lax.fori_loop(0, rows // PROJ_CHUNK, proj_chunk, 0)

    n_hbuf = len(hbufs)
    overlap = n_hbuf > 1

    def bu_rows(kt, r0):
        ub = us[pl.ds(r0, SSM_CHUNK), kt * MXU_DIM:(kt + 1) * MXU_DIM].astype(BF16)
        hbufs[kt % n_hbuf][pl.ds(r0, SSM_CHUNK), :] = _dot(ub, bb_ref[kt])

    def y_rows(kt, r0):
        hb = hbufs[kt % n_hbuf][pl.ds(r0, SSM_CHUNK), :].astype(BF16)
        u = us[pl.ds(r0, SSM_CHUNK), kt * MXU_DIM:(kt + 1) * MXU_DIM]
        y = _dot(hb, cc_ref[kt]) + dsk_ref[:, kt * MXU_DIM:(kt + 1) * MXU_DIM] * u
        ya[pl.ds(r0, SSM_CHUNK), kt * MXU_DIM:(kt + 1) * MXU_DIM] = y

    def chunk_loop(fn, kt):
        def body(c, carry):
            fn(kt, pl.multiple_of(c * SSM_CHUNK, SSM_CHUNK))
            return carry
        lax.fori_loop(0, rows // SSM_CHUNK, body, 0)

    if overlap:
        assert rows // SSM_CHUNK == SCAN_BLOCKS
        chunk_loop(bu_rows, 0)
    for kt in range(SSM_KT):
        c_lo = kt * KT_COLS
        hbuf = hbufs[kt % n_hbuf]
        if not overlap:
            chunk_loop(bu_rows, kt)

        width = KT_COLS // SCAN_PASSES
        for ch in range(SCAN_PASSES):
            lo = ch * width
            if overlap:
                heads_kt = GMLP_HEADS // SSM_KT
                mix_heads = range(kt * heads_kt, (kt + 1) * heads_kt) if ch == SCAN_PASSES - 1 else ()
            else:
                first = (kt * SCAN_PASSES + ch) * HEADS_PER_PASS
                mix_heads = range(first, first + HEADS_PER_PASS)
            a_r = jnp.broadcast_to(are_ref[:, c_lo + lo:c_lo + lo + width], (SUBLANES, width))
            a_i = jnp.broadcast_to(aim_ref[:, c_lo + lo:c_lo + lo + width], (SUBLANES, width))

            def scan_block(j, carry, kt=kt, hbuf=hbuf, ch=ch, lo=lo, a_r=a_r, a_i=a_i, c_lo=c_lo, mix_heads=mix_heads):
                for q in range(groups_per_block):
                    if groups_per_block == 1 and steps < npos:
                        t0, b0 = j * steps, 0
                    else:
                        t0, b0 = 0, (j * groups_per_block + q) * SUBLANES
                    h_r = hcre[pl.ds(b0, SUBLANES), c_lo + lo:c_lo + lo + width]
                    h_i = hcim[pl.ds(b0, SUBLANES), c_lo + lo:c_lo + lo + width]
                    for t in range(steps):
                        row = (t0 + t) * nb + b0
                        b_r = hbuf[pl.ds(row, SUBLANES), lo:lo + width]
                        b_i = hbuf[pl.ds(row, SUBLANES), KT_COLS + lo:KT_COLS + lo + width]
                        h_r, h_i = (a_r * h_r - a_i * h_i + b_r,
                                    a_r * h_i + a_i * h_r + b_i)
                        hbuf[pl.ds(row, SUBLANES), lo:lo + width] = h_r
                        hbuf[pl.ds(row, SUBLANES), KT_COLS + lo:KT_COLS + lo + width] = h_i
                    hcre[pl.ds(b0, SUBLANES), c_lo + lo:c_lo + lo + width] = h_r
                    hcim[pl.ds(b0, SUBLANES), c_lo + lo:c_lo + lo + width] = h_i
                if mix_heads:
                    m0 = j * mix_rows
                    head = lax.broadcasted_iota(jnp.int32, (1, GMLP_WIDTH), 1) // GMLP_HEAD_DIM
                    acc = sacc[pl.ds(m0, mix_rows), :]
                    for mh in mix_heads:
                        k_hi = (j + 1) * mix_rows
                        mixed = _dot(wk[mh, pl.ds(m0, mix_rows), 0:k_hi], vnb[0:k_hi, :])
                        acc = acc + jnp.where(head == mh, mixed, 0.0)
                    sacc[pl.ds(m0, mix_rows), :] = acc
                if overlap and ch == 0:
                    r0 = j * SSM_CHUNK
                    if kt + 1 < SSM_KT:
                        bu_rows(kt + 1, r0)
                    if kt > 0:
                        y_rows(kt - 1, r0)
                return carry

            for j in range(SCAN_BLOCKS):
                scan_block(j, 0)

        if not overlap:
            chunk_loop(y_rows, kt)
    if overlap:
        chunk_loop(y_rows, SSM_KT - 1)

    def out_chunk(c, carry):
        r0 = pl.multiple_of(c * ROW_CHUNK, ROW_CHUNK)
        z = _gelu(ya[pl.ds(r0, ROW_CHUNK), :])
        o_a = z * _sigmoid(_dot(z.astype(BF16), wglu_ref[...]) + bglu_ref[...])
        ycat_ref[pl.ds(r0, ROW_CHUNK), 0:SSM_WIDTH] = o_a.astype(BF16)
        y_b = gu[pl.ds(r0, ROW_CHUNK), :] * sacc[pl.ds(r0, ROW_CHUNK), :]
        ycat_ref[pl.ds(r0, ROW_CHUNK), SSM_WIDTH:SSM_WIDTH + GMLP_WIDTH] = y_b.astype(BF16)

        ridx = lax.broadcasted_iota(jnp.int32, (ROW_CHUNK, 1), 0) + r0
        pos = base_pos + (ridx >> shift)
        for g, win in enumerate(POOL_WINDOWS):
            cl = g * POOL_GROUP_CH
            reach = (win - 1) * nb
            if reach < ROW_CHUNK:
                off = pl.multiple_of(past_rows - reach + r0, SUBLANES)
                ext = xp[pl.ds(off, ROW_CHUNK + reach), cl:cl + POOL_GROUP_CH]
                cur = ext[reach:, :]
                wsum, span = ext, 1
                while span < win:
                    n = wsum.shape[0] - span * nb
                    wsum = wsum[span * nb:, :] + wsum[:n, :]
                    span *= 2
            else:
                cur = xp[pl.ds(past_rows + r0, ROW_CHUNK), cl:cl + POOL_GROUP_CH]
                wsum = cur
                for j in range(1, win):
                    off = pl.multiple_of(past_rows - j * nb + r0, SUBLANES)
                    wsum = wsum + xp[pl.ds(off, ROW_CHUNK), cl:cl + POOL_GROUP_CH]
            cnt = jnp.minimum(pos + 1, win).astype(F32)
            d = wsum / cnt - cur
            y_c = _dot(d.astype(BF16), poolw_ref[g]) * pscale_ref[:, cl:cl + POOL_GROUP_CH]
            col = SSM_WIDTH + GMLP_WIDTH + cl
            ycat_ref[pl.ds(r0, ROW_CHUNK), col:col + POOL_GROUP_CH] = y_c.astype(BF16)
        return carry

    lax.fori_loop(0, n_chunks, out_chunk, 0)

    pool_out_ref[...] = xp[rows:rows + past_rows, :]
    if n_tiles > 1:
        xp[0:past_rows, :] = xp[rows:rows + past_rows, :]


def _branches_call(x, h0re, h0im, ppast, state_layer, p, layer, bs, *, nb, npos, start_pos, emit_vn):
    seq_major_in = x.ndim == 3
    n_rows = x.shape[0] * x.shape[1] if seq_major_in else x.shape[0]
    rows = nb * npos
    n_tiles = n_rows // rows
    past_rows = POOL_PAST * nb
    kern = functools.partial(_branches_kernel, nb=nb, npos=npos, n_tiles=n_tiles, start_pos=start_pos,
                             seq_major_in=seq_major_in, emit_vn=emit_vn)
    row_tile = lambda w: pl.BlockSpec((rows, w), lambda i: (i, 0))
    x_spec = pl.BlockSpec((nb, npos, D_MODEL), lambda i: (0, i, 0)) if seq_major_in else row_tile(D_MODEL)
    n_hbuf = SSM_KT if n_tiles > 1 else 1
    fixed = lambda r, w: pl.BlockSpec((r, w), lambda i: (0, 0))
    consts = (p['bbc'], p['a_re'], p['a_im'], p['ccc'], p['d_skip'],
              p['w_glu'], p['b_glu'], p['gln_g'], p['gln_b'], p['w_s'], bs, p['pool_w'], p['pool_scale'])
    return pl.pallas_call(
        kern,
        grid=(n_tiles,),
        in_specs=[x_spec] + [_layer_const(s, state_layer) for s in (h0re, h0im, ppast)]
                 + [_w_in_cols(p['w_in'], layer, 0)] + [_layer_const(c, layer) for c in consts],
        out_specs=[row_tile(BRANCH_COLS - GMLP_WIDTH), fixed(nb, SSM_COLS), fixed(nb, SSM_COLS),
                   fixed(past_rows, POOL_WIDTH)] + ([row_tile(GMLP_WIDTH)] if emit_vn else []),
        out_shape=[jax.ShapeDtypeStruct((n_rows, BRANCH_COLS - GMLP_WIDTH), BF16),
                   jax.ShapeDtypeStruct((nb, SSM_COLS), F32),
                   jax.ShapeDtypeStruct((nb, SSM_COLS), F32),
                   jax.ShapeDtypeStruct((past_rows, POOL_WIDTH), F32)]
                  + ([jax.ShapeDtypeStruct((n_rows, GMLP_WIDTH), F32)] if emit_vn else []),
        scratch_shapes=[pltpu.VMEM((rows, 2 * KT_COLS), F32)] * n_hbuf
                       + [pltpu.VMEM((rows, SSM_WIDTH), F32),
                        pltpu.VMEM((rows, SSM_WIDTH), F32),
                        pltpu.VMEM((rows, GMLP_WIDTH), F32),
                        pltpu.VMEM((rows, GMLP_WIDTH), BF16),
                        pltpu.VMEM((rows, GMLP_WIDTH), F32),
                        pltpu.VMEM((rows + past_rows, POOL_WIDTH), F32),
                        pltpu.VMEM((GMLP_HEADS, rows, rows), BF16),
                        pltpu.VMEM((SSM_KT, MXU_DIM, 2 * KT_COLS), BF16),
                        pltpu.VMEM((SSM_KT, 2 * KT_COLS, MXU_DIM), BF16)],
        compiler_params=pltpu.CompilerParams(dimension_semantics=("arbitrary",),
                                             vmem_limit_bytes=VMEM_LIMIT_BYTES),
        name="branches",
    )(x, h0re, h0im, ppast, p['w_in'], *consts)


def _merge_ffn_kernel(x_ref, ycat_ref, cpast_ref,
                      wga_ref, wgb_ref, bg_ref, wbr_ref, wo_ref, g1_ref, b1_ref,
                      wup_ref, cw_ref, cb_ref, wdn_ref, g2_ref, b2_ref,
                      x2_ref, cout_ref, tailbuf, act, *, nb, rows, seq_major_in, seq_major_out):
    tail = (FFN_CONV - 1) * nb
    assert rows >= tail

    @pl.when(pl.program_id(0) == 0)
    def _init():
        tailbuf[...] = cpast_ref[...]

    x = _load_rows_t_major(x_ref, 0, rows, nb, seq_major_in)
    xb = x.astype(BF16)
    yc = ycat_ref[...]
    bounds = (0, SSM_WIDTH, SSM_WIDTH + GMLP_WIDTH, BRANCH_COLS - GMLP_WIDTH)
    merged = None

    def gate_logits(c0, c1):
        parts = []
        for ref, base in ((wga_ref, 0), (wgb_ref, BRANCH_COLS)):
            lo, hi = max(c0, base) - base, min(c1, base + BRANCH_COLS) - base
            if lo < hi:
                parts.append(_dot(xb, ref[:, lo:hi]))
        return parts[0] if len(parts) == 1 else jnp.concatenate(parts, axis=1)

    for i in range(N_BRANCH):
        gate = _sigmoid(gate_logits(i * D_MODEL, (i + 1) * D_MODEL) + bg_ref[:, i * D_MODEL:(i + 1) * D_MODEL])
        br = _dot(yc[:, bounds[i]:bounds[i + 1]], wbr_ref[bounds[i]:bounds[i + 1], :])
        merged = gate * br if merged is None else merged + gate * br
    mix = _dot(merged.astype(BF16), wo_ref[...])
    x1 = _layer_norm(ALPHA * x + mix, g1_ref[...], b1_ref[...])

    x1b = x1.astype(BF16)
    for cc in range(D_FF // FF_COL_CHUNK):
        conv = []
        for part in range(2):
            lo = part * D_FF + cc * FF_COL_CHUNK
            cols = slice(lo, lo + FF_COL_CHUNK)
            h = _dot(x1b, wup_ref[:, cols])
            he = jnp.concatenate([tailbuf[:, cols], h], axis=0)
            tailbuf[:, cols] = h[rows - tail:rows, :]
            y = cb_ref[:, cols] + he[0:rows, :] * cw_ref[0:1, cols]
            y = y + he[nb:nb + rows, :] * cw_ref[1:2, cols]
            y = y + h * cw_ref[2:3, cols]
            conv.append(y)
        a = _gelu(conv[1]) * conv[0]
        act[:, cc * FF_COL_CHUNK:(cc + 1) * FF_COL_CHUNK] = a.astype(BF16)
    ff = _dot(act[...], wdn_ref[...])
    x2 = _layer_norm(ALPHA * x1 + ff, g2_ref[...], b2_ref[...])
    if seq_major_out:
        x2_ref[...] = jnp.swapaxes(x2.reshape(rows // nb, nb, D_MODEL), 0, 1)
    else:
        x2_ref[...] = x2
    cout_ref[...] = tailbuf[...]


def _merge_ffn_call(x, ycat, cpast, state_layer, p, layer, *, nb, rows, seq_major_out):
    seq_major_in = x.ndim == 3
    n_rows = ycat.shape[0]
    tail = (FFN_CONV - 1) * nb
    kern = functools.partial(_merge_ffn_kernel, nb=nb, rows=rows,
                             seq_major_in=seq_major_in, seq_major_out=seq_major_out)
    consts = (p['b_gate'], p['w_br'], p['w_o'], p['ln1_g'], p['ln1_b'],
              p['w_up'], p['conv_w'], p['conv_b'], p['w_down'], p['ln2_g'], p['ln2_b'])
    t_major_spec = pl.BlockSpec((rows, D_MODEL), lambda i: (i, 0))
    seq_major_spec = pl.BlockSpec((nb, rows // nb, D_MODEL), lambda i: (0, i, 0))
    return pl.pallas_call(
        kern,
        grid=(n_rows // rows,),
        in_specs=[seq_major_spec if seq_major_in else t_major_spec,
                  pl.BlockSpec((rows, BRANCH_COLS - GMLP_WIDTH), lambda i: (i, 0)),
                  _layer_const(cpast, state_layer),
                  _w_in_cols(p['w_in'], layer, 1), _w_in_cols(p['w_in'], layer, 2)]
                 + [_layer_const(c, layer) for c in consts],
        out_specs=[seq_major_spec if seq_major_out else t_major_spec,
                   pl.BlockSpec((tail, F2), lambda i: (0, 0))],
        out_shape=[jax.ShapeDtypeStruct((nb, n_rows // nb, D_MODEL) if seq_major_out else (n_rows, D_MODEL), F32),
                   jax.ShapeDtypeStruct((tail, F2), F32)],
        scratch_shapes=[pltpu.VMEM((tail, F2), F32),
                        pltpu.VMEM((rows, D_FF), BF16)],
        compiler_params=pltpu.CompilerParams(dimension_semantics=("arbitrary",),
                                             vmem_limit_bytes=VMEM_LIMIT_BYTES),
        name="merge_convffn",
    )(x, ycat, cpast, p['w_in'], p['w_in'], *consts)


def _ssm_discretise(a_re, a_im, log_step, b_re, b_im):
    dt = jnp.exp(log_step)[..., None]
    mag = jnp.exp(a_re * dt)
    ab_re = mag * jnp.cos(a_im * dt)
    ab_im = mag * jnp.sin(a_im * dt)
    den = jnp.square(a_re) + jnp.square(a_im)
    nr = ab_re - 1.0
    k_re = (nr * a_re + ab_im * a_im) / den
    k_im = (ab_im * a_re - nr * a_im) / den
    bb_re = k_re[..., None] * b_re - k_im[..., None] * b_im
    bb_im = k_re[..., None] * b_im + k_im[..., None] * b_re
    return ab_re, ab_im, bb_re, bb_im


def _compact_in(m):
    gpt = SSM_GROUPS // SSM_KT
    return jnp.transpose(m, (0, 1, 3, 2)).reshape(-1, SSM_KT, gpt * SSM_GROUP_CH, SSM_STATE)


def _compact_out(m):
    gpt = SSM_GROUPS // SSM_KT
    m = m.reshape(-1, SSM_KT, gpt, SSM_GROUP_CH, SSM_STATE)
    return jnp.transpose(m, (0, 1, 4, 2, 3)).reshape(-1, SSM_KT, SSM_STATE, gpt * SSM_GROUP_CH)


def _gmlp_bias_rows(b_s, cl, nb):
    b = jnp.transpose(b_s[:, :, :cl], (0, 2, 1))
    return jnp.repeat(jnp.repeat(b, GMLP_HEAD_DIM, axis=2), nb, axis=1)


def _t_major(a):
    return jnp.transpose(a, (1, 0, 2)).reshape(a.shape[0] * a.shape[1], a.shape[2])


def _seq_major(a, n_seq):
    return jnp.transpose(a.reshape(a.shape[0] // n_seq, n_seq, a.shape[1]), (1, 0, 2))


def _group_layer(x, st_re, st_im, st_pool, st_conv, state_layer, p, layer, bs, *, nb, npos, start_pos, ffn_rows,
                 seq_major_out=False, emit_vn=False):
    ycat, h_re, h_im, pool_new, *vn = _branches_call(x, st_re, st_im, st_pool, state_layer, p, layer, bs,
                                                      nb=nb, npos=npos, start_pos=start_pos, emit_vn=emit_vn)
    x2, conv_new = _merge_ffn_call(x, ycat, st_conv, state_layer, p, layer, nb=nb, rows=ffn_rows,
                                   seq_major_out=seq_major_out)
    return (x2, h_re, h_im, pool_new, conv_new, *vn)


def kernel(x_prompt, x_sample, state_ssm_re, state_ssm_im, state_pool, state_ffn_conv, w_in, b_gate, ssm_a_re, ssm_a_im, ssm_log_step, ssm_b_re, ssm_b_im, ssm_c_re, ssm_c_im, ssm_d, ssm_w_glu, ssm_b_glu, gmlp_ln_g, gmlp_ln_b, gmlp_w_s, gmlp_b_s, pool_w, pool_scale, w_br_ssm, w_br_gmlp, w_br_pool, w_o, ln1_g, ln1_b, ffn_w_up, ffn_conv_w, ffn_conv_b, ffn_w_down, ln2_g, ln2_b):
    n_p, t_p, _ = x_prompt.shape
    n_s, t_s, _ = x_sample.shape
    hp = x_prompt
    hs = _t_major(x_sample)
    npos_p = TILE_ROWS // n_p
    npos_s = TILE_ROWS // n_s
    assert npos_p == GMLP_CHUNK and npos_s == t_s and t_p % npos_p == 0

    row = lambda v: v.reshape(DEPTH, 1, -1)
    ab_re, ab_im, bb_re, bb_im = _ssm_discretise(ssm_a_re, ssm_a_im, ssm_log_step, ssm_b_re, ssm_b_im)
    p = dict(
        w_in=w_in.astype(BF16),
        b_gate=row(b_gate),
        bbc=jnp.stack([_compact_in(bb_re), _compact_in(bb_im)], axis=2).astype(BF16),
        a_re=row(ab_re), a_im=row(ab_im),
        ccc=jnp.stack([_compact_out(ssm_c_re), -_compact_out(ssm_c_im)], axis=2).astype(BF16),
        d_skip=row(ssm_d),
        w_glu=ssm_w_glu.astype(BF16), b_glu=row(ssm_b_glu),
        gln_g=row(gmlp_ln_g), gln_b=row(gmlp_ln_b),
        w_s=gmlp_w_s, pool_w=pool_w.astype(BF16), pool_scale=row(pool_scale),
        w_br=jnp.concatenate([w_br_ssm, w_br_gmlp, w_br_pool], axis=1).astype(BF16),
        w_o=w_o.astype(BF16), ln1_g=row(ln1_g), ln1_b=row(ln1_b),
        w_up=ffn_w_up.astype(BF16), conv_w=ffn_conv_w, conv_b=row(ffn_conv_b),
        w_down=ffn_w_down.astype(BF16), ln2_g=row(ln2_g), ln2_b=row(ln2_b),
    )
    bs_p = _gmlp_bias_rows(gmlp_b_s, npos_p, n_p)
    bs_s = _gmlp_bias_rows(gmlp_b_s, npos_s, n_s)
    zeros_p = (jnp.zeros((1, n_p, SSM_COLS), F32), jnp.zeros((1, n_p, SSM_COLS), F32),
               jnp.zeros((1, POOL_PAST * n_p, POOL_WIDTH), F32), jnp.zeros((1, (FFN_CONV - 1) * n_p, F2), F32))
    ssm_re_s = state_ssm_re.reshape(DEPTH, n_s, SSM_COLS)
    ssm_im_s = state_ssm_im.reshape(DEPTH, n_s, SSM_COLS)

    pool_t = jnp.transpose(state_pool, (0, 2, 1, 3)).reshape(DEPTH, POOL_PAST * n_s, POOL_WIDTH)
    conv_t = jnp.transpose(state_ffn_conv, (0, 2, 1, 3)).reshape(DEPTH, (FFN_CONV - 1) * n_s, F2)

    outs = [[] for _ in range(9)]
    for l in range(DEPTH):
        hp, a_re_n, a_im_n, pool_n, conv_n = _group_layer(
            hp, *zeros_p, 0, p, l, bs_p, nb=n_p, npos=npos_p, start_pos=0, ffn_rows=2 * ROW_CHUNK,
            seq_major_out=(l == DEPTH - 1))
        hs, b_re_n, b_im_n, pool_m, conv_m, v_n = _group_layer(
            hs, ssm_re_s, ssm_im_s, pool_t, conv_t, l, p, l, bs_s, nb=n_s, npos=npos_s, start_pos=PAST_LEN, ffn_rows=ROW_CHUNK, emit_vn=True)
        for o, v in zip(outs, (a_re_n, a_im_n, pool_n, conv_n, b_re_n, b_im_n, v_n, pool_m, conv_m)):
            o.append(v)

    p_re, p_im, p_pool, p_conv, s_re, s_im, s_v, s_pool, s_conv = (jnp.stack(o) for o in outs)
    ssm = lambda a, n_seq: a.reshape(DEPTH, n_seq, SSM_GROUPS, SSM_STATE)
    per_seq = lambda a, n_seq: jnp.transpose(a.reshape(DEPTH, -1, n_seq, a.shape[-1]), (0, 2, 1, 3))
    return (hp, _seq_major(hs, n_s),
            ssm(p_re, n_p), ssm(p_im, n_p), per_seq(p_pool, n_p), per_seq(p_conv, n_p),
            ssm(s_re, n_s), ssm(s_im, n_s), per_seq(s_v, n_s), per_seq(s_pool, n_s), per_seq(s_conv, n_s))
```

```python
import functools
import math

import jax
import jax.numpy as jnp
from jax import lax
from jax.experimental import pallas as pl
from jax.experimental.pallas import tpu as pltpu

F32 = jnp.float32
BF16 = jnp.bfloat16

D_MODEL = 1024
DEPTH = 2
PAST_LEN = 16384
SSM_GROUP_CH = 16
SSM_WIDTH = 512
SSM_GROUPS = 32
SSM_STATE = 64
SSM_COLS = SSM_GROUPS * SSM_STATE
GMLP_HEADS = 4
GMLP_CHUNK = 128
GMLP_WIDTH = 256
GMLP_HEAD_DIM = 64
POOL_WINDOWS = (2, 4, 8, 16)
POOL_WIDTH = 512
POOL_GROUP_CH = 128
POOL_PAST = 15
N_BRANCH = 3
BRANCH_COLS = SSM_WIDTH + 2 * GMLP_WIDTH + POOL_WIDTH
D_FF = 2816
F2 = 2 * D_FF
FFN_CONV = 3
LN_EPS = 1e-5
ALPHA = (2 * DEPTH) ** 0.25

SUBLANES = 8
MXU_DIM = 256
VMEM_LIMIT_BYTES = 60 * 1024 * 1024

TILE_ROWS = 1024
ROW_CHUNK = 256
PROJ_CHUNK = 512
SSM_CHUNK = 512
SCAN_PASSES = 1
SCAN_BLOCKS = 2
SSM_KT = SSM_WIDTH // MXU_DIM
KT_COLS = SSM_COLS // SSM_KT
FF_COL_CHUNK = MXU_DIM
HEADS_PER_PASS = GMLP_HEADS // (SSM_KT * SCAN_PASSES)
assert HEADS_PER_PASS * SSM_KT * SCAN_PASSES == GMLP_HEADS


def _gelu(x):
    c = math.sqrt(2.0 / math.pi)
    return 0.5 * x * (1.0 + jnp.tanh(c * (x + 0.044715 * (x * x * x))))


def _sigmoid(x):
    return 1.0 / (1.0 + jnp.exp(-x))


def _layer_norm(x, g, b):
    mu = jnp.mean(x, axis=-1, keepdims=True)
    xc = x - mu
    var = jnp.mean(xc * xc, axis=-1, keepdims=True)
    return xc * lax.rsqrt(var + LN_EPS) * g + b


def _dot(a, b):
    return jnp.dot(a, b, preferred_element_type=F32)


def _load_rows_t_major(x_ref, r0, n, nb, seq_major):
    if not seq_major:
        return x_ref[pl.ds(r0, n), :]
    p0 = r0 // nb if isinstance(r0, int) else pl.multiple_of(r0 >> int(math.log2(nb)), SUBLANES)
    blk = x_ref[:, pl.ds(p0, n // nb), :]
    return jnp.swapaxes(blk, 0, 1).reshape(n, blk.shape[-1])


def _w_in_cols(w_in, layer, block):
    return pl.BlockSpec((None, w_in.shape[1], BRANCH_COLS), lambda i: (layer, 0, block),
                        pipeline_mode=pl.Buffered(1))


def _layer_const(a, layer):
    nd = a.ndim - 1
    return pl.BlockSpec((None,) + a.shape[1:], lambda i: (layer,) + (0,) * nd,
                        pipeline_mode=pl.Buffered(1))


def _expand_mix(ws_ref, wk, rows, shift, nb):
    cl = GMLP_CHUNK
    e = (lax.broadcasted_iota(jnp.int32, (rows, cl), 0) >> shift
         == lax.broadcasted_iota(jnp.int32, (rows, cl), 1)).astype(BF16)
    tril = (lax.broadcasted_iota(jnp.int32, (cl, cl), 0)
            >= lax.broadcasted_iota(jnp.int32, (cl, cl), 1))
    for h in range(GMLP_HEADS):
        w = jnp.where(tril, ws_ref[h], 0.0).astype(BF16)
        rows_w = _dot(e, w).astype(BF16)
        for c0 in range(0, rows, MXU_DIM):
            et = (lax.broadcasted_iota(jnp.int32, (cl, MXU_DIM), 0)
                  == (lax.broadcasted_iota(jnp.int32, (cl, MXU_DIM), 1) + c0) >> shift).astype(BF16)
            full = _dot(rows_w, et)
            same_seq = ((lax.broadcasted_iota(jnp.int32, (rows, MXU_DIM), 0) & (nb - 1))
                        == ((lax.broadcasted_iota(jnp.int32, (rows, MXU_DIM), 1) + c0) & (nb - 1)))
            wk[h, :, c0:c0 + MXU_DIM] = jnp.where(same_seq, full, 0.0).astype(BF16)


def _expand_ssm(bbc_ref, ccc_ref, bb, cc):
    n_in, n_st = KT_COLS // SSM_STATE * SSM_GROUP_CH, KT_COLS
    iota = lambda shape, d: lax.broadcasted_iota(jnp.int32, shape, d)
    log_n, log_c = int(math.log2(SSM_STATE)), int(math.log2(SSM_GROUP_CH))
    spread = ((iota((SSM_STATE, n_st), 1) & (SSM_STATE - 1)) == iota((SSM_STATE, n_st), 0)).astype(BF16)
    same_in = (iota((n_in, n_st), 0) >> log_c) == (iota((n_in, n_st), 1) >> log_n)
    gather = ((iota((n_st, SSM_STATE), 0) & (SSM_STATE - 1)) == iota((n_st, SSM_STATE), 1)).astype(BF16)
    same_out = (iota((n_st, n_in), 0) >> log_n) == (iota((n_st, n_in), 1) >> log_c)
    for kt in range(SSM_KT):
        for part in range(2):
            full = _dot(bbc_ref[kt, part], spread)
            bb[kt, :, part * n_st:(part + 1) * n_st] = jnp.where(same_in, full, 0.0).astype(BF16)
            full = _dot(gather, ccc_ref[kt, part])
            cc[kt, part * n_st:(part + 1) * n_st, :] = jnp.where(same_out, full, 0.0).astype(BF16)


def _branches_kernel(x_ref, h0re_ref, h0im_ref, ppast_ref,
                     w_in_ref, bbc_ref, are_ref, aim_ref, ccc_ref, dsk_ref,
                     wglu_ref, bglu_ref, lng_ref, lnb_ref, ws_ref, bs_ref,
                     poolw_ref, pscale_ref,
                     ycat_ref, hcre, hcim, pool_out_ref,
                     *scratch, nb, npos, n_tiles, start_pos, seq_major_in, emit_vn):
    vn_out_ref = scratch[0] if emit_vn else None
    *hbufs, us, ya, gu, vnb, sacc, xp, wk, bb_ref, cc_ref = scratch[1:] if emit_vn else scratch
    rows = nb * npos
    n_chunks = rows // ROW_CHUNK
    past_rows = POOL_PAST * nb
    tile = pl.program_id(0)
    shift = int(math.log2(nb))
    slabs_per_block = rows // SUBLANES // SCAN_BLOCKS
    steps = min(npos, slabs_per_block)
    groups_per_block = slabs_per_block // steps
    assert groups_per_block == 1 or steps == npos
    mix_rows = rows // SCAN_BLOCKS

    @pl.when(tile == 0)
    def _init():
        hcre[...] = h0re_ref[...]
        hcim[...] = h0im_ref[...]
        xp[0:past_rows, :] = ppast_ref[...]
        _expand_mix(ws_ref, wk, rows, shift, nb)
        _expand_ssm(bbc_ref, ccc_ref, bb_ref, cc_ref)

    base_pos = start_pos + tile * npos

    def proj_chunk(c, carry):
        r0 = pl.multiple_of(c * PROJ_CHUNK, PROJ_CHUNK)
        xb = _load_rows_t_major(x_ref, r0, PROJ_CHUNK, nb, seq_major_in).astype(BF16)
        pa = _dot(xb, w_in_ref[...])
        us[pl.ds(r0, PROJ_CHUNK), :] = pa[:, 0:SSM_WIDTH]
        gu[pl.ds(r0, PROJ_CHUNK), :] = _gelu(pa[:, SSM_WIDTH:SSM_WIDTH + GMLP_WIDTH])
        v = _gelu(pa[:, SSM_WIDTH + GMLP_WIDTH:SSM_WIDTH + 2 * GMLP_WIDTH])
        v = _layer_norm(v, lng_ref[...], lnb_ref[...])
        if emit_vn:
            vn_out_ref[pl.ds(r0, PROJ_CHUNK), :] = v
        vnb[pl.ds(r0, PROJ_CHUNK), :] = v.astype(BF16)
        sacc[pl.ds(r0, PROJ_CHUNK), :] = bs_ref[pl.ds(r0, PROJ_CHUNK), :]
        xp[pl.ds(past_rows + r0, PROJ_CHUNK), :] = pa[:, SSM_WIDTH + 2 * GMLP_WIDTH:BRANCH_COLS]
        return carry

    lax.fori_loop(0, rows // PROJ_CHUNK, proj_chunk, 0)

    n_hbuf = len(hbufs)
    overlap = n_hbuf > 1

    def bu_rows(kt, r0):
        ub = us[pl.ds(r0, SSM_CHUNK), kt * MXU_DIM:(kt + 1) * MXU_DIM].astype(BF16)
        hbufs[kt % n_hbuf][pl.ds(r0, SSM_CHUNK), :] = _dot(ub, bb_ref[kt])

    def y_rows(kt, r0):
        hb = hbufs[kt % n_hbuf][pl.ds(r0, SSM_CHUNK), :].astype(BF16)
        u = us[pl.ds(r0, SSM_CHUNK), kt * MXU_DIM:(kt + 1) * MXU_DIM]
        y = _dot(hb, cc_ref[kt]) + dsk_ref[:, kt * MXU_DIM:(kt + 1) * MXU_DIM] * u
        ya[pl.ds(r0, SSM_CHUNK), kt * MXU_DIM:(kt + 1) * MXU_DIM] = y

    def chunk_loop(fn, kt):
        def body(c, carry):
            fn(kt, pl.multiple_of(c * SSM_CHUNK, SSM_CHUNK))
            return carry
        lax.fori_loop(0, rows // SSM_CHUNK, body, 0)

    if overlap:
        assert rows // SSM_CHUNK == SCAN_BLOCKS
        chunk_loop(bu_rows, 0)
    for kt in range(SSM_KT):
        c_lo = kt * KT_COLS
        hbuf = hbufs[kt % n_hbuf]
        if not overlap:
            chunk_loop(bu_rows, kt)

        width = KT_COLS // SCAN_PASSES
        for ch in range(SCAN_PASSES):
            lo = ch * width
            if overlap:
                heads_kt = GMLP_HEADS // SSM_KT
                mix_heads = range(kt * heads_kt, (kt + 1) * heads_kt) if ch == SCAN_PASSES - 1 else ()
            else:
                first = (kt * SCAN_PASSES + ch) * HEADS_PER_PASS
                mix_heads = range(first, first + HEADS_PER_PASS)
            a_r = jnp.broadcast_to(are_ref[:, c_lo + lo:c_lo + lo + width], (SUBLANES, width))
            a_i = jnp.broadcast_to(aim_ref[:, c_lo + lo:c_lo + lo + width], (SUBLANES, width))

            def scan_block(j, carry, kt=kt, hbuf=hbuf, ch=ch, lo=lo, a_r=a_r, a_i=a_i, c_lo=c_lo, mix_heads=mix_heads):
                for q in range(groups_per_block):
                    if groups_per_block == 1 and steps < npos:
                        t0, b0 = j * steps, 0
                    else:
                        t0, b0 = 0, (j * groups_per_block + q) * SUBLANES
                    h_r = hcre[pl.ds(b0, SUBLANES), c_lo + lo:c_lo + lo + width]
                    h_i = hcim[pl.ds(b0, SUBLANES), c_lo + lo:c_lo + lo + width]
                    for t in range(steps):
                        row = (t0 + t) * nb + b0
                        b_r = hbuf[pl.ds(row, SUBLANES), lo:lo + width]
                        b_i = hbuf[pl.ds(row, SUBLANES), KT_COLS + lo:KT_COLS + lo + width]
                        h_r, h_i = (a_r * h_r - a_i * h_i + b_r,
                                    a_r * h_i + a_i * h_r + b_i)
                        hbuf[pl.ds(row, SUBLANES), lo:lo + width] = h_r
                        hbuf[pl.ds(row, SUBLANES), KT_COLS + lo:KT_COLS + lo + width] = h_i
                    hcre[pl.ds(b0, SUBLANES), c_lo + lo:c_lo + lo + width] = h_r
                    hcim[pl.ds(b0, SUBLANES), c_lo + lo:c_lo + lo + width] = h_i
                if mix_heads:
                    m0 = j * mix_rows
                    head = lax.broadcasted_iota(jnp.int32, (1, GMLP_WIDTH), 1) // GMLP_HEAD_DIM
                    acc = sacc[pl.ds(m0, mix_rows), :]
                    for mh in mix_heads:
                        k_hi = (j + 1) * mix_rows
                        mixed = _dot(wk[mh, pl.ds(m0, mix_rows), 0:k_hi], vnb[0:k_hi, :])
                        acc = acc + jnp.where(head == mh, mixed, 0.0)
                    sacc[pl.ds(m0, mix_rows), :] = acc
                if overlap and ch == 0:
                    r0 = j * SSM_CHUNK
                    if kt + 1 < SSM_KT:
                        bu_rows(kt + 1, r0)
                    if kt > 0:
                        y_rows(kt - 1, r0)
                return carry

            for j in range(SCAN_BLOCKS):
                scan_block(j, 0)

        if not overlap:
            chunk_loop(y_rows, kt)
    if overlap:
        chunk_loop(y_rows, SSM_KT - 1)

    def out_chunk(c, carry):
        r0 = pl.multiple_of(c * ROW_CHUNK, ROW_CHUNK)
        z = _gelu(ya[pl.ds(r0, ROW_CHUNK), :])
        o_a = z * _sigmoid(_dot(z.astype(BF16), wglu_ref[...]) + bglu_ref[...])
        ycat_ref[pl.ds(r0, ROW_CHUNK), 0:SSM_WIDTH] = o_a.astype(BF16)
        y_b = gu[pl.ds(r0, ROW_CHUNK), :] * sacc[pl.ds(r0, ROW_CHUNK), :]
        ycat_ref[pl.ds(r0, ROW_CHUNK), SSM_WIDTH:SSM_WIDTH + GMLP_WIDTH] = y_b.astype(BF16)

        ridx = lax.broadcasted_iota(jnp.int32, (ROW_CHUNK, 1), 0) + r0
        pos = base_pos + (ridx >> shift)
        for g, win in enumerate(POOL_WINDOWS):
            cl = g * POOL_GROUP_CH
            reach = (win - 1) * nb
            if reach < ROW_CHUNK:
                off = pl.multiple_of(past_rows - reach + r0, SUBLANES)
                ext = xp[pl.ds(off, ROW_CHUNK + reach), cl:cl + POOL_GROUP_CH]
                cur = ext[reach:, :]
                wsum, span = ext, 1
                while span < win:
                    n = wsum.shape[0] - span * nb
                    wsum = wsum[span * nb:, :] + wsum[:n, :]
                    span *= 2
            else:
                cur = xp[pl.ds(past_rows + r0, ROW_CHUNK), cl:cl + POOL_GROUP_CH]
                wsum = cur
                for j in range(1, win):
                    off = pl.multiple_of(past_rows - j * nb + r0, SUBLANES)
                    wsum = wsum + xp[pl.ds(off, ROW_CHUNK), cl:cl + POOL_GROUP_CH]
            cnt = jnp.minimum(pos + 1, win).astype(F32)
            d = wsum / cnt - cur
            y_c = _dot(d.astype(BF16), poolw_ref[g]) * pscale_ref[:, cl:cl + POOL_GROUP_CH]
            col = SSM_WIDTH + GMLP_WIDTH + cl
            ycat_ref[pl.ds(r0, ROW_CHUNK), col:col + POOL_GROUP_CH] = y_c.astype(BF16)
        return carry

    lax.fori_loop(0, n_chunks, out_chunk, 0)

    pool_out_ref[...] = xp[rows:rows + past_rows, :]
    if n_tiles > 1:
        xp[0:past_rows, :] = xp[rows:rows + past_rows, :]


def _branches_call(x, h0re, h0im, ppast, state_layer, p, layer, bs, *, nb, npos, start_pos, emit_vn):
    seq_major_in = x.ndim == 3
    n_rows = x.shape[0] * x.shape[1] if seq_major_in else x.shape[0]
    rows = nb * npos
    n_tiles = n_rows // rows
    past_rows = POOL_PAST * nb
    kern = functools.partial(_branches_kernel, nb=nb, npos=npos, n_tiles=n_tiles, start_pos=start_pos,
                             seq_major_in=seq_major_in, emit_vn=emit_vn)
    row_tile = lambda w: pl.BlockSpec((rows, w), lambda i: (i, 0))
    x_spec = pl.BlockSpec((nb, npos, D_MODEL), lambda i: (0, i, 0)) if seq_major_in else row_tile(D_MODEL)
    n_hbuf = SSM_KT if n_tiles > 1 else 1
    fixed = lambda r, w: pl.BlockSpec((r, w), lambda i: (0, 0))
    consts = (p['bbc'], p['a_re'], p['a_im'], p['ccc'], p['d_skip'],
              p['w_glu'], p['b_glu'], p['gln_g'], p['gln_b'], p['w_s'], bs, p['pool_w'], p['pool_scale'])
    return pl.pallas_call(
        kern,
        grid=(n_tiles,),
        in_specs=[x_spec] + [_layer_const(s, state_layer) for s in (h0re, h0im, ppast)]
                 + [_w_in_cols(p['w_in'], layer, 0)] + [_layer_const(c, layer) for c in consts],
        out_specs=[row_tile(BRANCH_COLS - GMLP_WIDTH), fixed(nb, SSM_COLS), fixed(nb, SSM_COLS),
                   fixed(past_rows, POOL_WIDTH)] + ([row_tile(GMLP_WIDTH)] if emit_vn else []),
        out_shape=[jax.ShapeDtypeStruct((n_rows, BRANCH_COLS - GMLP_WIDTH), BF16),
                   jax.ShapeDtypeStruct((nb, SSM_COLS), F32),
                   jax.ShapeDtypeStruct((nb, SSM_COLS), F32),
                   jax.ShapeDtypeStruct((past_rows, POOL_WIDTH), F32)]
                  + ([jax.ShapeDtypeStruct((n_rows, GMLP_WIDTH), F32)] if emit_vn else []),
        scratch_shapes=[pltpu.VMEM((rows, 2 * KT_COLS), F32)] * n_hbuf
                       + [pltpu.VMEM((rows, SSM_WIDTH), F32),
                        pltpu.VMEM((rows, SSM_WIDTH), F32),
                        pltpu.VMEM((rows, GMLP_WIDTH), F32),
                        pltpu.VMEM((rows, GMLP_WIDTH), BF16),
                        pltpu.VMEM((rows, GMLP_WIDTH), F32),
                        pltpu.VMEM((rows + past_rows, POOL_WIDTH), F32),
                        pltpu.VMEM((GMLP_HEADS, rows, rows), BF16),
                        pltpu.VMEM((SSM_KT, MXU_DIM, 2 * KT_COLS), BF16),
                        pltpu.VMEM((SSM_KT, 2 * KT_COLS, MXU_DIM), BF16)],
        compiler_params=pltpu.CompilerParams(dimension_semantics=("arbitrary",),
                                             vmem_limit_bytes=VMEM_LIMIT_BYTES),
        name="branches",
    )(x, h0re, h0im, ppast, p['w_in'], *consts)


def _merge_ffn_kernel(x_ref, ycat_ref, cpast_ref,
                      wga_ref, wgb_ref, bg_ref, wbr_ref, wo_ref, g1_ref, b1_ref,
                      wup_ref, cw_ref, cb_ref, wdn_ref, g2_ref, b2_ref,
                      x2_ref, tailbuf, act, *, nb, rows, seq_major_in, seq_major_out):
    tail = (FFN_CONV - 1) * nb
    assert rows >= tail

    @pl.when(pl.program_id(0) == 0)
    def _init():
        tailbuf[...] = cpast_ref[...]

    x = _load_rows_t_major(x_ref, 0, rows, nb, seq_major_in)
    xb = x.astype(BF16)
    yc = ycat_ref[...]
    bounds = (0, SSM_WIDTH, SSM_WIDTH + GMLP_WIDTH, BRANCH_COLS - GMLP_WIDTH)
    merged = None

    def gate_logits(c0, c1):
        parts = []
        for ref, base in ((wga_ref, 0), (wgb_ref, BRANCH_COLS)):
            lo, hi = max(c0, base) - base, min(c1, base + BRANCH_COLS) - base
            if lo < hi:
                parts.append(_dot(xb, ref[:, lo:hi]))
        return parts[0] if len(parts) == 1 else jnp.concatenate(parts, axis=1)

    for i in range(N_BRANCH):
        gate = _sigmoid(gate_logits(i * D_MODEL, (i + 1) * D_MODEL) + bg_ref[:, i * D_MODEL:(i + 1) * D_MODEL])
        br = _dot(yc[:, bounds[i]:bounds[i + 1]], wbr_ref[bounds[i]:bounds[i + 1], :])
        merged = gate * br if merged is None else merged + gate * br
    mix = _dot(merged.astype(BF16), wo_ref[...])
    x1 = _layer_norm(ALPHA * x + mix, g1_ref[...], b1_ref[...])

    x1b = x1.astype(BF16)
    for cc in range(D_FF // FF_COL_CHUNK):
        conv = []
        for part in range(2):
            lo = part * D_FF + cc * FF_COL_CHUNK
            cols = slice(lo, lo + FF_COL_CHUNK)
            h = _dot(x1b, wup_ref[:, cols])
            he = jnp.concatenate([tailbuf[:, cols], h], axis=0)
            tailbuf[:, cols] = h[rows - tail:rows, :]
            y = cb_ref[:, cols] + he[0:rows, :] * cw_ref[0:1, cols]
            y = y + he[nb:nb + rows, :] * cw_ref[1:2, cols]
            y = y + h * cw_ref[2:3, cols]
            conv.append(y)
        a = _gelu(conv[1]) * conv[0]
        act[:, cc * FF_COL_CHUNK:(cc + 1) * FF_COL_CHUNK] = a.astype(BF16)
    ff = _dot(act[...], wdn_ref[...])
    x2 = _layer_norm(ALPHA * x1 + ff, g2_ref[...], b2_ref[...])
    if seq_major_out:
        x2_ref[...] = jnp.swapaxes(x2.reshape(rows // nb, nb, D_MODEL), 0, 1)
    else:
        x2_ref[...] = x2


def _merge_ffn_call(x, ycat, cpast, state_layer, p, layer, *, nb, rows, seq_major_out):
    seq_major_in = x.ndim == 3
    n_rows = ycat.shape[0]
    tail = (FFN_CONV - 1) * nb
    kern = functools.partial(_merge_ffn_kernel, nb=nb, rows=rows,
                             seq_major_in=seq_major_in, seq_major_out=seq_major_out)
    consts = (p['b_gate'], p['w_br'], p['w_o'], p['ln1_g'], p['ln1_b'],
              p['w_up'], p['conv_w'], p['conv_b'], p['w_down'], p['ln2_g'], p['ln2_b'])
    t_major_spec = pl.BlockSpec((rows, D_MODEL), lambda i: (i, 0))
    seq_major_spec = pl.BlockSpec((nb, rows // nb, D_MODEL), lambda i: (0, i, 0))
    return pl.pallas_call(
        kern,
        grid=(n_rows // rows,),
        in_specs=[seq_major_spec if seq_major_in else t_major_spec,
                  pl.BlockSpec((rows, BRANCH_COLS - GMLP_WIDTH), lambda i: (i, 0)),
                  _layer_const(cpast, state_layer),
                  _w_in_cols(p['w_in'], layer, 1), _w_in_cols(p['w_in'], layer, 2)]
                 + [_layer_const(c, layer) for c in consts],
        out_specs=[seq_major_spec if seq_major_out else t_major_spec,
                   pl.BlockSpec((tail, F2), lambda i: (0, 0))],
        out_shape=[jax.ShapeDtypeStruct((nb, n_rows // nb, D_MODEL) if seq_major_out else (n_rows, D_MODEL), F32),
                   jax.ShapeDtypeStruct((tail, F2), F32)],
        scratch_shapes=[pltpu.VMEM((rows, D_FF), BF16)],
        compiler_params=pltpu.CompilerParams(dimension_semantics=("arbitrary",),
                                             vmem_limit_bytes=VMEM_LIMIT_BYTES),
        name="merge_convffn",
    )(x, ycat, cpast, p['w_in'], p['w_in'], *consts)


def _ssm_discretise(a_re, a_im, log_step, b_re, b_im):
    dt = jnp.exp(log_step)[..., None]
    mag = jnp.exp(a_re * dt)
    ab_re = mag * jnp.cos(a_im * dt)
    ab_im = mag * jnp.sin(a_im * dt)
    den = jnp.square(a_re) + jnp.square(a_im)
    nr = ab_re - 1.0
    k_re = (nr * a_re + ab_im * a_im) / den
    k_im = (ab_im * a_re - nr * a_im) / den
    bb_re = k_re[..., None] * b_re - k_im[..., None] * b_im
    bb_im = k_re[..., None] * b_im + k_im[..., None] * b_re
    return ab_re, ab_im, bb_re, bb_im


def _compact_in(m):
    gpt = SSM_GROUPS // SSM_KT
    return jnp.transpose(m, (0, 1, 3, 2)).reshape(-1, SSM_KT, gpt * SSM_GROUP_CH, SSM_STATE)


def _compact_out(m):
    gpt = SSM_GROUPS // SSM_KT
    m = m.reshape(-1, SSM_KT, gpt, SSM_GROUP_CH, SSM_STATE)
    return jnp.transpose(m, (0, 1, 4, 2, 3)).reshape(-1, SSM_KT, SSM_STATE, gpt * SSM_GROUP_CH)


def _gmlp_bias_rows(b_s, cl, nb):
    b = jnp.transpose(b_s[:, :, :cl], (0, 2, 1))
    return jnp.repeat(jnp.repeat(b, GMLP_HEAD_DIM, axis=2), nb, axis=1)


def _t_major(a):
    return jnp.transpose(a, (1, 0, 2)).reshape(a.shape[0] * a.shape[1], a.shape[2])


def _seq_major(a, n_seq):
    return jnp.transpose(a.reshape(a.shape[0] // n_seq, n_seq, a.shape[1]), (1, 0, 2))


def _group_layer(x, st_re, st_im, st_pool, st_conv, state_layer, p, layer, bs, *, nb, npos, start_pos, ffn_rows,
                 seq_major_out=False, emit_vn=False):
    ycat, h_re, h_im, pool_new, *vn = _branches_call(x, st_re, st_im, st_pool, state_layer, p, layer, bs,
                                                      nb=nb, npos=npos, start_pos=start_pos, emit_vn=emit_vn)
    x2, conv_new = _merge_ffn_call(x, ycat, st_conv, state_layer, p, layer, nb=nb, rows=ffn_rows,
                                   seq_major_out=seq_major_out)
    return (x2, h_re, h_im, pool_new, conv_new, *vn)


def kernel(x_prompt, x_sample, state_ssm_re, state_ssm_im, state_pool, state_ffn_conv, w_in, b_gate, ssm_a_re, ssm_a_im, ssm_log_step, ssm_b_re, ssm_b_im, ssm_c_re, ssm_c_im, ssm_d, ssm_w_glu, ssm_b_glu, gmlp_ln_g, gmlp_ln_b, gmlp_w_s, gmlp_b_s, pool_w, pool_scale, w_br_ssm, w_br_gmlp, w_br_pool, w_o, ln1_g, ln1_b, ffn_w_up, ffn_conv_w, ffn_conv_b, ffn_w_down, ln2_g, ln2_b):
    n_p, t_p, _ = x_prompt.shape
    n_s, t_s, _ = x_sample.shape
    hp = x_prompt
    hs = _t_major(x_sample)
    npos_p = TILE_ROWS // n_p
    npos_s = TILE_ROWS // n_s
    assert npos_p == GMLP_CHUNK and npos_s == t_s and t_p % npos_p == 0

    row = lambda v: v.reshape(DEPTH, 1, -1)
    ab_re, ab_im, bb_re, bb_im = _ssm_discretise(ssm_a_re, ssm_a_im, ssm_log_step, ssm_b_re, ssm_b_im)
    p = dict(
        w_in=w_in.astype(BF16),
        b_gate=row(b_gate),
        bbc=jnp.stack([_compact_in(bb_re), _compact_in(bb_im)], axis=2).astype(BF16),
        a_re=row(ab_re), a_im=row(ab_im),
        ccc=jnp.stack([_compact_out(ssm_c_re), -_compact_out(ssm_c_im)], axis=2).astype(BF16),
        d_skip=row(ssm_d),
        w_glu=ssm_w_glu.astype(BF16), b_glu=row(ssm_b_glu),
        gln_g=row(gmlp_ln_g), gln_b=row(gmlp_ln_b),
        w_s=gmlp_w_s, pool_w=pool_w.astype(BF16), pool_scale=row(pool_scale),
        w_br=jnp.concatenate([w_br_ssm, w_br_gmlp, w_br_pool], axis=1).astype(BF16),
        w_o=w_o.astype(BF16), ln1_g=row(ln1_g), ln1_b=row(ln1_b),
        w_up=ffn_w_up.astype(BF16), conv_w=ffn_conv_w, conv_b=row(ffn_conv_b),
        w_down=ffn_w_down.astype(BF16), ln2_g=row(ln2_g), ln2_b=row(ln2_b),
    )
    bs_p = _gmlp_bias_rows(gmlp_b_s, npos_p, n_p)
    bs_s = _gmlp_bias_rows(gmlp_b_s, npos_s, n_s)
    zeros_p = (jnp.zeros((1, n_p, SSM_COLS), F32), jnp.zeros((1, n_p, SSM_COLS), F32),
               jnp.zeros((1, POOL_PAST * n_p, POOL_WIDTH), F32), jnp.zeros((1, (FFN_CONV - 1) * n_p, F2), F32))
    ssm_re_s = state_ssm_re.reshape(DEPTH, n_s, SSM_COLS)
    ssm_im_s = state_ssm_im.reshape(DEPTH, n_s, SSM_COLS)

    pool_t = jnp.transpose(state_pool, (0, 2, 1, 3)).reshape(DEPTH, POOL_PAST * n_s, POOL_WIDTH)
    conv_t = jnp.transpose(state_ffn_conv, (0, 2, 1, 3)).reshape(DEPTH, (FFN_CONV - 1) * n_s, F2)

    outs = [[] for _ in range(9)]
    for l in range(DEPTH):
        hp, a_re_n, a_im_n, pool_n, conv_n = _group_layer(
            hp, *zeros_p, 0, p, l, bs_p, nb=n_p, npos=npos_p, start_pos=0, ffn_rows=2 * ROW_CHUNK,
            seq_major_out=(l == DEPTH - 1))
        hs, b_re_n, b_im_n, pool_m, conv_m, v_n = _group_layer(
            hs, ssm_re_s, ssm_im_s, pool_t, conv_t, l, p, l, bs_s, nb=n_s, npos=npos_s, start_pos=PAST_LEN, ffn_rows=2 * ROW_CHUNK, emit_vn=True)
        for o, v in zip(outs, (a_re_n, a_im_n, pool_n, conv_n, b_re_n, b_im_n, v_n, pool_m, conv_m)):
            o.append(v)

    p_re, p_im, p_pool, p_conv, s_re, s_im, s_v, s_pool, s_conv = (jnp.stack(o) for o in outs)
    ssm = lambda a, n_seq: a.reshape(DEPTH, n_seq, SSM_GROUPS, SSM_STATE)
    per_seq = lambda a, n_seq: jnp.transpose(a.reshape(DEPTH, -1, n_seq, a.shape[-1]), (0, 2, 1, 3))
    return (hp, _seq_major(hs, n_s),
            ssm(p_re, n_p), ssm(p_im, n_p), per_seq(p_pool, n_p), per_seq(p_conv, n_p),
            ssm(s_re, n_s), ssm(s_im, n_s), per_seq(s_v, n_s), per_seq(s_pool, n_s), per_seq(s_conv, n_s))
```

```python
import functools
import math

import jax
import jax.numpy as jnp
from jax import lax
from jax.experimental import pallas as pl
from jax.experimental.pallas import tpu as pltpu

F32 = jnp.float32
BF16 = jnp.bfloat16

D_MODEL = 1024
DEPTH = 2
PAST_LEN = 16384
SSM_GROUP_CH = 16
SSM_WIDTH = 512
SSM_GROUPS = 32
SSM_STATE = 64
SSM_COLS = SSM_GROUPS * SSM_STATE
GMLP_HEADS = 4
GMLP_CHUNK = 128
GMLP_WIDTH = 256
GMLP_HEAD_DIM = 64
POOL_WINDOWS = (2, 4, 8, 16)
POOL_WIDTH = 512
POOL_GROUP_CH = 128
POOL_PAST = 15
N_BRANCH = 3
BRANCH_COLS = SSM_WIDTH + 2 * GMLP_WIDTH + POOL_WIDTH
D_FF = 2816
F2 = 2 * D_FF
FFN_CONV = 3
LN_EPS = 1e-5
ALPHA = (2 * DEPTH) ** 0.25

SUBLANES = 8
MXU_DIM = 256
VMEM_LIMIT_BYTES = 60 * 1024 * 1024

TILE_ROWS = 1024
ROW_CHUNK = 256
PROJ_CHUNK = 512
SSM_CHUNK = 512
SCAN_PASSES = 1
SCAN_BLOCKS = 2
SSM_KT = SSM_WIDTH // MXU_DIM
KT_COLS = SSM_COLS // SSM_KT
FF_COL_CHUNK = MXU_DIM
HEADS_PER_PASS = GMLP_HEADS // (SSM_KT * SCAN_PASSES)
assert HEADS_PER_PASS * SSM_KT * SCAN_PASSES == GMLP_HEADS


def _gelu(x):
    c = math.sqrt(2.0 / math.pi)
    return 0.5 * x * (1.0 + jnp.tanh(c * (x + 0.044715 * (x * x * x))))


def _sigmoid(x):
    return 1.0 / (1.0 + jnp.exp(-x))


def _layer_norm(x, g, b):
    mu = jnp.mean(x, axis=-1, keepdims=True)
    xc = x - mu
    var = jnp.mean(xc * xc, axis=-1, keepdims=True)
    return xc * lax.rsqrt(var + LN_EPS) * g + b


def _dot(a, b):
    return jnp.dot(a, b, preferred_element_type=F32)


def _load_rows_t_major(x_ref, r0, n, nb, seq_major):
    if not seq_major:
        return x_ref[pl.ds(r0, n), :]
    p0 = r0 // nb if isinstance(r0, int) else pl.multiple_of(r0 >> int(math.log2(nb)), SUBLANES)
    blk = x_ref[:, pl.ds(p0, n // nb), :]
    return jnp.swapaxes(blk, 0, 1).reshape(n, blk.shape[-1])


def _w_in_cols(w_in, layer, block):
    return pl.BlockSpec((None, w_in.shape[1], BRANCH_COLS), lambda i: (layer, 0, block),
                        pipeline_mode=pl.Buffered(1))


def _layer_const(a, layer):
    nd = a.ndim - 1
    return pl.BlockSpec((None,) + a.shape[1:], lambda i: (layer,) + (0,) * nd,
                        pipeline_mode=pl.Buffered(1))


def _expand_mix(ws_ref, wk, rows, shift, nb):
    cl = GMLP_CHUNK
    e = (lax.broadcasted_iota(jnp.int32, (rows, cl), 0) >> shift
         == lax.broadcasted_iota(jnp.int32, (rows, cl), 1)).astype(BF16)
    tril = (lax.broadcasted_iota(jnp.int32, (cl, cl), 0)
            >= lax.broadcasted_iota(jnp.int32, (cl, cl), 1))
    for h in range(GMLP_HEADS):
        w = jnp.where(tril, ws_ref[h], 0.0).astype(BF16)
        rows_w = _dot(e, w).astype(BF16)
        for c0 in range(0, rows, MXU_DIM):
            et = (lax.broadcasted_iota(jnp.int32, (cl, MXU_DIM), 0)
                  == (lax.broadcasted_iota(jnp.int32, (cl, MXU_DIM), 1) + c0) >> shift).astype(BF16)
            full = _dot(rows_w, et)
            same_seq = ((lax.broadcasted_iota(jnp.int32, (rows, MXU_DIM), 0) & (nb - 1))
                        == ((lax.broadcasted_iota(jnp.int32, (rows, MXU_DIM), 1) + c0) & (nb - 1)))
            wk[h, :, c0:c0 + MXU_DIM] = jnp.where(same_seq, full, 0.0).astype(BF16)


def _expand_ssm(bbc_ref, ccc_ref, bb, cc):
    n_in, n_st = KT_COLS // SSM_STATE * SSM_GROUP_CH, KT_COLS
    iota = lambda shape, d: lax.broadcasted_iota(jnp.int32, shape, d)
    log_n, log_c = int(math.log2(SSM_STATE)), int(math.log2(SSM_GROUP_CH))
    spread = ((iota((SSM_STATE, n_st), 1) & (SSM_STATE - 1)) == iota((SSM_STATE, n_st), 0)).astype(BF16)
    same_in = (iota((n_in, n_st), 0) >> log_c) == (iota((n_in, n_st), 1) >> log_n)
    gather = ((iota((n_st, SSM_STATE), 0) & (SSM_STATE - 1)) == iota((n_st, SSM_STATE), 1)).astype(BF16)
    same_out = (iota((n_st, n_in), 0) >> log_n) == (iota((n_st, n_in), 1) >> log_c)
    for kt in range(SSM_KT):
        for part in range(2):
            full = _dot(bbc_ref[kt, part], spread)
            bb[kt, :, part * n_st:(part + 1) * n_st] = jnp.where(same_in, full, 0.0).astype(BF16)
            full = _dot(gather, ccc_ref[kt, part])
            cc[kt, part * n_st:(part + 1) * n_st, :] = jnp.where(same_out, full, 0.0).astype(BF16)


def _branches_kernel(x_ref, h0re_ref, h0im_ref, ppast_ref,
                     w_in_ref, bbc_ref, are_ref, aim_ref, ccc_ref, dsk_ref,
                     wglu_ref, bglu_ref, lng_ref, lnb_ref, ws_ref, bs_ref,
                     poolw_ref, pscale_ref,
                     ycat_ref, hcre, hcim, pool_out_ref,
                     *scratch, nb, npos, n_tiles, start_pos, seq_major_in, emit_vn):
    vn_out_ref = scratch[0] if emit_vn else None
    *hbufs, us, ya, gu, vnb, sacc, xp, wk, bb_ref, cc_ref = scratch[1:] if emit_vn else scratch
    rows = nb * npos
    n_chunks = rows // ROW_CHUNK
    past_rows = POOL_PAST * nb
    tile = pl.program_id(0)
    shift = int(math.log2(nb))
    slabs_per_block = rows // SUBLANES // SCAN_BLOCKS
    steps = min(npos, slabs_per_block)
    groups_per_block = slabs_per_block // steps
    assert groups_per_block == 1 or steps == npos
    mix_rows = rows // SCAN_BLOCKS

    @pl.when(tile == 0)
    def _init():
        hcre[...] = h0re_ref[...]
        hcim[...] = h0im_ref[...]
        xp[0:past_rows, :] = ppast_ref[...]
        _expand_mix(ws_ref, wk, rows, shift, nb)
        _expand_ssm(bbc_ref, ccc_ref, bb_ref, cc_ref)

    base_pos = start_pos + tile * npos

    def proj_chunk(c, carry):
        r0 = pl.multiple_of(c * PROJ_CHUNK, PROJ_CHUNK)
        xb = _load_rows_t_major(x_ref, r0, PROJ_CHUNK, nb, seq_major_in).astype(BF16)
        pa = _dot(xb, w_in_ref[...])
        us[pl.ds(r0, PROJ_CHUNK), :] = pa[:, 0:SSM_WIDTH]
        gu[pl.ds(r0, PROJ_CHUNK), :] = _gelu(pa[:, SSM_WIDTH:SSM_WIDTH + GMLP_WIDTH])
        v = _gelu(pa[:, SSM_WIDTH + GMLP_WIDTH:SSM_WIDTH + 2 * GMLP_WIDTH])
        v = _layer_norm(v, lng_ref[...], lnb_ref[...])
        if emit_vn:
            vn_out_ref[pl.ds(r0, PROJ_CHUNK), :] = v
        vnb[pl.ds(r0, PROJ_CHUNK), :] = v.astype(BF16)
        sacc[pl.ds(r0, PROJ_CHUNK), :] = bs_ref[pl.ds(r0, PROJ_CHUNK), :]
        xp[pl.ds(past_rows + r0, PROJ_CHUNK), :] = pa[:, SSM_WIDTH + 2 * GMLP_WIDTH:BRANCH_COLS]
        return carry

    lax.fori_loop(0, rows // PROJ_CHUNK, proj_chunk, 0)

    n_hbuf = len(hbufs)
    overlap = n_hbuf > 1

    def bu_rows(kt, r0):
        ub = us[pl.ds(r0, SSM_CHUNK), kt * MXU_DIM:(kt + 1) * MXU_DIM].astype(BF16)
        hbufs[kt % n_hbuf][pl.ds(r0, SSM_CHUNK), :] = _dot(ub, bb_ref[kt])

    def y_rows(kt, r0):
        hb = hbufs[kt % n_hbuf][pl.ds(r0, SSM_CHUNK), :].astype(BF16)
        u = us[pl.ds(r0, SSM_CHUNK), kt * MXU_DIM:(kt + 1) * MXU_DIM]
        y = _dot(hb, cc_ref[kt]) + dsk_ref[:, kt * MXU_DIM:(kt + 1) * MXU_DIM] * u
        ya[pl.ds(r0, SSM_CHUNK), kt * MXU_DIM:(kt + 1) * MXU_DIM] = y

    def chunk_loop(fn, kt):
        def body(c, carry):
            fn(kt, pl.multiple_of(c * SSM_CHUNK, SSM_CHUNK))
            return carry
        lax.fori_loop(0, rows // SSM_CHUNK, body, 0)

    if overlap:
        assert rows // SSM_CHUNK == SCAN_BLOCKS
        chunk_loop(bu_rows, 0)
    for kt in range(SSM_KT):
        c_lo = kt * KT_COLS
        hbuf = hbufs[kt % n_hbuf]
        if not overlap:
            chunk_loop(bu_rows, kt)

        width = KT_COLS // SCAN_PASSES
        for ch in range(SCAN_PASSES):
            lo = ch * width
            if overlap:
                heads_kt = GMLP_HEADS // SSM_KT
                mix_heads = range(kt * heads_kt, (kt + 1) * heads_kt) if ch == SCAN_PASSES - 1 else ()
            else:
                first = (kt * SCAN_PASSES + ch) * HEADS_PER_PASS
                mix_heads = range(first, first + HEADS_PER_PASS)
            a_r = jnp.broadcast_to(are_ref[:, c_lo + lo:c_lo + lo + width], (SUBLANES, width))
            a_i = jnp.broadcast_to(aim_ref[:, c_lo + lo:c_lo + lo + width], (SUBLANES, width))

            def scan_block(j, carry, kt=kt, hbuf=hbuf, ch=ch, lo=lo, a_r=a_r, a_i=a_i, c_lo=c_lo, mix_heads=mix_heads):
                for q in range(groups_per_block):
                    if groups_per_block == 1 and steps < npos:
                        t0, b0 = j * steps, 0
                    else:
                        t0, b0 = 0, (j * groups_per_block + q) * SUBLANES
                    h_r = hcre[pl.ds(b0, SUBLANES), c_lo + lo:c_lo + lo + width]
                    h_i = hcim[pl.ds(b0, SUBLANES), c_lo + lo:c_lo + lo + width]
                    for t in range(steps):
                        row = (t0 + t) * nb + b0
                        b_r = hbuf[pl.ds(row, SUBLANES), lo:lo + width]
                        b_i = hbuf[pl.ds(row, SUBLANES), KT_COLS + lo:KT_COLS + lo + width]
                        h_r, h_i = (a_r * h_r - a_i * h_i + b_r,
                                    a_r * h_i + a_i * h_r + b_i)
                        hbuf[pl.ds(row, SUBLANES), lo:lo + width] = h_r
                        hbuf[pl.ds(row, SUBLANES), KT_COLS + lo:KT_COLS + lo + width] = h_i
                    hcre[pl.ds(b0, SUBLANES), c_lo + lo:c_lo + lo + width] = h_r
                    hcim[pl.ds(b0, SUBLANES), c_lo + lo:c_lo + lo + width] = h_i
                if mix_heads:
                    m0 = j * mix_rows
                    head = lax.broadcasted_iota(jnp.int32, (1, GMLP_WIDTH), 1) // GMLP_HEAD_DIM
                    acc = sacc[pl.ds(m0, mix_rows), :]
                    for mh in mix_heads:
                        k_hi = (j + 1) * mix_rows
                        mixed = _dot(wk[mh, pl.ds(m0, mix_rows), 0:k_hi], vnb[0:k_hi, :])
                        acc = acc + jnp.where(head == mh, mixed, 0.0)
                    sacc[pl.ds(m0, mix_rows), :] = acc
                if overlap and ch == 0:
                    r0 = j * SSM_CHUNK
                    if kt + 1 < SSM_KT:
                        bu_rows(kt + 1, r0)
                    if kt > 0:
                        y_rows(kt - 1, r0)
                return carry

            for j in range(SCAN_BLOCKS):
                scan_block(j, 0)

        if not overlap:
            chunk_loop(y_rows, kt)
    if overlap:
        chunk_loop(y_rows, SSM_KT - 1)

    def out_chunk(c, carry):
        r0 = pl.multiple_of(c * ROW_CHUNK, ROW_CHUNK)
        z = _gelu(ya[pl.ds(r0, ROW_CHUNK), :])
        o_a = z * _sigmoid(_dot(z.astype(BF16), wglu_ref[...]) + bglu_ref[...])
        ycat_ref[pl.ds(r0, ROW_CHUNK), 0:SSM_WIDTH] = o_a.astype(BF16)
        y_b = gu[pl.ds(r0, ROW_CHUNK), :] * sacc[pl.ds(r0, ROW_CHUNK), :]
        ycat_ref[pl.ds(r0, ROW_CHUNK), SSM_WIDTH:SSM_WIDTH + GMLP_WIDTH] = y_b.astype(BF16)

        ridx = lax.broadcasted_iota(jnp.int32, (ROW_CHUNK, 1), 0) + r0
        pos = base_pos + (ridx >> shift)
        for g, win in enumerate(POOL_WINDOWS):
            cl = g * POOL_GROUP_CH
            reach = (win - 1) * nb
            if reach < ROW_CHUNK:
                off = pl.multiple_of(past_rows - reach + r0, SUBLANES)
                ext = xp[pl.ds(off, ROW_CHUNK + reach), cl:cl + POOL_GROUP_CH]
                cur = ext[reach:, :]
                wsum, span = ext, 1
                while span < win:
                    n = wsum.shape[0] - span * nb
                    wsum = wsum[span * nb:, :] + wsum[:n, :]
                    span *= 2
            else:
                cur = xp[pl.ds(past_rows + r0, ROW_CHUNK), cl:cl + POOL_GROUP_CH]
                wsum = cur
                for j in range(1, win):
                    off = pl.multiple_of(past_rows - j * nb + r0, SUBLANES)
                    wsum = wsum + xp[pl.ds(off, ROW_CHUNK), cl:cl + POOL_GROUP_CH]
            cnt = jnp.minimum(pos + 1, win).astype(F32)
            d = wsum / cnt - cur
            y_c = _dot(d.astype(BF16), poolw_ref[g]) * pscale_ref[:, cl:cl + POOL_GROUP_CH]
            col = SSM_WIDTH + GMLP_WIDTH + cl
            ycat_ref[pl.ds(r0, ROW_CHUNK), col:col + POOL_GROUP_CH] = y_c.astype(BF16)
        return carry

    lax.fori_loop(0, n_chunks, out_chunk, 0)

    pool_out_ref[...] = xp[rows:rows + past_rows, :]
    if n_tiles > 1:
        xp[0:past_rows, :] = xp[rows:rows + past_rows, :]


def _branches_call(x, h0re, h0im, ppast, state_layer, p, layer, bs, *, nb, npos, start_pos, emit_vn):
    seq_major_in = x.ndim == 3
    n_rows = x.shape[0] * x.shape[1] if seq_major_in else x.shape[0]
    rows = nb * npos
    n_tiles = n_rows // rows
    past_rows = POOL_PAST * nb
    kern = functools.partial(_branches_kernel, nb=nb, npos=npos, n_tiles=n_tiles, start_pos=start_pos,
                             seq_major_in=seq_major_in, emit_vn=emit_vn)
    row_tile = lambda w: pl.BlockSpec((rows, w), lambda i: (i, 0))
    x_spec = pl.BlockSpec((nb, npos, D_MODEL), lambda i: (0, i, 0)) if seq_major_in else row_tile(D_MODEL)
    n_hbuf = SSM_KT if n_tiles > 1 else 1
    fixed = lambda r, w: pl.BlockSpec((r, w), lambda i: (0, 0))
    consts = (p['bbc'], p['a_re'], p['a_im'], p['ccc'], p['d_skip'],
              p['w_glu'], p['b_glu'], p['gln_g'], p['gln_b'], p['w_s'], bs, p['pool_w'], p['pool_scale'])
    return pl.pallas_call(
        kern,
        grid=(n_tiles,),
        in_specs=[x_spec] + [_layer_const(s, state_layer) for s in (h0re, h0im, ppast)]
                 + [_w_in_cols(p['w_in'], layer, 0)] + [_layer_const(c, layer) for c in consts],
        out_specs=[row_tile(BRANCH_COLS - GMLP_WIDTH), fixed(nb, SSM_COLS), fixed(nb, SSM_COLS),
                   fixed(past_rows, POOL_WIDTH)] + ([row_tile(GMLP_WIDTH)] if emit_vn else []),
        out_shape=[jax.ShapeDtypeStruct((n_rows, BRANCH_COLS - GMLP_WIDTH), BF16),
                   jax.ShapeDtypeStruct((nb, SSM_COLS), F32),
                   jax.ShapeDtypeStruct((nb, SSM_COLS), F32),
                   jax.ShapeDtypeStruct((past_rows, POOL_WIDTH), F32)]
                  + ([jax.ShapeDtypeStruct((n_rows, GMLP_WIDTH), F32)] if emit_vn else []),
        scratch_shapes=[pltpu.VMEM((rows, 2 * KT_COLS), F32)] * n_hbuf
                       + [pltpu.VMEM((rows, SSM_WIDTH), F32),
                        pltpu.VMEM((rows, SSM_WIDTH), F32),
                        pltpu.VMEM((rows, GMLP_WIDTH), F32),
                        pltpu.VMEM((rows, GMLP_WIDTH), BF16),
                        pltpu.VMEM((rows, GMLP_WIDTH), F32),
                        pltpu.VMEM((rows + past_rows, POOL_WIDTH), F32),
                        pltpu.VMEM((GMLP_HEADS, rows, rows), BF16),
                        pltpu.VMEM((SSM_KT, MXU_DIM, 2 * KT_COLS), BF16),
                        pltpu.VMEM((SSM_KT, 2 * KT_COLS, MXU_DIM), BF16)],
        compiler_params=pltpu.CompilerParams(dimension_semantics=("arbitrary",),
                                             vmem_limit_bytes=VMEM_LIMIT_BYTES),
        name="branches",
    )(x, h0re, h0im, ppast, p['w_in'], *consts)


def _merge_ffn_kernel(x_ref, ycat_ref, cpast_ref,
                      wga_ref, wgb_ref, bg_ref, wbr_ref, wo_ref, g1_ref, b1_ref,
                      wup_ref, cw_ref, cb_ref, wdn_ref, g2_ref, b2_ref,
                      x2_ref, cout_ref, tailbuf, act, *, nb, rows, seq_major_in, seq_major_out):
    tail = (FFN_CONV - 1) * nb
    assert rows >= tail

    @pl.when(pl.program_id(0) == 0)
    def _init():
        tailbuf[...] = cpast_ref[...]

    x = _load_rows_t_major(x_ref, 0, rows, nb, seq_major_in)
    xb = x.astype(BF16)
    yc = ycat_ref[...]
    bounds = (0, SSM_WIDTH, SSM_WIDTH + GMLP_WIDTH, BRANCH_COLS - GMLP_WIDTH)
    merged = None

    def gate_logits(c0, c1):
        parts = []
        for ref, base in ((wga_ref, 0), (wgb_ref, BRANCH_COLS)):
            lo, hi = max(c0, base) - base, min(c1, base + BRANCH_COLS) - base
            if lo < hi:
                parts.append(_dot(xb, ref[:, lo:hi]))
        return parts[0] if len(parts) == 1 else jnp.concatenate(parts, axis=1)

    for i in range(N_BRANCH):
        gate = _sigmoid(gate_logits(i * D_MODEL, (i + 1) * D_MODEL) + bg_ref[:, i * D_MODEL:(i + 1) * D_MODEL])
        br = _dot(yc[:, bounds[i]:bounds[i + 1]], wbr_ref[bounds[i]:bounds[i + 1], :])
        merged = gate * br if merged is None else merged + gate * br
    mix = _dot(merged.astype(BF16), wo_ref[...])
    x1 = _layer_norm(ALPHA * x + mix, g1_ref[...], b1_ref[...])

    x1b = x1.astype(BF16)
    for cc in range(D_FF // FF_COL_CHUNK):
        conv = []
        for part in range(2):
            lo = part * D_FF + cc * FF_COL_CHUNK
            cols = slice(lo, lo + FF_COL_CHUNK)
            h = _dot(x1b, wup_ref[:, cols])
            he = jnp.concatenate([tailbuf[:, cols], h], axis=0)
            tailbuf[:, cols] = h[rows - tail:rows, :]
            y = cb_ref[:, cols] + he[0:rows, :] * cw_ref[0:1, cols]
            y = y + he[nb:nb + rows, :] * cw_ref[1:2, cols]
            y = y + h * cw_ref[2:3, cols]
            conv.append(y)
        a = _gelu(conv[1]) * conv[0]
        act[:, cc * FF_COL_CHUNK:(cc + 1) * FF_COL_CHUNK] = a.astype(BF16)
    ff = _dot(act[...], wdn_ref[...])
    x2 = _layer_norm(ALPHA * x1 + ff, g2_ref[...], b2_ref[...])
    if seq_major_out:
        x2_ref[...] = jnp.swapaxes(x2.reshape(rows // nb, nb, D_MODEL), 0, 1)
    else:
        x2_ref[...] = x2
    cout_ref[...] = tailbuf[...]


def _merge_ffn_call(x, ycat, cpast, state_layer, p, layer, *, nb, rows, seq_major_out):
    seq_major_in = x.ndim == 3
    n_rows = ycat.shape[0]
    tail = (FFN_CONV - 1) * nb
    kern = functools.partial(_merge_ffn_kernel, nb=nb, rows=rows,
                             seq_major_in=seq_major_in, seq_major_out=seq_major_out)
    consts = (p['b_gate'], p['w_br'], p['w_o'], p['ln1_g'], p['ln1_b'],
              p['w_up'], p['conv_w'], p['conv_b'], p['w_down'], p['ln2_g'], p['ln2_b'])
    t_major_spec = pl.BlockSpec((rows, D_MODEL), lambda i: (i, 0))
    seq_major_spec = pl.BlockSpec((nb, rows // nb, D_MODEL), lambda i: (0, i, 0))
    return pl.pallas_call(
        kern,
        grid=(n_rows // rows,),
        in_specs=[seq_major_spec if seq_major_in else t_major_spec,
                  pl.BlockSpec((rows, BRANCH_COLS - GMLP_WIDTH), lambda i: (i, 0)),
                  _layer_const(cpast, state_layer),
                  _w_in_cols(p['w_in'], layer, 1), _w_in_cols(p['w_in'], layer, 2)]
                 + [_layer_const(c, layer) for c in consts],
        out_specs=[seq_major_spec if seq_major_out else t_major_spec,
                   pl.BlockSpec((tail, F2), lambda i: (0, 0))],
        out_shape=[jax.ShapeDtypeStruct((nb, n_rows // nb, D_MODEL) if seq_major_out else (n_rows, D_MODEL), F32),
                   jax.ShapeDtypeStruct((tail, F2), F32)],
        scratch_shapes=[pltpu.VMEM((tail, F2), F32),
                        pltpu.VMEM((rows, D_FF), BF16)],
        compiler_params=pltpu.CompilerParams(dimension_semantics=("arbitrary",),
                                             vmem_limit_bytes=VMEM_LIMIT_BYTES),
        name="merge_convffn",
    )(x, ycat, cpast, p['w_in'], p['w_in'], *consts)


def _ssm_discretise(a_re, a_im, log_step, b_re, b_im):
    dt = jnp.exp(log_step)[..., None]
    mag = jnp.exp(a_re * dt)
    ab_re = mag * jnp.cos(a_im * dt)
    ab_im = mag * jnp.sin(a_im * dt)
    den = jnp.square(a_re) + jnp.square(a_im)
    nr = ab_re - 1.0
    k_re = (nr * a_re + ab_im * a_im) / den
    k_im = (ab_im * a_re - nr * a_im) / den
    bb_re = k_re[..., None] * b_re - k_im[..., None] * b_im
    bb_im = k_re[..., None] * b_im + k_im[..., None] * b_re
    return ab_re, ab_im, bb_re, bb_im


def _compact_in(m):
    gpt = SSM_GROUPS // SSM_KT
    return jnp.transpose(m, (0, 1, 3, 2)).reshape(-1, SSM_KT, gpt * SSM_GROUP_CH, SSM_STATE)


def _compact_out(m):
    gpt = SSM_GROUPS // SSM_KT
    m = m.reshape(-1, SSM_KT, gpt, SSM_GROUP_CH, SSM_STATE)
    return jnp.transpose(m, (0, 1, 4, 2, 3)).reshape(-1, SSM_KT, SSM_STATE, gpt * SSM_GROUP_CH)


def _gmlp_bias_rows(b_s, cl, nb):
    b = jnp.transpose(b_s[:, :, :cl], (0, 2, 1))
    return jnp.repeat(jnp.repeat(b, GMLP_HEAD_DIM, axis=2), nb, axis=1)


def _t_major(a):
    return jnp.transpose(a, (1, 0, 2)).reshape(a.shape[0] * a.shape[1], a.shape[2])


def _seq_major(a, n_seq):
    return jnp.transpose(a.reshape(a.shape[0] // n_seq, n_seq, a.shape[1]), (1, 0, 2))


def _group_layer(x, st_re, st_im, st_pool, st_conv, state_layer, p, layer, bs, *, nb, npos, start_pos, ffn_rows,
                 seq_major_out=False, emit_vn=False):
    ycat, h_re, h_im, pool_new, *vn = _branches_call(x, st_re, st_im, st_pool, state_layer, p, layer, bs,
                                                      nb=nb, npos=npos, start_pos=start_pos, emit_vn=emit_vn)
    x2, conv_new = _merge_ffn_call(x, ycat, st_conv, state_layer, p, layer, nb=nb, rows=ffn_rows,
                                   seq_major_out=seq_major_out)
    return (x2, h_re, h_im, pool_new, conv_new, *vn)


def kernel(x_prompt, x_sample, state_ssm_re, state_ssm_im, state_pool, state_ffn_conv, w_in, b_gate, ssm_a_re, ssm_a_im, ssm_log_step, ssm_b_re, ssm_b_im, ssm_c_re, ssm_c_im, ssm_d, ssm_w_glu, ssm_b_glu, gmlp_ln_g, gmlp_ln_b, gmlp_w_s, gmlp_b_s, pool_w, pool_scale, w_br_ssm, w_br_gmlp, w_br_pool, w_o, ln1_g, ln1_b, ffn_w_up, ffn_conv_w, ffn_conv_b, ffn_w_down, ln2_g, ln2_b):
    n_p, t_p, _ = x_prompt.shape
    n_s, t_s, _ = x_sample.shape
    hp = x_prompt
    hs = _t_major(x_sample)
    npos_p = TILE_ROWS // n_p
    npos_s = TILE_ROWS // n_s
    assert npos_p == GMLP_CHUNK and npos_s == t_s and t_p % npos_p == 0

    row = lambda v: v.reshape(DEPTH, 1, -1)
    ab_re, ab_im, bb_re, bb_im = _ssm_discretise(ssm_a_re, ssm_a_im, ssm_log_step, ssm_b_re, ssm_b_im)
    p = dict(
        w_in=w_in.reshape(DEPTH * D_MODEL, -1).astype(BF16).reshape(DEPTH, D_MODEL, -1),
        b_gate=row(b_gate),
        bbc=jnp.stack([_compact_in(bb_re), _compact_in(bb_im)], axis=2).astype(BF16),
        a_re=row(ab_re), a_im=row(ab_im),
        ccc=jnp.stack([_compact_out(ssm_c_re), -_compact_out(ssm_c_im)], axis=2).astype(BF16),
        d_skip=row(ssm_d),
        w_glu=ssm_w_glu.astype(BF16), b_glu=row(ssm_b_glu),
        gln_g=row(gmlp_ln_g), gln_b=row(gmlp_ln_b),
        w_s=gmlp_w_s, pool_w=pool_w.astype(BF16), pool_scale=row(pool_scale),
        w_br=jnp.concatenate([w_br_ssm, w_br_gmlp, w_br_pool], axis=1).astype(BF16),
        w_o=w_o.astype(BF16), ln1_g=row(ln1_g), ln1_b=row(ln1_b),
        w_up=ffn_w_up.astype(BF16), conv_w=ffn_conv_w, conv_b=row(ffn_conv_b),
        w_down=ffn_w_down.astype(BF16), ln2_g=row(ln2_g), ln2_b=row(ln2_b),
    )
    bs_p = _gmlp_bias_rows(gmlp_b_s, npos_p, n_p)
    bs_s = _gmlp_bias_rows(gmlp_b_s, npos_s, n_s)
    zeros_p = (jnp.zeros((1, n_p, SSM_COLS), F32), jnp.zeros((1, n_p, SSM_COLS), F32),
               jnp.zeros((1, POOL_PAST * n_p, POOL_WIDTH), F32), jnp.zeros((1, (FFN_CONV - 1) * n_p, F2), F32))
    ssm_re_s = state_ssm_re.reshape(DEPTH, n_s, SSM_COLS)
    ssm_im_s = state_ssm_im.reshape(DEPTH, n_s, SSM_COLS)

    pool_t = jnp.transpose(state_pool, (0, 2, 1, 3)).reshape(DEPTH, POOL_PAST * n_s, POOL_WIDTH)
    conv_t = jnp.transpose(state_ffn_conv, (0, 2, 1, 3)).reshape(DEPTH, (FFN_CONV - 1) * n_s, F2)

    outs = [[] for _ in range(9)]
    for l in range(DEPTH):
        hp, a_re_n, a_im_n, pool_n, conv_n = _group_layer(
            hp, *zeros_p, 0, p, l, bs_p, nb=n_p, npos=npos_p, start_pos=0, ffn_rows=2 * ROW_CHUNK,
            seq_major_out=(l == DEPTH - 1))
        hs, b_re_n, b_im_n, pool_m, conv_m, v_n = _group_layer(
            hs, ssm_re_s, ssm_im_s, pool_t, conv_t, l, p, l, bs_s, nb=n_s, npos=npos_s, start_pos=PAST_LEN, ffn_rows=ROW_CHUNK, emit_vn=True)
        for o, v in zip(outs, (a_re_n, a_im_n, pool_n, conv_n, b_re_n, b_im_n, v_n, pool_m, conv_m)):
            o.append(v)

    p_re, p_im, p_pool, p_conv, s_re, s_im, s_v, s_pool, s_conv = (jnp.stack(o) for o in outs)
    ssm = lambda a, n_seq: a.reshape(DEPTH, n_seq, SSM_GROUPS, SSM_STATE)
    per_seq = lambda a, n_seq: jnp.transpose(a.reshape(DEPTH, -1, n_seq, a.shape[-1]), (0, 2, 1, 3))
    return (hp, _seq_major(hs, n_s),
            ssm(p_re, n_p), ssm(p_im, n_p), per_seq(p_pool, n_p), per_seq(p_conv, n_p),
            ssm(s_re, n_s), ssm(s_im, n_s), per_seq(s_v, n_s), per_seq(s_pool, n_s), per_seq(s_conv, n_s))
```

```python
import functools
import math

import jax
import jax.numpy as jnp
from jax import lax
from jax.experimental import pallas as pl
from jax.experimental.pallas import tpu as pltpu

F32 = jnp.float32
BF16 = jnp.bfloat16

D_MODEL = 1024
DEPTH = 2
PAST_LEN = 16384
SSM_GROUP_CH = 16
SSM_WIDTH = 512
SSM_GROUPS = 32
SSM_STATE = 64
SSM_COLS = SSM_GROUPS * SSM_STATE
GMLP_HEADS = 4
GMLP_CHUNK = 128
GMLP_WIDTH = 256
GMLP_HEAD_DIM = 64
POOL_WINDOWS = (2, 4, 8, 16)
POOL_WIDTH = 512
POOL_GROUP_CH = 128
POOL_PAST = 15
N_BRANCH = 3
BRANCH_COLS = SSM_WIDTH + 2 * GMLP_WIDTH + POOL_WIDTH
D_FF = 2816
F2 = 2 * D_FF
FFN_CONV = 3
LN_EPS = 1e-5
ALPHA = (2 * DEPTH) ** 0.25

SUBLANES = 8
MXU_DIM = 256
VMEM_LIMIT_BYTES = 60 * 1024 * 1024

TILE_ROWS = 1024
ROW_CHUNK = 256
PROJ_CHUNK = 512
SSM_CHUNK = 512
SCAN_PASSES = 1
SCAN_BLOCKS = 2
SSM_KT = SSM_WIDTH // MXU_DIM
KT_COLS = SSM_COLS // SSM_KT
FF_COL_CHUNK = MXU_DIM
HEADS_PER_PASS = GMLP_HEADS // (SSM_KT * SCAN_PASSES)
assert HEADS_PER_PASS * SSM_KT * SCAN_PASSES == GMLP_HEADS


def _gelu(x):
    c = math.sqrt(2.0 / math.pi)
    return 0.5 * x * (1.0 + jnp.tanh(c * (x + 0.044715 * (x * x * x))))


def _sigmoid(x):
    return 1.0 / (1.0 + jnp.exp(-x))


def _layer_norm(x, g, b):
    mu = jnp.mean(x, axis=-1, keepdims=True)
    xc = x - mu
    var = jnp.mean(xc * xc, axis=-1, keepdims=True)
    return xc * lax.rsqrt(var + LN_EPS) * g + b


def _dot(a, b):
    return jnp.dot(a, b, preferred_element_type=F32)


def _load_rows_t_major(x_ref, r0, n, nb, seq_major):
    if not seq_major:
        return x_ref[pl.ds(r0, n), :]
    p0 = r0 // nb if isinstance(r0, int) else pl.multiple_of(r0 >> int(math.log2(nb)), SUBLANES)
    blk = x_ref[:, pl.ds(p0, n // nb), :]
    return jnp.swapaxes(blk, 0, 1).reshape(n, blk.shape[-1])


def _w_in_cols(w_in, layer, block):
    return pl.BlockSpec((None, w_in.shape[1], BRANCH_COLS), lambda i: (layer, 0, block),
                        pipeline_mode=pl.Buffered(1))


def _layer_const(a, layer):
    nd = a.ndim - 1
    return pl.BlockSpec((None,) + a.shape[1:], lambda i: (layer,) + (0,) * nd,
                        pipeline_mode=pl.Buffered(1))


def _expand_mix(ws_ref, wk, rows, shift, nb):
    cl = GMLP_CHUNK
    e = (lax.broadcasted_iota(jnp.int32, (rows, cl), 0) >> shift
         == lax.broadcasted_iota(jnp.int32, (rows, cl), 1)).astype(BF16)
    tril = (lax.broadcasted_iota(jnp.int32, (cl, cl), 0)
            >= lax.broadcasted_iota(jnp.int32, (cl, cl), 1))
    for h in range(GMLP_HEADS):
        w = jnp.where(tril, ws_ref[h], 0.0).astype(BF16)
        rows_w = _dot(e, w).astype(BF16)
        for c0 in range(0, rows, MXU_DIM):
            et = (lax.broadcasted_iota(jnp.int32, (cl, MXU_DIM), 0)
                  == (lax.broadcasted_iota(jnp.int32, (cl, MXU_DIM), 1) + c0) >> shift).astype(BF16)
            full = _dot(rows_w, et)
            same_seq = ((lax.broadcasted_iota(jnp.int32, (rows, MXU_DIM), 0) & (nb - 1))
                        == ((lax.broadcasted_iota(jnp.int32, (rows, MXU_DIM), 1) + c0) & (nb - 1)))
            wk[h, :, c0:c0 + MXU_DIM] = jnp.where(same_seq, full, 0.0).astype(BF16)


def _expand_ssm(bbc_ref, ccc_ref, bb, cc):
    n_in, n_st = KT_COLS // SSM_STATE * SSM_GROUP_CH, KT_COLS
    iota = lambda shape, d: lax.broadcasted_iota(jnp.int32, shape, d)
    log_n, log_c = int(math.log2(SSM_STATE)), int(math.log2(SSM_GROUP_CH))
    spread = ((iota((SSM_STATE, n_st), 1) & (SSM_STATE - 1)) == iota((SSM_STATE, n_st), 0)).astype(BF16)
    same_in = (iota((n_in, n_st), 0) >> log_c) == (iota((n_in, n_st), 1) >> log_n)
    gather = ((iota((n_st, SSM_STATE), 0) & (SSM_STATE - 1)) == iota((n_st, SSM_STATE), 1)).astype(BF16)
    same_out = (iota((n_st, n_in), 0) >> log_n) == (iota((n_st, n_in), 1) >> log_c)
    for kt in range(SSM_KT):
        for part in range(2):
            full = _dot(bbc_ref[kt, part], spread)
            bb[kt, :, part * n_st:(part + 1) * n_st] = jnp.where(same_in, full, 0.0).astype(BF16)
            full = _dot(gather, ccc_ref[kt, part])
            cc[kt, part * n_st:(part + 1) * n_st, :] = jnp.where(same_out, full, 0.0).astype(BF16)


def _branches_kernel(x_ref, h0re_ref, h0im_ref, ppast_ref,
                     w_in_ref, bbc_ref, are_ref, aim_ref, ccc_ref, dsk_ref,
                     wglu_ref, bglu_ref, lng_ref, lnb_ref, ws_ref, bs_ref,
                     poolw_ref, pscale_ref,
                     ycat_ref, hcre, hcim, pool_out_ref,
                     *scratch, nb, npos, n_tiles, start_pos, seq_major_in, emit_vn):
    vn_out_ref = scratch[0] if emit_vn else None
    *hbufs, us, ya, gu, vnb, sacc, xp, wk, bb_ref, cc_ref = scratch[1:] if emit_vn else scratch
    rows = nb * npos
    n_chunks = rows // ROW_CHUNK
    past_rows = POOL_PAST * nb
    tile = pl.program_id(0)
    shift = int(math.log2(nb))
    slabs_per_block = rows // SUBLANES // SCAN_BLOCKS
    steps = min(npos, slabs_per_block)
    groups_per_block = slabs_per_block // steps
    assert groups_per_block == 1 or steps == npos
    mix_rows = rows // SCAN_BLOCKS

    @pl.when(tile == 0)
    def _init():
        hcre[...] = h0re_ref[...]
        hcim[...] = h0im_ref[...]
        xp[0:past_rows, :] = ppast_ref[...]
        _expand_mix(ws_ref, wk, rows, shift, nb)
        _expand_ssm(bbc_ref, ccc_ref, bb_ref, cc_ref)

    base_pos = start_pos + tile * npos

    def proj_chunk(c, carry):
        r0 = pl.multiple_of(c * PROJ_CHUNK, PROJ_CHUNK)
        xb = _load_rows_t_major(x_ref, r0, PROJ_CHUNK, nb, seq_major_in).astype(BF16)
        pa = _dot(xb, w_in_ref[...])
        us[pl.ds(r0, PROJ_CHUNK), :] = pa[:, 0:SSM_WIDTH]
        gu[pl.ds(r0, PROJ_CHUNK), :] = _gelu(pa[:, SSM_WIDTH:SSM_WIDTH + GMLP_WIDTH])
        v = _gelu(pa[:, SSM_WIDTH + GMLP_WIDTH:SSM_WIDTH + 2 * GMLP_WIDTH])
        v = _layer_norm(v, lng_ref[...], lnb_ref[...])
        if emit_vn:
            vn_out_ref[pl.ds(r0, PROJ_CHUNK), :] = v
        vnb[pl.ds(r0, PROJ_CHUNK), :] = v.astype(BF16)
        sacc[pl.ds(r0, PROJ_CHUNK), :] = bs_ref[pl.ds(r0, PROJ_CHUNK), :]
        xp[pl.ds(past_rows + r0, PROJ_CHUNK), :] = pa[:, SSM_WIDTH + 2 * GMLP_WIDTH:BRANCH_COLS]
        return carry

    lax.fori_loop(0, rows // PROJ_CHUNK, proj_chunk, 0)

    n_hbuf = len(hbufs)
    overlap = n_hbuf > 1

    def bu_rows(kt, r0):
        ub = us[pl.ds(r0, SSM_CHUNK), kt * MXU_DIM:(kt + 1) * MXU_DIM].astype(BF16)
        hbufs[kt % n_hbuf][pl.ds(r0, SSM_CHUNK), :] = _dot(ub, bb_ref[kt])

    def y_rows(kt, r0):
        hb = hbufs[kt % n_hbuf][pl.ds(r0, SSM_CHUNK), :].astype(BF16)
        u = us[pl.ds(r0, SSM_CHUNK), kt * MXU_DIM:(kt + 1) * MXU_DIM]
        y = _dot(hb, cc_ref[kt]) + dsk_ref[:, kt * MXU_DIM:(kt + 1) * MXU_DIM] * u
        ya[pl.ds(r0, SSM_CHUNK), kt * MXU_DIM:(kt + 1) * MXU_DIM] = y

    def chunk_loop(fn, kt):
        def body(c, carry):
            fn(kt, pl.multiple_of(c * SSM_CHUNK, SSM_CHUNK))
            return carry
        lax.fori_loop(0, rows // SSM_CHUNK, body, 0)

    if overlap:
        assert rows // SSM_CHUNK == SCAN_BLOCKS
        chunk_loop(bu_rows, 0)
    for kt in range(SSM_KT):
        c_lo = kt * KT_COLS
        hbuf = hbufs[kt % n_hbuf]
        if not overlap:
            chunk_loop(bu_rows, kt)

        width = KT_COLS // SCAN_PASSES
        for ch in range(SCAN_PASSES):
            lo = ch * width
            if overlap:
                heads_kt = GMLP_HEADS // SSM_KT
                mix_heads = range(kt * heads_kt, (kt + 1) * heads_kt) if ch == SCAN_PASSES - 1 else ()
            else:
                first = (kt * SCAN_PASSES + ch) * HEADS_PER_PASS
                mix_heads = range(first, first + HEADS_PER_PASS)
            a_r = jnp.broadcast_to(are_ref[:, c_lo + lo:c_lo + lo + width], (SUBLANES, width))
            a_i = jnp.broadcast_to(aim_ref[:, c_lo + lo:c_lo + lo + width], (SUBLANES, width))

            def scan_block(j, carry, kt=kt, hbuf=hbuf, ch=ch, lo=lo, a_r=a_r, a_i=a_i, c_lo=c_lo, mix_heads=mix_heads):
                for q in range(groups_per_block):
                    if groups_per_block == 1 and steps < npos:
                        t0, b0 = j * steps, 0
                    else:
                        t0, b0 = 0, (j * groups_per_block + q) * SUBLANES
                    h_r = hcre[pl.ds(b0, SUBLANES), c_lo + lo:c_lo + lo + width]
                    h_i = hcim[pl.ds(b0, SUBLANES), c_lo + lo:c_lo + lo + width]
                    for t in range(steps):
                        row = (t0 + t) * nb + b0
                        b_r = hbuf[pl.ds(row, SUBLANES), lo:lo + width]
                        b_i = hbuf[pl.ds(row, SUBLANES), KT_COLS + lo:KT_COLS + lo + width]
                        h_r, h_i = (a_r * h_r - a_i * h_i + b_r,
                                    a_r * h_i + a_i * h_r + b_i)
                        hbuf[pl.ds(row, SUBLANES), lo:lo + width] = h_r
                        hbuf[pl.ds(row, SUBLANES), KT_COLS + lo:KT_COLS + lo + width] = h_i
                    hcre[pl.ds(b0, SUBLANES), c_lo + lo:c_lo + lo + width] = h_r
                    hcim[pl.ds(b0, SUBLANES), c_lo + lo:c_lo + lo + width] = h_i
                if mix_heads:
                    m0 = j * mix_rows
                    head = lax.broadcasted_iota(jnp.int32, (1, GMLP_WIDTH), 1) // GMLP_HEAD_DIM
                    acc = sacc[pl.ds(m0, mix_rows), :]
                    for mh in mix_heads:
                        k_hi = (j + 1) * mix_rows
                        mixed = _dot(wk[mh, pl.ds(m0, mix_rows), 0:k_hi], vnb[0:k_hi, :])
                        acc = acc + jnp.where(head == mh, mixed, 0.0)
                    sacc[pl.ds(m0, mix_rows), :] = acc
                if overlap and ch == 0:
                    r0 = j * SSM_CHUNK
                    if kt + 1 < SSM_KT:
                        bu_rows(kt + 1, r0)
                    if kt > 0:
                        y_rows(kt - 1, r0)
                return carry

            for j in range(SCAN_BLOCKS):
                scan_block(j, 0)

        if not overlap:
            chunk_loop(y_rows, kt)
    if overlap:
        chunk_loop(y_rows, SSM_KT - 1)

    def out_chunk(c, carry):
        r0 = pl.multiple_of(c * ROW_CHUNK, ROW_CHUNK)
        z = _gelu(ya[pl.ds(r0, ROW_CHUNK), :])
        o_a = z * _sigmoid(_dot(z.astype(BF16), wglu_ref[...]) + bglu_ref[...])
        ycat_ref[pl.ds(r0, ROW_CHUNK), 0:SSM_WIDTH] = o_a.astype(BF16)
        y_b = gu[pl.ds(r0, ROW_CHUNK), :] * sacc[pl.ds(r0, ROW_CHUNK), :]
        ycat_ref[pl.ds(r0, ROW_CHUNK), SSM_WIDTH:SSM_WIDTH + GMLP_WIDTH] = y_b.astype(BF16)

        ridx = lax.broadcasted_iota(jnp.int32, (ROW_CHUNK, 1), 0) + r0
        pos = base_pos + (ridx >> shift)
        for g, win in enumerate(POOL_WINDOWS):
            cl = g * POOL_GROUP_CH
            reach = (win - 1) * nb
            if reach < ROW_CHUNK:
                off = pl.multiple_of(past_rows - reach + r0, SUBLANES)
                ext = xp[pl.ds(off, ROW_CHUNK + reach), cl:cl + POOL_GROUP_CH]
                cur = ext[reach:, :]
                wsum, span = ext, 1
                while span < win:
                    n = wsum.shape[0] - span * nb
                    wsum = wsum[span * nb:, :] + wsum[:n, :]
                    span *= 2
            else:
                cur = xp[pl.ds(past_rows + r0, ROW_CHUNK), cl:cl + POOL_GROUP_CH]
                wsum = cur
                for j in range(1, win):
                    off = pl.multiple_of(past_rows - j * nb + r0, SUBLANES)
                    wsum = wsum + xp[pl.ds(off, ROW_CHUNK), cl:cl + POOL_GROUP_CH]
            cnt = jnp.minimum(pos + 1, win).astype(F32)
            d = wsum / cnt - cur
            y_c = _dot(d.astype(BF16), poolw_ref[g]) * pscale_ref[:, cl:cl + POOL_GROUP_CH]
            col = SSM_WIDTH + GMLP_WIDTH + cl
            ycat_ref[pl.ds(r0, ROW_CHUNK), col:col + POOL_GROUP_CH] = y_c.astype(BF16)
        return carry

    lax.fori_loop(0, n_chunks, out_chunk, 0)

    pool_out_ref[...] = xp[rows:rows + past_rows, :]
    if n_tiles > 1:
        xp[0:past_rows, :] = xp[rows:rows + past_rows, :]


def _branches_call(x, h0re, h0im, ppast, state_layer, p, layer, bs, *, nb, npos, start_pos, emit_vn):
    seq_major_in = x.ndim == 3
    n_rows = x.shape[0] * x.shape[1] if seq_major_in else x.shape[0]
    rows = nb * npos
    n_tiles = n_rows // rows
    past_rows = POOL_PAST * nb
    kern = functools.partial(_branches_kernel, nb=nb, npos=npos, n_tiles=n_tiles, start_pos=start_pos,
                             seq_major_in=seq_major_in, emit_vn=emit_vn)
    row_tile = lambda w: pl.BlockSpec((rows, w), lambda i: (i, 0))
    x_spec = pl.BlockSpec((nb, npos, D_MODEL), lambda i: (0, i, 0)) if seq_major_in else row_tile(D_MODEL)
    n_hbuf = SSM_KT if n_tiles > 1 else 1
    fixed = lambda r, w: pl.BlockSpec((r, w), lambda i: (0, 0))
    consts = (p['bbc'], p['a_re'], p['a_im'], p['ccc'], p['d_skip'],
              p['w_glu'], p['b_glu'], p['gln_g'], p['gln_b'], p['w_s'], bs, p['pool_w'], p['pool_scale'])
    return pl.pallas_call(
        kern,
        grid=(n_tiles,),
        in_specs=[x_spec] + [_layer_const(s, state_layer) for s in (h0re, h0im, ppast)]
                 + [_w_in_cols(p['w_in'], layer, 0)] + [_layer_const(c, layer) for c in consts],
        out_specs=[row_tile(BRANCH_COLS - GMLP_WIDTH), fixed(nb, SSM_COLS), fixed(nb, SSM_COLS),
                   fixed(past_rows, POOL_WIDTH)] + ([row_tile(GMLP_WIDTH)] if emit_vn else []),
        out_shape=[jax.ShapeDtypeStruct((n_rows, BRANCH_COLS - GMLP_WIDTH), BF16),
                   jax.ShapeDtypeStruct((nb, SSM_COLS), F32),
                   jax.ShapeDtypeStruct((nb, SSM_COLS), F32),
                   jax.ShapeDtypeStruct((past_rows, POOL_WIDTH), F32)]
                  + ([jax.ShapeDtypeStruct((n_rows, GMLP_WIDTH), F32)] if emit_vn else []),
        scratch_shapes=[pltpu.VMEM((rows, 2 * KT_COLS), F32)] * n_hbuf
                       + [pltpu.VMEM((rows, SSM_WIDTH), F32),
                        pltpu.VMEM((rows, SSM_WIDTH), F32),
                        pltpu.VMEM((rows, GMLP_WIDTH), F32),
                        pltpu.VMEM((rows, GMLP_WIDTH), BF16),
                        pltpu.VMEM((rows, GMLP_WIDTH), F32),
                        pltpu.VMEM((rows + past_rows, POOL_WIDTH), F32),
                        pltpu.VMEM((GMLP_HEADS, rows, rows), BF16),
                        pltpu.VMEM((SSM_KT, MXU_DIM, 2 * KT_COLS), BF16),
                        pltpu.VMEM((SSM_KT, 2 * KT_COLS, MXU_DIM), BF16)],
        compiler_params=pltpu.CompilerParams(dimension_semantics=("arbitrary",),
                                             vmem_limit_bytes=VMEM_LIMIT_BYTES),
        name="branches",
    )(x, h0re, h0im, ppast, p['w_in'], *consts)


def _merge_ffn_kernel(x_ref, ycat_ref, cpast_ref,
                      wga_ref, wgb_ref, bg_ref, wbr_ref, wo_ref, g1_ref, b1_ref,
                      wup_ref, cw_ref, cb_ref, wdn_ref, g2_ref, b2_ref,
                      x2_ref, cout_ref, tailbuf, act, *, nb, rows, seq_major_in, seq_major_out):
    tail = (FFN_CONV - 1) * nb
    assert rows >= tail

    @pl.when(pl.program_id(0) == 0)
    def _init():
        tailbuf[...] = cpast_ref[...]

    x = _load_rows_t_major(x_ref, 0, rows, nb, seq_major_in)
    xb = x.astype(BF16)
    yc = ycat_ref[...]
    bounds = (0, SSM_WIDTH, SSM_WIDTH + GMLP_WIDTH, BRANCH_COLS - GMLP_WIDTH)
    merged = None

    def gate_logits(c0, c1):
        parts = []
        for ref, base in ((wga_ref, 0), (wgb_ref, BRANCH_COLS)):
            lo, hi = max(c0, base) - base, min(c1, base + BRANCH_COLS) - base
            if lo < hi:
                parts.append(_dot(xb, ref[:, lo:hi]))
        return parts[0] if len(parts) == 1 else jnp.concatenate(parts, axis=1)

    for i in range(N_BRANCH):
        gate = _sigmoid(gate_logits(i * D_MODEL, (i + 1) * D_MODEL) + bg_ref[:, i * D_MODEL:(i + 1) * D_MODEL])
        br = _dot(yc[:, bounds[i]:bounds[i + 1]], wbr_ref[bounds[i]:bounds[i + 1], :])
        merged = gate * br if merged is None else merged + gate * br
    mix = _dot(merged.astype(BF16), wo_ref[...])
    x1 = _layer_norm(ALPHA * x + mix, g1_ref[...], b1_ref[...])

    x1b = x1.astype(BF16)
    for cc in range(D_FF // FF_COL_CHUNK):
        conv = []
        for part in range(2):
            lo = part * D_FF + cc * FF_COL_CHUNK
            cols = slice(lo, lo + FF_COL_CHUNK)
            h = _dot(x1b, wup_ref[:, cols])
            he = jnp.concatenate([tailbuf[:, cols], h], axis=0)
            tailbuf[:, cols] = h[rows - tail:rows, :]
            y = cb_ref[:, cols] + he[0:rows, :] * cw_ref[0:1, cols]
            y = y + he[nb:nb + rows, :] * cw_ref[1:2, cols]
            y = y + h * cw_ref[2:3, cols]
            conv.append(y)
        a = _gelu(conv[1]) * conv[0]
        act[:, cc * FF_COL_CHUNK:(cc + 1) * FF_COL_CHUNK] = a.astype(BF16)
    ff = _dot(act[...], wdn_ref[...])
    x2 = _layer_norm(ALPHA * x1 + ff, g2_ref[...], b2_ref[...])
    if seq_major_out:
        x2_ref[...] = jnp.swapaxes(x2.reshape(rows // nb, nb, D_MODEL), 0, 1)
    else:
        x2_ref[...] = x2
    cout_ref[...] = tailbuf[...]


def _merge_ffn_call(x, ycat, cpast, state_layer, p, layer, *, nb, rows, seq_major_out):
    seq_major_in = x.ndim == 3
    n_rows = ycat.shape[0]
    tail = (FFN_CONV - 1) * nb
    kern = functools.partial(_merge_ffn_kernel, nb=nb, rows=rows,
                             seq_major_in=seq_major_in, seq_major_out=seq_major_out)
    consts = (p['b_gate'], p['w_br'], p['w_o'], p['ln1_g'], p['ln1_b'],
              p['w_up'], p['conv_w'], p['conv_b'], p['w_down'], p['ln2_g'], p['ln2_b'])
    t_major_spec = pl.BlockSpec((rows, D_MODEL), lambda i: (i, 0))
    seq_major_spec = pl.BlockSpec((nb, rows // nb, D_MODEL), lambda i: (0, i, 0))
    return pl.pallas_call(
        kern,
        grid=(n_rows // rows,),
        in_specs=[seq_major_spec if seq_major_in else t_major_spec,
                  pl.BlockSpec((rows, BRANCH_COLS - GMLP_WIDTH), lambda i: (i, 0)),
                  _layer_const(cpast, state_layer),
                  _w_in_cols(p['w_in'], layer, 1), _w_in_cols(p['w_in'], layer, 2)]
                 + [_layer_const(c, layer) for c in consts],
        out_specs=[seq_major_spec if seq_major_out else t_major_spec,
                   pl.BlockSpec((tail, F2), lambda i: (0, 0))],
        out_shape=[jax.ShapeDtypeStruct((nb, n_rows // nb, D_MODEL) if seq_major_out else (n_rows, D_MODEL), F32),
                   jax.ShapeDtypeStruct((tail, F2), F32)],
        scratch_shapes=[pltpu.VMEM((tail, F2), F32),
                        pltpu.VMEM((rows, D_FF), BF16)],
        compiler_params=pltpu.CompilerParams(dimension_semantics=("arbitrary",),
                                             vmem_limit_bytes=VMEM_LIMIT_BYTES,
                                             allow_input_fusion=[k in (10, 13) for k in range(5 + len(consts))]),
        name="merge_convffn",
    )(x, ycat, cpast, p['w_in'], p['w_in'], *consts)


def _ssm_discretise(a_re, a_im, log_step, b_re, b_im):
    dt = jnp.exp(log_step)[..., None]
    mag = jnp.exp(a_re * dt)
    ab_re = mag * jnp.cos(a_im * dt)
    ab_im = mag * jnp.sin(a_im * dt)
    den = jnp.square(a_re) + jnp.square(a_im)
    nr = ab_re - 1.0
    k_re = (nr * a_re + ab_im * a_im) / den
    k_im = (ab_im * a_re - nr * a_im) / den
    bb_re = k_re[..., None] * b_re - k_im[..., None] * b_im
    bb_im = k_re[..., None] * b_im + k_im[..., None] * b_re
    return ab_re, ab_im, bb_re, bb_im


def _compact_in(m):
    gpt = SSM_GROUPS // SSM_KT
    return jnp.transpose(m, (0, 1, 3, 2)).reshape(-1, SSM_KT, gpt * SSM_GROUP_CH, SSM_STATE)


def _compact_out(m):
    gpt = SSM_GROUPS // SSM_KT
    m = m.reshape(-1, SSM_KT, gpt, SSM_GROUP_CH, SSM_STATE)
    return jnp.transpose(m, (0, 1, 4, 2, 3)).reshape(-1, SSM_KT, SSM_STATE, gpt * SSM_GROUP_CH)


def _gmlp_bias_rows(b_s, cl, nb):
    b = jnp.transpose(b_s[:, :, :cl], (0, 2, 1))
    return jnp.repeat(jnp.repeat(b, GMLP_HEAD_DIM, axis=2), nb, axis=1)


def _t_major(a):
    return jnp.transpose(a, (1, 0, 2)).reshape(a.shape[0] * a.shape[1], a.shape[2])


def _seq_major(a, n_seq):
    return jnp.transpose(a.reshape(a.shape[0] // n_seq, n_seq, a.shape[1]), (1, 0, 2))


def _group_layer(x, st_re, st_im, st_pool, st_conv, state_layer, p, layer, bs, *, nb, npos, start_pos, ffn_rows,
                 seq_major_out=False, emit_vn=False):
    ycat, h_re, h_im, pool_new, *vn = _branches_call(x, st_re, st_im, st_pool, state_layer, p, layer, bs,
                                                      nb=nb, npos=npos, start_pos=start_pos, emit_vn=emit_vn)
    x2, conv_new = _merge_ffn_call(x, ycat, st_conv, state_layer, p, layer, nb=nb, rows=ffn_rows,
                                   seq_major_out=seq_major_out)
    return (x2, h_re, h_im, pool_new, conv_new, *vn)


def kernel(x_prompt, x_sample, state_ssm_re, state_ssm_im, state_pool, state_ffn_conv, w_in, b_gate, ssm_a_re, ssm_a_im, ssm_log_step, ssm_b_re, ssm_b_im, ssm_c_re, ssm_c_im, ssm_d, ssm_w_glu, ssm_b_glu, gmlp_ln_g, gmlp_ln_b, gmlp_w_s, gmlp_b_s, pool_w, pool_scale, w_br_ssm, w_br_gmlp, w_br_pool, w_o, ln1_g, ln1_b, ffn_w_up, ffn_conv_w, ffn_conv_b, ffn_w_down, ln2_g, ln2_b):
    n_p, t_p, _ = x_prompt.shape
    n_s, t_s, _ = x_sample.shape
    hp = x_prompt
    hs = _t_major(x_sample)
    npos_p = TILE_ROWS // n_p
    npos_s = TILE_ROWS // n_s
    assert npos_p == GMLP_CHUNK and npos_s == t_s and t_p % npos_p == 0

    row = lambda v: v.reshape(DEPTH, 1, -1)
    ab_re, ab_im, bb_re, bb_im = _ssm_discretise(ssm_a_re, ssm_a_im, ssm_log_step, ssm_b_re, ssm_b_im)
    p = dict(
        w_in=w_in.astype(BF16),
        b_gate=row(b_gate),
        bbc=jnp.stack([_compact_in(bb_re), _compact_in(bb_im)], axis=2).astype(BF16),
        a_re=row(ab_re), a_im=row(ab_im),
        ccc=jnp.stack([_compact_out(ssm_c_re), -_compact_out(ssm_c_im)], axis=2).astype(BF16),
        d_skip=row(ssm_d),
        w_glu=ssm_w_glu.astype(BF16), b_glu=row(ssm_b_glu),
        gln_g=row(gmlp_ln_g), gln_b=row(gmlp_ln_b),
        w_s=gmlp_w_s, pool_w=pool_w.astype(BF16), pool_scale=row(pool_scale),
        w_br=jnp.concatenate([w_br_ssm, w_br_gmlp, w_br_pool], axis=1).astype(BF16),
        w_o=w_o.astype(BF16), ln1_g=row(ln1_g), ln1_b=row(ln1_b),
        w_up=ffn_w_up.astype(BF16), conv_w=ffn_conv_w, conv_b=row(ffn_conv_b),
        w_down=ffn_w_down.astype(BF16), ln2_g=row(ln2_g), ln2_b=row(ln2_b),
    )
    bs_p = _gmlp_bias_rows(gmlp_b_s, npos_p, n_p)
    bs_s = _gmlp_bias_rows(gmlp_b_s, npos_s, n_s)
    zeros_p = (jnp.zeros((1, n_p, SSM_COLS), F32), jnp.zeros((1, n_p, SSM_COLS), F32),
               jnp.zeros((1, POOL_PAST * n_p, POOL_WIDTH), F32), jnp.zeros((1, (FFN_CONV - 1) * n_p, F2), F32))
    ssm_re_s = state_ssm_re.reshape(DEPTH, n_s, SSM_COLS)
    ssm_im_s = state_ssm_im.reshape(DEPTH, n_s, SSM_COLS)

    pool_t = jnp.transpose(state_pool, (0, 2, 1, 3)).reshape(DEPTH, POOL_PAST * n_s, POOL_WIDTH)
    conv_t = jnp.transpose(state_ffn_conv, (0, 2, 1, 3)).reshape(DEPTH, (FFN_CONV - 1) * n_s, F2)

    outs = [[] for _ in range(9)]
    for l in range(DEPTH):
        hp, a_re_n, a_im_n, pool_n, conv_n = _group_layer(
            hp, *zeros_p, 0, p, l, bs_p, nb=n_p, npos=npos_p, start_pos=0, ffn_rows=2 * ROW_CHUNK,
            seq_major_out=(l == DEPTH - 1))
        hs, b_re_n, b_im_n, pool_m, conv_m, v_n = _group_layer(
            hs, ssm_re_s, ssm_im_s, pool_t, conv_t, l, p, l, bs_s, nb=n_s, npos=npos_s, start_pos=PAST_LEN, ffn_rows=ROW_CHUNK, emit_vn=True)
        for o, v in zip(outs, (a_re_n, a_im_n, pool_n, conv_n, b_re_n, b_im_n, v_n, pool_m, conv_m)):
            o.append(v)

    p_re, p_im, p_pool, p_conv, s_re, s_im, s_v, s_pool, s_conv = (jnp.stack(o) for o in outs)
    ssm = lambda a, n_seq: a.reshape(DEPTH, n_seq, SSM_GROUPS, SSM_STATE)
    per_seq = lambda a, n_seq: jnp.transpose(a.reshape(DEPTH, -1, n_seq, a.shape[-1]), (0, 2, 1, 3))
    return (hp, _seq_major(hs, n_s),
            ssm(p_re, n_p), ssm(p_im, n_p), per_seq(p_pool, n_p), per_seq(p_conv, n_p),
            ssm(s_re, n_s), ssm(s_im, n_s), per_seq(s_v, n_s), per_seq(s_pool, n_s), per_seq(s_conv, n_s))
```
